```python
import jax
import jax.numpy as jnp
from jax import lax
import numpy as np

D_MODEL = 1024
BATCH = 16
SEQ = 2048
DEPTH = 2

MEM_LEN = 256
MIX_WIDTH = D_MODEL
A_WIDTH = MIX_WIDTH // 2
HGRN_HEAD_DIM = 128
HGRN_HEADS = A_WIDTH // HGRN_HEAD_DIM
HGRN_CHUNK = 64
B_WIDTH = MIX_WIDTH - A_WIDTH
CONV_WIDTH = 31
C_WIDTH = MIX_WIDTH // 2
MOBA_HEAD_DIM = 64
MOBA_HEADS = C_WIDTH // MOBA_HEAD_DIM
MOBA_BLOCK = 256
MOBA_TOPK = 3
MOBA_QCHUNK = 64
D_WIDTH = MIX_WIDTH - C_WIDTH
SGU_CHUNK = 128
SGU_GROUPS = 4
SGU_GROUP_DIM = D_WIDTH // SGU_GROUPS
XA_HEADS = 4
XA_HEAD_DIM = D_MODEL // XA_HEADS
FFN_HIDDEN = ((-(-8 * D_MODEL // 3) + 255) // 256) * 256
EVEN_IN = 4 * A_WIDTH + 2 * B_WIDTH
ODD_IN = 3 * C_WIDTH + 2 * D_WIDTH
N_EVEN = (DEPTH + 1) // 2
N_ODD = DEPTH // 2
EPS = 1e-6

kernel_name = 'hybrid_hgrn2_conv_moba_sgu_trunk'


def _rmsnorm(x, g):
    xf = x.astype(jnp.float32)
    y = xf * lax.rsqrt(jnp.mean(xf * xf, axis=-1, keepdims=True) + EPS)
    return (y * g.astype(jnp.float32)).astype(x.dtype)


def _layernorm(x, g, b):
    xf = x.astype(jnp.float32)
    mu = jnp.mean(xf, axis=-1, keepdims=True)
    var = jnp.mean(jnp.square(xf - mu), axis=-1, keepdims=True)
    y = (xf - mu) * lax.rsqrt(var + EPS) * g.astype(jnp.float32) + b.astype(jnp.float32)
    return y.astype(x.dtype)


def _hgrn2_chunkwise(q, f, v):
    bsz, seq, _ = q.shape
    n_chunks = seq // HGRN_CHUNK

    def heads(t):
        t = t.astype(jnp.float32).reshape(bsz, n_chunks, HGRN_CHUNK, HGRN_HEADS, HGRN_HEAD_DIM)
        return t.transpose(1, 0, 3, 2, 4)

    qf, ff, vf = heads(q), heads(f), heads(v)
    cum = jnp.cumsum(jnp.log(ff), axis=3)
    kf = 1.0 - ff
    q_in = qf * jnp.exp(cum)
    k_in = kf * jnp.exp(-cum)
    causal = jnp.tril(jnp.ones((HGRN_CHUNK, HGRN_CHUNK), dtype=bool))
    att = jnp.where(causal, jnp.einsum('nbhcd,nbhsd->nbhcs', q_in, k_in), 0.0)
    o_intra = jnp.einsum('nbhcs,nbhse->nbhce', att, vf)
    cum_last = cum[..., -1:, :]
    chunk_kv = jnp.einsum('nbhsd,nbhse->nbhde', kf * jnp.exp(cum_last - cum), vf)
    decay = jnp.exp(cum_last[..., 0, :])

    def step(state, xs):
        q_n, kv_n, d_n = xs
        o_n = jnp.einsum('bhcd,bhde->bhce', q_n, state)
        return d_n[..., None] * state + kv_n, o_n

    state0 = jnp.zeros((bsz, HGRN_HEADS, HGRN_HEAD_DIM, HGRN_HEAD_DIM), jnp.float32)
    _, o_inter = lax.scan(step, state0, (q_in, chunk_kv, decay))
    o = (o_intra + o_inter).transpose(1, 0, 3, 2, 4)
    return o.reshape(bsz, seq, HGRN_HEADS, HGRN_HEAD_DIM)


def _causal_depthwise_conv(x, w, b):
    y = lax.conv_general_dilated(
        x, w[:, None, :], window_strides=(1,), padding=[(CONV_WIDTH - 1, 0)],
        dimension_numbers=('NWC', 'WIO', 'NWC'), feature_group_count=x.shape[-1])
    return y + b


def _mixer_hgrn2_conv(h, w_in, w_out, lb, out_norm, dw_w, dw_b, ln_g, ln_b):
    bsz, seq, _ = h.shape
    proj = h @ w_in
    qz, fz, iz, gz, glu_a, glu_b = jnp.split(
        proj, [A_WIDTH, 2 * A_WIDTH, 3 * A_WIDTH, 4 * A_WIDTH, 4 * A_WIDTH + B_WIDTH], axis=-1)
    f = lb + (1.0 - lb) * jax.nn.sigmoid(fz.astype(jnp.float32))
    o = _hgrn2_chunkwise(jax.nn.silu(qz), f, iz)
    o = o * lax.rsqrt(jnp.mean(o * o, axis=-1, keepdims=True) + EPS)
    o_a = (o.reshape(bsz, seq, A_WIDTH) * out_norm.astype(jnp.float32)
           * jax.nn.silu(gz.astype(jnp.float32))).astype(h.dtype)
    c = glu_a * jax.nn.sigmoid(glu_b)
    c = _causal_depthwise_conv(c, dw_w, dw_b)
    o_b = jax.nn.silu(_layernorm(c, ln_g, ln_b))
    return jnp.concatenate([o_a, o_b], axis=-1) @ w_out


def _moba_attention(q, k, v):
    bsz, seq, n_h, hd = q.shape
    n_blk = -(-seq // MOBA_BLOCK)
    s_pad = n_blk * MOBA_BLOCK
    pad = s_pad - seq
    if pad:
        cfg = ((0, 0), (0, pad), (0, 0), (0, 0))
        q, k, v = jnp.pad(q, cfg), jnp.pad(k, cfg), jnp.pad(v, cfg)
    qh = q.transpose(0, 2, 1, 3)
    kb = k.transpose(0, 2, 1, 3).reshape(bsz, n_h, n_blk, MOBA_BLOCK, hd)
    vb = v.transpose(0, 2, 1, 3).reshape(bsz, n_h, n_blk, MOBA_BLOCK, hd)
    topk = min(MOBA_TOPK, n_blk - 1)
    scale = MOBA_HEAD_DIM ** -0.5
    if topk > 0:
        k_mean = jnp.mean(kb.astype(jnp.float32), axis=3)
        blk_score = jnp.einsum('bhqd,bhnd->bhqn', qh.astype(jnp.float32), k_mean)
        q_blk = jnp.arange(s_pad) // MOBA_BLOCK
        fully_past = jnp.arange(n_blk)[None, :] < q_blk[:, None]
        blk_score = jnp.where(fully_past, blk_score, -jnp.inf)
        _, idx = lax.top_k(blk_score, topk)
        idx = idx.astype(jnp.int32)
    else:
        idx = jnp.zeros((bsz, n_h, s_pad, 0), jnp.int32)
    n_qc = s_pad // MOBA_QCHUNK
    h_ar = jnp.arange(n_h)[:, None, None]

    def per_batch(args):
        q_b, kb_b, vb_b, idx_b = args

        def per_chunk(c):
            start = c * MOBA_QCHUNK
            q_c = lax.dynamic_slice_in_dim(q_b, start, MOBA_QCHUNK, axis=1)
            jq = start // MOBA_BLOCK
            k_own = lax.dynamic_index_in_dim(kb_b, jq, axis=1, keepdims=False)
            v_own = lax.dynamic_index_in_dim(vb_b, jq, axis=1, keepdims=False)
            qpos = start + jnp.arange(MOBA_QCHUNK)
            kpos = jq * MOBA_BLOCK + jnp.arange(MOBA_BLOCK)
            s_own = jnp.einsum('hqd,hpd->hqp', q_c, k_own).astype(jnp.float32) * scale
            s_own = jnp.where(kpos[None, :] <= qpos[:, None], s_own, -jnp.inf)
            if topk > 0:
                idx_c = lax.dynamic_slice_in_dim(idx_b, start, MOBA_QCHUNK, axis=1)
                kg = kb_b[h_ar, idx_c]
                vg = vb_b[h_ar, idx_c]
                s_sel = jnp.einsum('hqd,hqkpd->hqkp', q_c, kg).astype(jnp.float32) * scale
                s_sel = jnp.where((idx_c < jq)[..., None], s_sel, -jnp.inf)
                s_sel = s_sel.reshape(n_h, MOBA_QCHUNK, topk * MOBA_BLOCK)
                p = jax.nn.softmax(jnp.concatenate([s_sel, s_own], axis=-1), axis=-1).astype(v_own.dtype)
                p_sel = p[..., :topk * MOBA_BLOCK].reshape(n_h, MOBA_QCHUNK, topk, MOBA_BLOCK)
                o = (jnp.einsum('hqkp,hqkpe->hqe', p_sel, vg)
                     + jnp.einsum('hqp,hpe->hqe', p[..., topk * MOBA_BLOCK:], v_own))
            else:
                p = jax.nn.softmax(s_own, axis=-1).astype(v_own.dtype)
                o = jnp.einsum('hqp,hpe->hqe', p, v_own)
            return o

        out = lax.map(per_chunk, jnp.arange(n_qc))
        return out.transpose(1, 0, 2, 3).reshape(n_h, s_pad, hd)

    out = lax.map(per_batch, (qh, kb, vb, idx))
    return out.transpose(0, 2, 1, 3)[:, :seq]


def _mixer_moba_sgu(h, w_in, w_out, sgu_ln_g, sgu_ln_b, sgu_w, sgu_b):
    bsz, seq, _ = h.shape
    proj = h @ w_in
    qz, kz, vz, uz, zz = jnp.split(
        proj, [C_WIDTH, 2 * C_WIDTH, 3 * C_WIDTH, 3 * C_WIDTH + D_WIDTH], axis=-1)
    hs = (bsz, seq, MOBA_HEADS, MOBA_HEAD_DIM)
    o_c = _moba_attention(qz.reshape(hs), kz.reshape(hs), vz.reshape(hs)).reshape(bsz, seq, C_WIDTH)
    u = jax.nn.gelu(uz, approximate=False)
    z = jax.nn.gelu(zz, approximate=False).reshape(bsz, seq, SGU_GROUPS, SGU_GROUP_DIM)
    z = _layernorm(z, sgu_ln_g.reshape(SGU_GROUPS, SGU_GROUP_DIM), sgu_ln_b.reshape(SGU_GROUPS, SGU_GROUP_DIM))
    zc = z.reshape(bsz, seq // SGU_CHUNK, SGU_CHUNK, SGU_GROUPS, SGU_GROUP_DIM)
    w_s = jnp.where(jnp.tril(jnp.ones((SGU_CHUNK, SGU_CHUNK), dtype=bool)), sgu_w, 0.0)
    mixed = jnp.einsum('gts,bnsgc->bntgc', w_s.astype(zc.dtype), zc) + sgu_b.T[None, None, :, :, None]
    o_d = u * mixed.reshape(bsz, seq, D_WIDTH)
    return jnp.concatenate([o_c, o_d], axis=-1) @ w_out


def _cross_attention(h, mem_n, wq, wkv, wo):
    bsz, seq, _ = h.shape
    q = (h @ wq).reshape(bsz, seq, XA_HEADS, XA_HEAD_DIM)
    k, v = jnp.split(mem_n @ wkv, 2, axis=-1)
    k = k.reshape(bsz, -1, XA_HEADS, XA_HEAD_DIM)
    v = v.reshape(bsz, -1, XA_HEADS, XA_HEAD_DIM)
    s = jnp.einsum('bshd,bmhd->bhsm', q, k).astype(jnp.float32) * (XA_HEAD_DIM ** -0.5)
    p = jax.nn.softmax(s, axis=-1).astype(v.dtype)
    o = jnp.einsum('bhsm,bmhd->bshd', p, v).reshape(bsz, seq, D_MODEL)
    return o @ wo


def _swiglu(h, w_in, w_out):
    a, g = jnp.split(h @ w_in, 2, axis=-1)
    return (jax.nn.silu(a) * g) @ w_out


def setup_inputs(seed: int = 0) -> dict:
    key = jax.random.key(seed)
    ks = jax.random.split(key, 32)

    def nrm(k, shape, scale):
        return jax.random.normal(k, shape, jnp.float32) * scale

    def gain(k, shape):
        return 1.0 + 0.02 * jax.random.normal(k, shape, jnp.float32)

    return {
        'x': nrm(ks[0], (BATCH, SEQ, D_MODEL), 1.0),
        'mem': nrm(ks[1], (BATCH, MEM_LEN, D_MODEL), 1.0),
        'norm_mix': gain(ks[2], (DEPTH, D_MODEL)),
        'norm_xattn': gain(ks[3], (DEPTH, D_MODEL)),
        'norm_ffn': gain(ks[4], (DEPTH, D_MODEL)),
        'mem_norm': gain(ks[5], (D_MODEL,)),
        'final_norm': gain(ks[6], (D_MODEL,)),
        'w_in_ab': nrm(ks[7], (N_EVEN, D_MODEL, EVEN_IN), D_MODEL ** -0.5),
        'w_out_ab': nrm(ks[8], (N_EVEN, MIX_WIDTH, D_MODEL), MIX_WIDTH ** -0.5),
        'hgrn_lower_bounds': nrm(ks[9], (DEPTH + 1, A_WIDTH), 0.1),
        'hgrn_out_norm': gain(ks[10], (N_EVEN, A_WIDTH)),
        'conv_dw_w': nrm(ks[11], (N_EVEN, CONV_WIDTH, B_WIDTH), CONV_WIDTH ** -0.5),
        'conv_dw_b': nrm(ks[12], (N_EVEN, B_WIDTH), 0.02),
        'conv_ln_g': gain(ks[13], (N_EVEN, B_WIDTH)),
        'conv_ln_b': nrm(ks[14], (N_EVEN, B_WIDTH), 0.02),
        'w_in_cd': nrm(ks[15], (N_ODD, D_MODEL, ODD_IN), D_MODEL ** -0.5),
        'w_out_cd': nrm(ks[16], (N_ODD, MIX_WIDTH, D_MODEL), MIX_WIDTH ** -0.5),
        'sgu_ln_g': gain(ks[17], (N_ODD, D_WIDTH)),
        'sgu_ln_b': nrm(ks[18], (N_ODD, D_WIDTH), 0.02),
        'sgu_w': nrm(ks[19], (N_ODD, SGU_GROUPS, SGU_CHUNK, SGU_CHUNK), SGU_CHUNK ** -0.5),
        'sgu_b': gain(ks[20], (N_ODD, SGU_GROUPS, SGU_CHUNK)),
        'xa_wq': nrm(ks[21], (DEPTH, D_MODEL, D_MODEL), D_MODEL ** -0.5),
        'xa_wkv': nrm(ks[22], (DEPTH, D_MODEL, 2 * D_MODEL), D_MODEL ** -0.5),
        'xa_wo': nrm(ks[23], (DEPTH, D_MODEL, D_MODEL), D_MODEL ** -0.5),
        'ffn_w_in': nrm(ks[24], (DEPTH, D_MODEL, 2 * FFN_HIDDEN), D_MODEL ** -0.5),
        'ffn_w_out': nrm(ks[25], (DEPTH, FFN_HIDDEN, D_MODEL), FFN_HIDDEN ** -0.5),
    }


def reference(x, mem, norm_mix, norm_xattn, norm_ffn, mem_norm, final_norm,
              w_in_ab, w_out_ab, hgrn_lower_bounds, hgrn_out_norm,
              conv_dw_w, conv_dw_b, conv_ln_g, conv_ln_b,
              w_in_cd, w_out_cd, sgu_ln_g, sgu_ln_b, sgu_w, sgu_b,
              xa_wq, xa_wkv, xa_wo, ffn_w_in, ffn_w_out):
    lb_all = jnp.cumsum(jax.nn.softmax(hgrn_lower_bounds.astype(jnp.float32), axis=0), axis=0)
    mem_n = _rmsnorm(mem, mem_norm)
    for l in range(DEPTH):
        h = _rmsnorm(x, norm_mix[l])
        if l % 2 == 0:
            e = l // 2
            x = x + _mixer_hgrn2_conv(h, w_in_ab[e], w_out_ab[e], lb_all[l], hgrn_out_norm[e],
                                      conv_dw_w[e], conv_dw_b[e], conv_ln_g[e], conv_ln_b[e])
        else:
            o = l // 2
            x = x + _mixer_moba_sgu(h, w_in_cd[o], w_out_cd[o], sgu_ln_g[o], sgu_ln_b[o],
                                    sgu_w[o], sgu_b[o])
        x = x + _cross_attention(_rmsnorm(x, norm_xattn[l]), mem_n, xa_wq[l], xa_wkv[l], xa_wo[l])
        x = x + _swiglu(_rmsnorm(x, norm_ffn[l]), ffn_w_in[l], ffn_w_out[l])
    return _rmsnorm(x, final_norm)
```

```python
import functools

import jax
import jax.numpy as jnp
from jax import lax
from jax.experimental import pallas as pl
from jax.experimental.pallas import tpu as pltpu

F32 = jnp.float32
BF16 = jnp.bfloat16
EPS = 1e-6
NEG = -1e30

V7X_VMEM_BYTES = 64 * 1024 * 1024
VMEM_LIMIT = V7X_VMEM_BYTES - 8 * 1024 * 1024
SUBLANES = 8

HGRN_HEAD_DIM = 128
HGRN_CHUNK = 64
CONV_TILE = 64
MOBA_HEAD_DIM = 64
MOBA_BLOCK = 256
MOBA_TOPK = 3
SGU_CHUNK = 128
SGU_GROUPS = 4
XA_HEADS = 4
ROW_TILE = 512
FFN_CHUNK = 256

_NT = (((1,), (1,)), ((), ()))
_TN = (((0,), (0,)), ((), ()))


def _params(*sem):
    return pltpu.CompilerParams(dimension_semantics=sem, vmem_limit_bytes=VMEM_LIMIT)


def _rms(x, g):
    return x * lax.rsqrt(jnp.mean(x * x, axis=-1, keepdims=True) + EPS) * g


def _sigmoid(x):
    return 1.0 / (1.0 + jnp.exp(-x))


def _silu(x):
    return x * _sigmoid(x)


def _gelu(x):
    return 0.5 * x * (1.0 + lax.erf(x * (2.0 ** -0.5)))


def _dot(a, b):
    return jnp.dot(a, b, preferred_element_type=F32)


def _norm_matmul_kernel(x_ref, g_ref, w_ref, o_ref):
    h = _rms(x_ref[...], g_ref[...]).astype(BF16)
    o_ref[...] = _dot(h, w_ref[...]).astype(o_ref.dtype)


def _norm_matmul(x2d, g, w, out_dtype):
    t, d = x2d.shape
    n = w.shape[1]
    return pl.pallas_call(
        _norm_matmul_kernel,
        grid=(t // ROW_TILE,),
        in_specs=[pl.BlockSpec((ROW_TILE, d), lambda i: (i, 0)),
                  pl.BlockSpec((1, d), lambda i: (0, 0)),
                  pl.BlockSpec((d, n), lambda i: (0, 0))],
        out_specs=pl.BlockSpec((ROW_TILE, n), lambda i: (i, 0)),
        out_shape=jax.ShapeDtypeStruct((t, n), out_dtype),
        compiler_params=_params("parallel"),
        name="norm_matmul",
    )(x2d, g.reshape(1, d), w)


def _hgrn_kernel(q_ref, f_ref, i_ref, g_ref, lbz_ref, on_ref, o_ref, *, layer):
    seq, dk = q_ref.shape[1], q_ref.shape[2]
    c = HGRN_CHUNK
    lbz = lbz_ref[...]
    e = jnp.exp(lbz - jnp.max(lbz, axis=0, keepdims=True))
    lb = jnp.sum(e[:layer + 1], axis=0, keepdims=True) / jnp.sum(e, axis=0, keepdims=True)
    on = on_ref[...]
    row = lax.broadcasted_iota(jnp.int32, (c, c), 0)
    col = lax.broadcasted_iota(jnp.int32, (c, c), 1)
    causal = col <= row
    tril = causal.astype(F32)

    def body(n, st):
        sl = pl.ds(pl.multiple_of(n * c, c), c)
        qz, fz, v, gz = q_ref[0, sl, :], f_ref[0, sl, :], i_ref[0, sl, :], g_ref[0, sl, :]
        f = lb + (1.0 - lb) * _sigmoid(fz)
        cum = jnp.dot(tril, jnp.log(f), precision=lax.Precision.HIGHEST,
                      preferred_element_type=F32)
        k = 1.0 - f
        q_in = (_silu(qz) * jnp.exp(cum)).astype(BF16)
        k_in = (k * jnp.exp(-cum)).astype(BF16)
        att = lax.dot_general(q_in, k_in, _NT, preferred_element_type=F32)
        att = jnp.where(causal, att, 0.0).astype(BF16)
        vb = v.astype(BF16)
        o = _dot(att, vb) + lax.dot_general(q_in, st.astype(BF16), _NT, preferred_element_type=F32)
        cl = cum[c - 1:c, :]
        kdec = (k * jnp.exp(cl - cum)).astype(BF16)
        st = jnp.exp(cl) * st + lax.dot_general(vb, kdec, _TN, preferred_element_type=F32)
        o = o * lax.rsqrt(jnp.mean(o * o, axis=-1, keepdims=True) + EPS)
        o_ref[0, sl, :] = (o * on * _silu(gz)).astype(o_ref.dtype)
        return st

    lax.fori_loop(0, seq // c, body, jnp.zeros((dk, dk), F32))


def _hgrn(proj, lbz, out_norm, layer, width):
    b, s, _ = proj.shape
    hd = HGRN_HEAD_DIM
    heads = width // hd
    sec = lambda k: pl.BlockSpec((1, s, hd), lambda i, h: (i, 0, k * heads + h))
    return pl.pallas_call(
        functools.partial(_hgrn_kernel, layer=layer),
        grid=(b, heads),
        in_specs=[sec(0), sec(1), sec(2), sec(3),
                  pl.BlockSpec((lbz.shape[0], hd), lambda i, h: (0, h)),
                  pl.BlockSpec((1, hd), lambda i, h: (0, h))],
        out_specs=pl.BlockSpec((1, s, hd), lambda i, h: (i, 0, h)),
        out_shape=jax.ShapeDtypeStruct((b, s, width), BF16),
        compiler_params=_params("parallel", "parallel"),
        name="hgrn2",
    )(proj, proj, proj, proj, lbz, out_norm.reshape(1, width))


def _conv_kernel(a_ref, b_ref, w_ref, db_ref, lg_ref, lb_ref, o_ref, cpad_ref):
    seq, ch = a_ref.shape[1], a_ref.shape[2]
    kw = w_ref.shape[0]
    pad = cpad_ref.shape[0] - seq
    tt = CONV_TILE
    cpad_ref[0:pad, :] = jnp.zeros((pad, ch), F32)

    def fill(t, carry):
        sl = pl.ds(pl.multiple_of(t * tt, tt), tt)
        cpad_ref[pl.ds(pl.multiple_of(pad + t * tt, SUBLANES), tt), :] = (
            a_ref[0, sl, :] * _sigmoid(b_ref[0, sl, :]))
        return carry

    lax.fori_loop(0, seq // tt, fill, 0)
    db, lg, lb = db_ref[...], lg_ref[...], lb_ref[...]

    def conv(t, carry):
        base = pl.multiple_of(t * tt, tt)
        win = cpad_ref[pl.ds(base, tt + pad), :]
        acc = jnp.zeros((tt, ch), F32)
        offs = [pad - kw + 1 + k for k in range(kw)]
        for r in range(SUBLANES):
            taps = [k for k in range(kw) if offs[k] % SUBLANES == r]
            if not taps:
                continue
            reach = max(offs[k] - r for k in taps)
            shifted = win[r:r + reach + tt, :]
            for k in taps:
                acc = acc + w_ref[k:k + 1, :] * shifted[offs[k] - r:offs[k] - r + tt, :]
        acc = acc + db
        mu = jnp.mean(acc, axis=-1, keepdims=True)
        d = acc - mu
        y = d * lax.rsqrt(jnp.mean(d * d, axis=-1, keepdims=True) + EPS) * lg + lb
        o_ref[0, pl.ds(base, tt), :] = _silu(y).astype(o_ref.dtype)
        return carry

    lax.fori_loop(0, seq // tt, conv, 0)


def _conv_module(proj, col_a, col_b, dw_w, dw_b, ln_g, ln_b):
    b, s, _ = proj.shape
    kw, ch = dw_w.shape
    pad = -(-(kw - 1) // SUBLANES) * SUBLANES
    vec = lambda: pl.BlockSpec((1, ch), lambda i: (0, 0))
    return pl.pallas_call(
        _conv_kernel,
        grid=(b,),
        in_specs=[pl.BlockSpec((1, s, ch), lambda i: (i, 0, col_a)),
                  pl.BlockSpec((1, s, ch), lambda i: (i, 0, col_b)),
                  pl.BlockSpec((kw, ch), lambda i: (0, 0)), vec(), vec(), vec()],
        out_specs=pl.BlockSpec((1, s, ch), lambda i: (i, 0, 0)),
        out_shape=jax.ShapeDtypeStruct((b, s, ch), BF16),
        scratch_shapes=[pltpu.VMEM((pad + s, ch), F32)],
        compiler_params=_params("parallel"),
        name="conv_module",
    )(proj, proj, dw_w, dw_b.reshape(1, ch), ln_g.reshape(1, ch), ln_b.reshape(1, ch))


def _moba_kernel(q_ref, k_ref, v_ref, o_ref, kaug_ref, vb_ref):
    seq, w = q_ref.shape[1], q_ref.shape[2]
    hd, blk, topk = MOBA_HEAD_DIM, MOBA_BLOCK, MOBA_TOPK
    nblk = seq // blk
    scale = hd ** -0.5
    k = k_ref[0]
    kmean = jnp.mean(k.reshape(nblk, blk, w), axis=1)
    lane_k = lax.broadcasted_iota(jnp.int32, (seq, w), 1)
    blk_k = lax.broadcasted_iota(jnp.int32, (seq, w), 0) // blk
    for e in range(2):
        in_head = (lane_k >= e * hd) & (lane_k < (e + 1) * hd)
        onehot = (lane_k - (1 - e) * hd == blk_k).astype(F32)
        kaug_ref[e] = jnp.where(in_head, k, onehot).astype(BF16)
    vb_ref[...] = v_ref[0].astype(BF16)

    lane_q = lax.broadcasted_iota(jnp.int32, (blk, w), 1)
    lane_m = lax.broadcasted_iota(jnp.int32, (nblk, w), 1)
    ridx = lax.broadcasted_iota(jnp.int32, (nblk, blk), 0)
    for jq in range(nblk):
        q2 = q_ref[0, jq * blk:(jq + 1) * blk, :]
        span = (jq + 1) * blk
        rl = lax.broadcasted_iota(jnp.int32, (blk, span), 0)
        cl = lax.broadcasted_iota(jnp.int32, (blk, span), 1)
        causal = cl - jq * blk <= rl
        outs = []
        for e in range(2):
            lo = (1 - e) * hd
            if jq > topk:
                kme = jnp.where((lane_m >= e * hd) & (lane_m < (e + 1) * hd), kmean, 0.0)
                st = lax.dot_general(kme, q2, _NT, precision=lax.Precision.HIGHEST,
                                     preferred_element_type=F32)
                valid = ridx < jq
                rows = []
                for n in range(nblk):
                    if n < jq:
                        sn = st[n:n + 1, :]
                        beats = valid & ((st > sn) | ((st == sn) & (ridx < n)))
                        rank = jnp.sum(beats.astype(F32), axis=0, keepdims=True)
                        rows.append(jnp.where(rank < topk, 0.0, NEG))
                    else:
                        rows.append(jnp.zeros((1, blk), F32))
                pieces = [jnp.concatenate(rows, axis=0)]
                if lo:
                    pieces.insert(0, jnp.zeros((lo, blk), F32))
                if w - lo - nblk:
                    pieces.append(jnp.zeros((w - lo - nblk, blk), F32))
                bias_q = jnp.concatenate(pieces, axis=0).T
            else:
                bias_q = jnp.zeros((blk, w), F32)
            in_head = (lane_q >= e * hd) & (lane_q < (e + 1) * hd)
            q_aug = jnp.where(in_head, q2 * scale, bias_q).astype(BF16)
            s = lax.dot_general(q_aug, kaug_ref[e, 0:span, :], _NT, preferred_element_type=F32)
            s = jnp.where(causal, s, NEG)
            p = jnp.exp(s - jnp.max(s, axis=-1, keepdims=True))
            p = p / jnp.sum(p, axis=-1, keepdims=True)
            outs.append(_dot(p.astype(BF16), vb_ref[0:span, :]))
        o_ref[0, jq * blk:(jq + 1) * blk, :] = jnp.where(lane_q < hd, outs[0], outs[1]).astype(o_ref.dtype)


def _moba(proj, width):
    b, s, _ = proj.shape
    w = 2 * MOBA_HEAD_DIM
    pairs = width // w
    sec = lambda k: pl.BlockSpec((1, s, w), lambda i, h: (i, 0, k * pairs + h))
    return pl.pallas_call(
        _moba_kernel,
        grid=(b, pairs),
        in_specs=[sec(0), sec(1), sec(2)],
        out_specs=pl.BlockSpec((1, s, w), lambda i, h: (i, 0, h)),
        out_shape=jax.ShapeDtypeStruct((b, s, width), BF16),
        scratch_shapes=[pltpu.VMEM((2, s, w), BF16), pltpu.VMEM((s, w), BF16)],
        compiler_params=_params("parallel", "parallel"),
        name="moba",
    )(proj, proj, proj)


def _sgu_kernel(u_ref, z_ref, lg_ref, lb_ref, w_ref, bias_ref, o_ref):
    ts, width = u_ref.shape[1], u_ref.shape[2]
    c = SGU_CHUNK
    gd = width // SGU_GROUPS
    row = lax.broadcasted_iota(jnp.int32, (c, c), 0)
    col = lax.broadcasted_iota(jnp.int32, (c, c), 1)
    for g in range(SGU_GROUPS):
        gs = slice(g * gd, (g + 1) * gd)
        z = _gelu(z_ref[0, :, gs])
        mu = jnp.mean(z, axis=-1, keepdims=True)
        d = z - mu
        zn = (d * lax.rsqrt(jnp.mean(d * d, axis=-1, keepdims=True) + EPS) * lg_ref[:, gs]
              + lb_ref[:, gs]).astype(BF16)
        wg = jnp.where(col <= row, w_ref[g], 0.0).astype(BF16)
        for n in range(ts // c):
            ts_ = slice(n * c, (n + 1) * c)
            mixed = _dot(wg, zn[ts_, :]) + bias_ref[:, gs]
            o_ref[0, ts_, gs] = (_gelu(u_ref[0, ts_, gs]) * mixed).astype(o_ref.dtype)


def _sgu(proj, col_u, col_z, ln_g, ln_b, w, bias):
    b, s, _ = proj.shape
    width = ln_g.shape[0]
    groups, c, _ = w.shape
    ts = ROW_TILE
    bias_full = jnp.repeat(bias.T, width // groups, axis=1)
    return pl.pallas_call(
        _sgu_kernel,
        grid=(b, s // ts),
        in_specs=[pl.BlockSpec((1, ts, width), lambda i, j: (i, j, col_u)),
                  pl.BlockSpec((1, ts, width), lambda i, j: (i, j, col_z)),
                  pl.BlockSpec((1, width), lambda i, j: (0, 0)),
                  pl.BlockSpec((1, width), lambda i, j: (0, 0)),
                  pl.BlockSpec((groups, c, c), lambda i, j: (0, 0, 0)),
                  pl.BlockSpec((c, width), lambda i, j: (0, 0))],
        out_specs=pl.BlockSpec((1, ts, width), lambda i, j: (i, j, 0)),
        out_shape=jax.ShapeDtypeStruct((b, s, width), BF16),
        compiler_params=_params("parallel", "parallel"),
        name="sgu",
    )(proj, proj, ln_g.reshape(1, width), ln_b.reshape(1, width), w, bias_full)


def _proj_res_kernel(x_ref, a_ref, b_ref, wa_ref, wb_ref, o_ref):
    o_ref[...] = x_ref[...] + _dot(a_ref[...], wa_ref[...]) + _dot(b_ref[...], wb_ref[...])


def _proj_res(x2d, a2d, b2d, w):
    t, d = x2d.shape
    ka, kb = a2d.shape[1], b2d.shape[1]
    assert ka == kb
    row = lambda n: pl.BlockSpec((ROW_TILE, n), lambda i: (i, 0))
    return pl.pallas_call(
        _proj_res_kernel,
        grid=(t // ROW_TILE,),
        in_specs=[row(d), row(ka), row(kb),
                  pl.BlockSpec((ka, d), lambda i: (0, 0)),
                  pl.BlockSpec((kb, d), lambda i: (1, 0))],
        out_specs=row(d),
        out_shape=jax.ShapeDtypeStruct((t, d), F32),
        compiler_params=_params("parallel"),
        name="mixer_out_proj",
    )(x2d, a2d, b2d, w, w)


def _xattn_kernel(x_ref, g_ref, wq_ref, kv_ref, wo_ref, o_ref):
    x = x_ref[...]
    d = x.shape[1]
    hd = d // XA_HEADS
    q = _dot(_rms(x, g_ref[...]).astype(BF16), wq_ref[...])
    outs = []
    for h in range(XA_HEADS):
        hs = slice(h * hd, (h + 1) * hd)
        s = lax.dot_general(q[:, hs].astype(BF16), kv_ref[0, :, hs], _NT,
                            preferred_element_type=F32) * (hd ** -0.5)
        p = jnp.exp(s - jnp.max(s, axis=-1, keepdims=True))
        p = p / jnp.sum(p, axis=-1, keepdims=True)
        outs.append(_dot(p.astype(BF16), kv_ref[0, :, d + h * hd:d + (h + 1) * hd]).astype(BF16))
    o_ref[...] = x + _dot(jnp.concatenate(outs, axis=-1), wo_ref[...])


def _xattn(x2d, g, wq, kv, wo, seq):
    t, d = x2d.shape
    m = kv.shape[1]
    per_batch = seq // ROW_TILE
    full = lambda: pl.BlockSpec((d, d), lambda i: (0, 0))
    return pl.pallas_call(
        _xattn_kernel,
        grid=(t // ROW_TILE,),
        in_specs=[pl.BlockSpec((ROW_TILE, d), lambda i: (i, 0)),
                  pl.BlockSpec((1, d), lambda i: (0, 0)), full(),
                  pl.BlockSpec((1, m, 2 * d), lambda i: (i // per_batch, 0, 0)), full()],
        out_specs=pl.BlockSpec((ROW_TILE, d), lambda i: (i, 0)),
        out_shape=jax.ShapeDtypeStruct((t, d), F32),
        compiler_params=_params("parallel"),
        name="cross_attention",
    )(x2d, g.reshape(1, d), wq, kv, wo)


def _ffn_kernel(x_ref, g_ref, w1_ref, w2_ref, fg_ref, o_ref, *, final):
    x = x_ref[...]
    hidden = w2_ref.shape[0]
    h = _rms(x, g_ref[...]).astype(BF16)
    acc = x
    for c in range(hidden // FFN_CHUNK):
        cs = slice(c * FFN_CHUNK, (c + 1) * FFN_CHUNK)
        gs = slice(hidden + c * FFN_CHUNK, hidden + (c + 1) * FFN_CHUNK)
        u = (_silu(_dot(h, w1_ref[:, cs])) * _dot(h, w1_ref[:, gs])).astype(BF16)
        acc = acc + _dot(u, w2_ref[cs, :])
    o_ref[...] = _rms(acc, fg_ref[...]) if final else acc


def _ffn(x2d, g, w1, w2, final_gain, final):
    t, d = x2d.shape
    hidden = w2.shape[0]
    vec = lambda: pl.BlockSpec((1, d), lambda i: (0, 0))
    resident = lambda shape: pl.BlockSpec(shape, lambda i: (0, 0), pipeline_mode=pl.Buffered(1))
    return pl.pallas_call(
        functools.partial(_ffn_kernel, final=final),
        grid=(t // ROW_TILE,),
        in_specs=[pl.BlockSpec((ROW_TILE, d), lambda i: (i, 0)), vec(),
                  resident((d, 2 * hidden)), resident((hidden, d)), vec()],
        out_specs=pl.BlockSpec((ROW_TILE, d), lambda i: (i, 0)),
        out_shape=jax.ShapeDtypeStruct((t, d), F32),
        compiler_params=_params("parallel"),
        name="swiglu_ffn",
    )(x2d, g.reshape(1, d), w1, w2, final_gain.reshape(1, d))


def kernel(x, mem, norm_mix, norm_xattn, norm_ffn, mem_norm, final_norm, w_in_ab, w_out_ab, hgrn_lower_bounds, hgrn_out_norm, conv_dw_w, conv_dw_b, conv_ln_g, conv_ln_b, w_in_cd, w_out_cd, sgu_ln_g, sgu_ln_b, sgu_w, sgu_b, xa_wq, xa_wkv, xa_wo, ffn_w_in, ffn_w_out):
    b, s, d = x.shape
    m = mem.shape[1]
    depth = norm_mix.shape[0]
    bf = lambda a: a.astype(BF16)
    x2d = x.reshape(b * s, d)
    mem2d = mem.reshape(b * m, d)
    for l in range(depth):
        if l % 2 == 0:
            e = l // 2
            a_width = hgrn_out_norm.shape[1]
            b_width = conv_dw_w.shape[2]
            proj = _norm_matmul(x2d, norm_mix[l], bf(w_in_ab[e]), F32).reshape(b, s, -1)
            o_a = _hgrn(proj, hgrn_lower_bounds, hgrn_out_norm[e], l, a_width)
            col = 4 * a_width // b_width
            o_b = _conv_module(proj, col, col + 1, conv_dw_w[e], conv_dw_b[e], conv_ln_g[e], conv_ln_b[e])
            w_out = bf(w_out_ab[e])
        else:
            o = l // 2
            d_width = sgu_ln_g.shape[1]
            c_width = w_out_cd.shape[1] - d_width
            proj = _norm_matmul(x2d, norm_mix[l], bf(w_in_cd[o]), F32).reshape(b, s, -1)
            o_a = _moba(proj, c_width)
            col = 3 * c_width // d_width
            o_b = _sgu(proj, col, col + 1, sgu_ln_g[o], sgu_ln_b[o], sgu_w[o], sgu_b[o])
            w_out = bf(w_out_cd[o])
        x2d = _proj_res(x2d, o_a.reshape(b * s, -1), o_b.reshape(b * s, -1), w_out)
        kv = _norm_matmul(mem2d, mem_norm, bf(xa_wkv[l]), BF16).reshape(b, m, 2 * d)
        x2d = _xattn(x2d, norm_xattn[l], bf(xa_wq[l]), kv, bf(xa_wo[l]), s)
        x2d = _ffn(x2d, norm_ffn[l], bf(ffn_w_in[l]), bf(ffn_w_out[l]), final_norm, l == depth - 1)
    return x2d.reshape(b, s, d)
```

```python
import functools

import jax
import jax.numpy as jnp
from jax import lax
from jax.experimental import pallas as pl
from jax.experimental.pallas import tpu as pltpu

F32 = jnp.float32
BF16 = jnp.bfloat16
EPS = 1e-6
NEG = -1e30
LOG2E = 1.4426950408889634

V7X_VMEM_BYTES = 64 * 1024 * 1024
VMEM_LIMIT = V7X_VMEM_BYTES - 8 * 1024 * 1024
SUBLANES = 8
LANES = 128

HGRN_HEAD_DIM = 128
HGRN_CHUNK = 64
HGRN_GROUP = 4
CONV_TILE = 64
CONV_NORM_ROWS = 256
MOBA_HEAD_DIM = 64
MOBA_BLOCK = 256
MOBA_TOPK = 3
SGU_CHUNK = 128
SGU_GROUPS = 4
XA_HEADS = 4
ROW_TILE = 512
FFN_CHUNK = 256

_NT = (((1,), (1,)), ((), ()))
_TN = (((0,), (0,)), ((), ()))


def _params(*sem):
    return pltpu.CompilerParams(dimension_semantics=sem, vmem_limit_bytes=VMEM_LIMIT)


def _rms(x, g):
    return x * lax.rsqrt(jnp.mean(x * x, axis=-1, keepdims=True) + EPS) * g


def _sigmoid(x):
    return 1.0 / (1.0 + jnp.exp(-x))


def _silu(x):
    return x * _sigmoid(x)


def _gelu(x):
    return 0.5 * x * (1.0 + lax.erf(x * (2.0 ** -0.5)))


def _dot(a, b):
    return jnp.dot(a, b, preferred_element_type=F32)


def _exact_dot01(m01, x):
    hi = x.astype(BF16)
    r1 = x - hi.astype(F32)
    mid = r1.astype(BF16)
    lo = (r1 - mid.astype(F32)).astype(BF16)
    return _dot(m01, hi) + _dot(m01, mid) + _dot(m01, lo)


def _norm_matmul_kernel(x_ref, g_ref, w_ref, o_ref):
    h = _rms(x_ref[...], g_ref[...]).astype(BF16)
    o_ref[...] = _dot(h, w_ref[...]).astype(o_ref.dtype)


def _norm_matmul(x2d, g, w, out_dtype):
    t, d = x2d.shape
    n = w.shape[1]
    return pl.pallas_call(
        _norm_matmul_kernel,
        grid=(t // ROW_TILE,),
        in_specs=[pl.BlockSpec((ROW_TILE, d), lambda i: (i, 0)),
                  pl.BlockSpec((1, d), lambda i: (0, 0)),
                  pl.BlockSpec((d, n), lambda i: (0, 0))],
        out_specs=pl.BlockSpec((ROW_TILE, n), lambda i: (i, 0)),
        out_shape=jax.ShapeDtypeStruct((t, n), out_dtype),
        compiler_params=_params("parallel"),
        name="norm_matmul",
    )(x2d, g.reshape(1, d), w)


def _hgrn_kernel(q_ref, f_ref, i_ref, g_ref, lbz_ref, on_ref, o_ref, *, layer):
    seq, dk = q_ref.shape[1], q_ref.shape[2]
    c, grp = HGRN_CHUNK, HGRN_GROUP
    rows = c * grp
    lbz = lbz_ref[...]
    e = jnp.exp(lbz - jnp.max(lbz, axis=0, keepdims=True))
    lb = jnp.sum(e[:layer + 1], axis=0, keepdims=True) / jnp.sum(e, axis=0, keepdims=True)
    on = on_ref[...]
    row = lax.broadcasted_iota(jnp.int32, (rows, rows), 0)
    col = lax.broadcasted_iota(jnp.int32, (rows, rows), 1)
    causal = (col <= row) & (col >= (row // c) * c)
    tril = causal.astype(BF16)

    def body(n, st):
        sl = pl.ds(pl.multiple_of(n * rows, rows), rows)
        qz, fz, v, gz = q_ref[0, sl, :], f_ref[0, sl, :], i_ref[0, sl, :], g_ref[0, sl, :]
        f = lb + (1.0 - lb) * _sigmoid(fz)
        cum = _exact_dot01(tril, jnp.log(f))
        k = 1.0 - f
        q_in = (_silu(qz) * jnp.exp(cum)).astype(BF16)
        k_in = (k * jnp.exp(-cum)).astype(BF16)
        att = lax.dot_general(q_in, k_in, _NT, preferred_element_type=F32)
        att = jnp.where(causal, att, 0.0).astype(BF16)
        vb = v.astype(BF16)
        o_intra = _dot(att, vb)
        cum3 = cum.reshape(grp, c, dk)
        cl = cum3[:, c - 1:c, :]
        kdec = (k.reshape(grp, c, dk) * jnp.exp(cl - cum3)).astype(BF16)
        decay = jnp.exp(cl)
        kv = [lax.dot_general(vb[j * c:(j + 1) * c], kdec[j], _TN, preferred_element_type=F32)
              for j in range(grp)]
        o_inter = []
        for j in range(grp):
            o_inter.append(lax.dot_general(q_in[j * c:(j + 1) * c], st.astype(BF16), _NT,
                                           preferred_element_type=F32))
            st = decay[j] * st + kv[j]
        o = o_intra + jnp.concatenate(o_inter, axis=0)
        o = o * lax.rsqrt(jnp.mean(o * o, axis=-1, keepdims=True) + EPS)
        o_ref[0, sl, :] = (o * on * _silu(gz)).astype(o_ref.dtype)
        return st

    lax.fori_loop(0, seq // rows, body, jnp.zeros((dk, dk), F32), unroll=2)


def _hgrn(proj, lbz, out_norm, layer, width):
    b, s, _ = proj.shape
    hd = HGRN_HEAD_DIM
    heads = width // hd
    sec = lambda k: pl.BlockSpec((1, s, hd), lambda i, h: (i, 0, k * heads + h))
    return pl.pallas_call(
        functools.partial(_hgrn_kernel, layer=layer),
        grid=(b, heads),
        in_specs=[sec(0), sec(1), sec(2), sec(3),
                  pl.BlockSpec((lbz.shape[0], hd), lambda i, h: (0, h)),
                  pl.BlockSpec((1, hd), lambda i, h: (0, h))],
        out_specs=pl.BlockSpec((1, s, hd), lambda i, h: (i, 0, h)),
        out_shape=jax.ShapeDtypeStruct((b, s, width), BF16),
        compiler_params=_params("parallel", "parallel"),
        name="hgrn2",
    )(proj, proj, proj, proj, lbz, out_norm.reshape(1, width))


def _conv_kernel(a_ref, b_ref, w_ref, db_ref, lg_ref, lb_ref, o_ref, cpad_ref, acc_ref):
    seq, ch = a_ref.shape[1], a_ref.shape[2]
    kw = w_ref.shape[0]
    pad = cpad_ref.shape[0] - seq
    tt = CONV_TILE
    cpad_ref[0:pad, :] = jnp.zeros((pad, ch), F32)

    def fill(t, carry):
        sl = pl.ds(pl.multiple_of(t * tt, tt), tt)
        cpad_ref[pl.ds(pl.multiple_of(pad + t * tt, SUBLANES), tt), :] = (
            a_ref[0, sl, :] * _sigmoid(b_ref[0, sl, :]))
        return carry

    lax.fori_loop(0, seq // tt, fill, 0)
    db, lg, lb = db_ref[...], lg_ref[...], lb_ref[...]

    offs = [pad - kw + 1 + k for k in range(kw)]
    sup = acc_ref.shape[0]
    lane_blocks = ch // LANES

    def conv(s, carry):
        sbase = pl.multiple_of(s * sup, sup)

        def block(i, c2):
            t, j = i // lane_blocks, i % lane_blocks
            base = pl.multiple_of(sbase + t * tt, tt)
            ls = pl.ds(pl.multiple_of(j * LANES, LANES), LANES)
            win = cpad_ref[pl.ds(base, tt + pad), ls]
            acc = jnp.zeros((tt, LANES), F32)
            for r in range(SUBLANES):
                taps = [k for k in range(kw) if offs[k] % SUBLANES == r]
                if not taps:
                    continue
                shifted = pltpu.roll(win, tt + pad - r, axis=0) if r else win
                for k in taps:
                    acc = acc + w_ref[k:k + 1, ls] * shifted[offs[k] - r:offs[k] - r + tt, :]
            acc_ref[pl.ds(pl.multiple_of(t * tt, tt), tt), ls] = acc
            return c2

        lax.fori_loop(0, (sup // tt) * lane_blocks, block, 0)
        acc = acc_ref[...] + db
        mu = jnp.mean(acc, axis=-1, keepdims=True)
        d = acc - mu
        y = d * lax.rsqrt(jnp.mean(d * d, axis=-1, keepdims=True) + EPS) * lg + lb
        o_ref[0, pl.ds(sbase, sup), :] = _silu(y).astype(o_ref.dtype)
        return carry

    lax.fori_loop(0, seq // sup, conv, 0)


def _conv_module(proj, col_a, col_b, dw_w, dw_b, ln_g, ln_b):
    b, s, _ = proj.shape
    kw, ch = dw_w.shape
    pad = -(-(kw - 1) // SUBLANES) * SUBLANES
    vec = lambda: pl.BlockSpec((1, ch), lambda i: (0, 0))
    return pl.pallas_call(
        _conv_kernel,
        grid=(b,),
        in_specs=[pl.BlockSpec((1, s, ch), lambda i: (i, 0, col_a)),
                  pl.BlockSpec((1, s, ch), lambda i: (i, 0, col_b)),
                  pl.BlockSpec((kw, ch), lambda i: (0, 0)), vec(), vec(), vec()],
        out_specs=pl.BlockSpec((1, s, ch), lambda i: (i, 0, 0)),
        out_shape=jax.ShapeDtypeStruct((b, s, ch), BF16),
        scratch_shapes=[pltpu.VMEM((pad + s, ch), F32), pltpu.VMEM((CONV_NORM_ROWS, ch), F32)],
        compiler_params=_params("parallel"),
        name="conv_module",
    )(proj, proj, dw_w, dw_b.reshape(1, ch), ln_g.reshape(1, ch), ln_b.reshape(1, ch))


def _moba_kernel(q_ref, k_ref, v_ref, o_ref, kaug_ref, vb_ref):
    seq, w = q_ref.shape[1], q_ref.shape[2]
    hd, blk, topk = MOBA_HEAD_DIM, MOBA_BLOCK, MOBA_TOPK
    nblk = seq // blk
    scale = hd ** -0.5
    k = k_ref[0]
    kmean = jnp.mean(k.reshape(nblk, blk, w), axis=1)
    lane_k = lax.broadcasted_iota(jnp.int32, (seq, w), 1)
    blk_k = lax.broadcasted_iota(jnp.int32, (seq, w), 0) // blk
    for e in range(2):
        in_head = (lane_k >= e * hd) & (lane_k < (e + 1) * hd)
        onehot = (lane_k - (1 - e) * hd == blk_k).astype(F32)
        kaug_ref[e] = jnp.where(in_head, k, onehot).astype(BF16)
    vb_ref[...] = v_ref[0].astype(BF16)

    lane_q = lax.broadcasted_iota(jnp.int32, (blk, w), 1)
    lane_m = lax.broadcasted_iota(jnp.int32, (nblk, w), 1)
    ridx = lax.broadcasted_iota(jnp.int32, (nblk, blk), 0)
    own_causal = (lax.broadcasted_iota(jnp.int32, (blk, blk), 1)
                  <= lax.broadcasted_iota(jnp.int32, (blk, blk), 0))

    def scores(jq, e):
        q2 = q_ref[0, jq * blk:(jq + 1) * blk, :]
        lo = (1 - e) * hd
        if jq > topk:
            kme = jnp.where((lane_m >= e * hd) & (lane_m < (e + 1) * hd), kmean, 0.0)
            st = lax.dot_general(kme, q2, _NT, precision=lax.Precision.HIGHEST,
                                 preferred_element_type=F32)
            valid = ridx < jq
            rows = []
            for n in range(nblk):
                if n < jq:
                    sn = st[n:n + 1, :]
                    beats = valid & ((st > sn) | ((st == sn) & (ridx < n)))
                    rank = jnp.sum(beats.astype(F32), axis=0, keepdims=True)
                    rows.append(jnp.where(rank < topk, 0.0, NEG))
                else:
                    rows.append(jnp.zeros((1, blk), F32))
            pieces = [jnp.concatenate(rows, axis=0)]
            if lo:
                pieces.insert(0, jnp.zeros((lo, blk), F32))
            if w - lo - nblk:
                pieces.append(jnp.zeros((w - lo - nblk, blk), F32))
            bias_q = jnp.concatenate(pieces, axis=0).T
        else:
            bias_q = jnp.zeros((blk, w), F32)
        in_head = (lane_q >= e * hd) & (lane_q < (e + 1) * hd)
        q_aug = jnp.where(in_head, q2 * (scale * LOG2E), bias_q).astype(BF16)
        s = lax.dot_general(q_aug, kaug_ref[e, 0:(jq + 1) * blk, :], _NT, preferred_element_type=F32)
        s_own = jnp.where(own_causal, s[:, jq * blk:], NEG)
        m = jnp.max(s_own, axis=-1, keepdims=True)
        if not jq:
            return (s_own,), m
        s_past = s[:, :jq * blk]
        return (s_past, s_own), jnp.maximum(m, jnp.max(s_past, axis=-1, keepdims=True))

    def attend(jq, parts, m):
        ps = [jnp.exp2(s - m) for s in parts]
        l = sum(jnp.sum(p, axis=-1, keepdims=True) for p in ps)
        p = jnp.concatenate(ps, axis=-1) if len(ps) > 1 else ps[0]
        return _dot(p.astype(BF16), vb_ref[0:(jq + 1) * blk, :]) * (1.0 / l)

    units = [(jq, e) for jq in range(nblk) for e in range(2)]
    pending = scores(*units[0])
    outs = {}
    for u, (jq, e) in enumerate(units):
        ahead = scores(*units[u + 1]) if u + 1 < len(units) else None
        outs[e] = attend(jq, *pending)
        if e == 1:
            o_ref[0, jq * blk:(jq + 1) * blk, :] = jnp.where(lane_q < hd, outs[0], outs[1]).astype(o_ref.dtype)
        pending = ahead


def _moba(proj, width):
    b, s, _ = proj.shape
    w = 2 * MOBA_HEAD_DIM
    pairs = width // w
    sec = lambda k: pl.BlockSpec((1, s, w), lambda i, h: (i, 0, k * pairs + h))
    return pl.pallas_call(
        _moba_kernel,
        grid=(b, pairs),
        in_specs=[sec(0), sec(1), sec(2)],
        out_specs=pl.BlockSpec((1, s, w), lambda i, h: (i, 0, h)),
        out_shape=jax.ShapeDtypeStruct((b, s, width), BF16),
        scratch_shapes=[pltpu.VMEM((2, s, w), BF16), pltpu.VMEM((s, w), BF16)],
        compiler_params=_params("parallel", "parallel"),
        name="moba",
    )(proj, proj, proj)


def _sgu_kernel(u_ref, z_ref, lg_ref, lb_ref, w_ref, bias_ref, o_ref):
    ts, width = u_ref.shape[1], u_ref.shape[2]
    c = SGU_CHUNK
    gd = width // SGU_GROUPS
    row = lax.broadcasted_iota(jnp.int32, (c, c), 0)
    col = lax.broadcasted_iota(jnp.int32, (c, c), 1)
    for g in range(SGU_GROUPS):
        gs = slice(g * gd, (g + 1) * gd)
        z = _gelu(z_ref[0, :, gs])
        mu = jnp.mean(z, axis=-1, keepdims=True)
        d = z - mu
        zn = (d * lax.rsqrt(jnp.mean(d * d, axis=-1, keepdims=True) + EPS) * lg_ref[:, gs]
              + lb_ref[:, gs]).astype(BF16)
        wg = jnp.where(col <= row, w_ref[g], 0.0).astype(BF16)
        for n in range(ts // c):
            ts_ = slice(n * c, (n + 1) * c)
            mixed = _dot(wg, zn[ts_, :]) + bias_ref[:, gs]
            o_ref[0, ts_, gs] = (_gelu(u_ref[0, ts_, gs]) * mixed).astype(o_ref.dtype)


def _sgu(proj, col_u, col_z, ln_g, ln_b, w, bias):
    b, s, _ = proj.shape
    width = ln_g.shape[0]
    groups, c, _ = w.shape
    ts = ROW_TILE
    bias_full = jnp.repeat(bias.T, width // groups, axis=1)
    return pl.pallas_call(
        _sgu_kernel,
        grid=(b, s // ts),
        in_specs=[pl.BlockSpec((1, ts, width), lambda i, j: (i, j, col_u)),
                  pl.BlockSpec((1, ts, width), lambda i, j: (i, j, col_z)),
                  pl.BlockSpec((1, width), lambda i, j: (0, 0)),
                  pl.BlockSpec((1, width), lambda i, j: (0, 0)),
                  pl.BlockSpec((groups, c, c), lambda i, j: (0, 0, 0)),
                  pl.BlockSpec((c, width), lambda i, j: (0, 0))],
        out_specs=pl.BlockSpec((1, ts, width), lambda i, j: (i, j, 0)),
        out_shape=jax.ShapeDtypeStruct((b, s, width), BF16),
        compiler_params=_params("parallel", "parallel"),
        name="sgu",
    )(proj, proj, ln_g.reshape(1, width), ln_b.reshape(1, width), w, bias_full)


def _proj_res_kernel(x_ref, a_ref, b_ref, wa_ref, wb_ref, o_ref):
    o_ref[...] = x_ref[...] + _dot(a_ref[...], wa_ref[...]) + _dot(b_ref[...], wb_ref[...])


def _proj_res(x2d, a2d, b2d, w):
    t, d = x2d.shape
    ka, kb = a2d.shape[1], b2d.shape[1]
    assert ka == kb
    row = lambda n: pl.BlockSpec((ROW_TILE, n), lambda i: (i, 0))
    return pl.pallas_call(
        _proj_res_kernel,
        grid=(t // ROW_TILE,),
        in_specs=[row(d), row(ka), row(kb),
                  pl.BlockSpec((ka, d), lambda i: (0, 0)),
                  pl.BlockSpec((kb, d), lambda i: (1, 0))],
        out_specs=row(d),
        out_shape=jax.ShapeDtypeStruct((t, d), F32),
        compiler_params=_params("parallel"),
        name="mixer_out_proj",
    )(x2d, a2d, b2d, w, w)


def _xattn_kernel(x_ref, g_ref, wq_ref, kv_ref, wo_ref, o_ref):
    x = x_ref[...]
    d = x.shape[1]
    hd = d // XA_HEADS
    q = _dot(_rms(x, g_ref[...]).astype(BF16), wq_ref[...])
    outs = []
    for h in range(XA_HEADS):
        hs = slice(h * hd, (h + 1) * hd)
        s = lax.dot_general(q[:, hs].astype(BF16), kv_ref[0, :, hs], _NT,
                            preferred_element_type=F32) * (hd ** -0.5)
        p = jnp.exp(s - jnp.max(s, axis=-1, keepdims=True))
        p = p / jnp.sum(p, axis=-1, keepdims=True)
        outs.append(_dot(p.astype(BF16), kv_ref[0, :, d + h * hd:d + (h + 1) * hd]).astype(BF16))
    o_ref[...] = x + _dot(jnp.concatenate(outs, axis=-1), wo_ref[...])


def _xattn(x2d, g, wq, kv, wo, seq):
    t, d = x2d.shape
    m = kv.shape[1]
    per_batch = seq // ROW_TILE
    full = lambda: pl.BlockSpec((d, d), lambda i: (0, 0))
    return pl.pallas_call(
        _xattn_kernel,
        grid=(t // ROW_TILE,),
        in_specs=[pl.BlockSpec((ROW_TILE, d), lambda i: (i, 0)),
                  pl.BlockSpec((1, d), lambda i: (0, 0)), full(),
                  pl.BlockSpec((1, m, 2 * d), lambda i: (i // per_batch, 0, 0)), full()],
        out_specs=pl.BlockSpec((ROW_TILE, d), lambda i: (i, 0)),
        out_shape=jax.ShapeDtypeStruct((t, d), F32),
        compiler_params=_params("parallel"),
        name="cross_attention",
    )(x2d, g.reshape(1, d), wq, kv, wo)


def _ffn_kernel(x_ref, g_ref, w1_ref, w2_ref, fg_ref, o_ref, *, final):
    x = x_ref[...]
    hidden = w2_ref.shape[0]
    h = _rms(x, g_ref[...]).astype(BF16)
    acc = x
    for c in range(hidden // FFN_CHUNK):
        cs = slice(c * FFN_CHUNK, (c + 1) * FFN_CHUNK)
        gs = slice(hidden + c * FFN_CHUNK, hidden + (c + 1) * FFN_CHUNK)
        u = (_silu(_dot(h, w1_ref[:, cs])) * _dot(h, w1_ref[:, gs])).astype(BF16)
        acc = acc + _dot(u, w2_ref[cs, :])
    o_ref[...] = _rms(acc, fg_ref[...]) if final else acc


def _ffn(x2d, g, w1, w2, final_gain, final):
    t, d = x2d.shape
    hidden = w2.shape[0]
    vec = lambda: pl.BlockSpec((1, d), lambda i: (0, 0))
    resident = lambda shape: pl.BlockSpec(shape, lambda i: (0, 0), pipeline_mode=pl.Buffered(1))
    return pl.pallas_call(
        functools.partial(_ffn_kernel, final=final),
        grid=(t // ROW_TILE,),
        in_specs=[pl.BlockSpec((ROW_TILE, d), lambda i: (i, 0)), vec(),
                  resident((d, 2 * hidden)), resident((hidden, d)), vec()],
        out_specs=pl.BlockSpec((ROW_TILE, d), lambda i: (i, 0)),
        out_shape=jax.ShapeDtypeStruct((t, d), F32),
        compiler_params=_params("parallel"),
        name="swiglu_ffn",
    )(x2d, g.reshape(1, d), w1, w2, final_gain.reshape(1, d))


def kernel(x, mem, norm_mix, norm_xattn, norm_ffn, mem_norm, final_norm, w_in_ab, w_out_ab, hgrn_lower_bounds, hgrn_out_norm, conv_dw_w, conv_dw_b, conv_ln_g, conv_ln_b, w_in_cd, w_out_cd, sgu_ln_g, sgu_ln_b, sgu_w, sgu_b, xa_wq, xa_wkv, xa_wo, ffn_w_in, ffn_w_out):
    b, s, d = x.shape
    m = mem.shape[1]
    depth = norm_mix.shape[0]
    bf = lambda a: a.astype(BF16)
    x2d = x.reshape(b * s, d)
    mem2d = mem.reshape(b * m, d)
    for l in range(depth):
        if l % 2 == 0:
            e = l // 2
            a_width = hgrn_out_norm.shape[1]
            b_width = conv_dw_w.shape[2]
            proj = _norm_matmul(x2d, norm_mix[l], bf(w_in_ab[e]), F32).reshape(b, s, -1)
            o_a = _hgrn(proj, hgrn_lower_bounds, hgrn_out_norm[e], l, a_width)
            col = 4 * a_width // b_width
            o_b = _conv_module(proj, col, col + 1, conv_dw_w[e], conv_dw_b[e], conv_ln_g[e], conv_ln_b[e])
            w_out = bf(w_out_ab[e])
        else:
            o = l // 2
            d_width = sgu_ln_g.shape[1]
            c_width = w_out_cd.shape[1] - d_width
            proj = _norm_matmul(x2d, norm_mix[l], bf(w_in_cd[o]), F32).reshape(b, s, -1)
            o_a = _moba(proj, c_width)
            col = 3 * c_width // d_width
            o_b = _sgu(proj, col, col + 1, sgu_ln_g[o], sgu_ln_b[o], sgu_w[o], sgu_b[o])
            w_out = bf(w_out_cd[o])
        x2d = _proj_res(x2d, o_a.reshape(b * s, -1), o_b.reshape(b * s, -1), w_out)
        kv = _norm_matmul(mem2d, mem_norm, bf(xa_wkv[l]), BF16).reshape(b, m, 2 * d)
        x2d = _xattn(x2d, norm_xattn[l], bf(xa_wq[l]), kv, bf(xa_wo[l]), s)
        x2d = _ffn(x2d, norm_ffn[l], bf(ffn_w_in[l]), bf(ffn_w_out[l]), final_norm, l == depth - 1)
    return x2d.reshape(b, s, d)
```

```python
import functools

import jax
import jax.numpy as jnp
from jax import lax
from jax.experimental import pallas as pl
from jax.experimental.pallas import tpu as pltpu

F32 = jnp.float32
BF16 = jnp.bfloat16
EPS = 1e-6
NEG = -1e30
LOG2E = 1.4426950408889634

V7X_VMEM_BYTES = 64 * 1024 * 1024
VMEM_LIMIT = V7X_VMEM_BYTES - 8 * 1024 * 1024
SUBLANES = 8
LANES = 128

HGRN_HEAD_DIM = 128
HGRN_CHUNK = 64
HGRN_GROUP = 4
HGRN_UNROLL = 4
CONV_TILE = 64
CONV_NORM_ROWS = 256
MOBA_HEAD_DIM = 64
MOBA_BLOCK = 256
MOBA_TOPK = 3
SGU_CHUNK = 128
SGU_GROUPS = 4
XA_HEADS = 4
ROW_TILE = 512
FFN_CHUNK = 256

_NT = (((1,), (1,)), ((), ()))
_TN = (((0,), (0,)), ((), ()))


def _params(*sem):
    return pltpu.CompilerParams(dimension_semantics=sem, vmem_limit_bytes=VMEM_LIMIT)


def _rms(x, g):
    return x * lax.rsqrt(jnp.mean(x * x, axis=-1, keepdims=True) + EPS) * g


def _sigmoid(x):
    return 1.0 / (1.0 + jnp.exp(-x))


def _silu(x):
    return x * _sigmoid(x)


def _gelu(x):
    return 0.5 * x * (1.0 + lax.erf(x * (2.0 ** -0.5)))


def _dot(a, b):
    return jnp.dot(a, b, preferred_element_type=F32)


def _exact_dot01(m01, x):
    hi = x.astype(BF16)
    r1 = x - hi.astype(F32)
    mid = r1.astype(BF16)
    lo = (r1 - mid.astype(F32)).astype(BF16)
    return _dot(m01, hi) + _dot(m01, mid) + _dot(m01, lo)


def _norm_matmul_kernel(x_ref, g_ref, w_ref, o_ref):
    h = _rms(x_ref[...], g_ref[...]).astype(BF16)
    o_ref[...] = _dot(h, w_ref[...]).astype(o_ref.dtype)


def _norm_matmul(x2d, g, w, out_dtype):
    t, d = x2d.shape
    n = w.shape[1]
    return pl.pallas_call(
        _norm_matmul_kernel,
        grid=(t // ROW_TILE,),
        in_specs=[pl.BlockSpec((ROW_TILE, d), lambda i: (i, 0)),
                  pl.BlockSpec((1, d), lambda i: (0, 0)),
                  pl.BlockSpec((d, n), lambda i: (0, 0))],
        out_specs=pl.BlockSpec((ROW_TILE, n), lambda i: (i, 0)),
        out_shape=jax.ShapeDtypeStruct((t, n), out_dtype),
        compiler_params=_params("parallel"),
        name="norm_matmul",
    )(x2d, g.reshape(1, d), w)


def _hgrn_kernel(q_ref, f_ref, i_ref, g_ref, lbz_ref, on_ref, o_ref, *, layer):
    seq, dk = q_ref.shape[1], q_ref.shape[2]
    c, grp = HGRN_CHUNK, HGRN_GROUP
    rows = c * grp
    lbz = lbz_ref[...]
    e = jnp.exp(lbz - jnp.max(lbz, axis=0, keepdims=True))
    lb = jnp.sum(e[:layer + 1], axis=0, keepdims=True) / jnp.sum(e, axis=0, keepdims=True)
    on = on_ref[...]
    row = lax.broadcasted_iota(jnp.int32, (rows, rows), 0)
    col = lax.broadcasted_iota(jnp.int32, (rows, rows), 1)
    causal = (col <= row) & (col >= (row // c) * c)
    tril = causal.astype(BF16)

    def body(n, st):
        sl = pl.ds(pl.multiple_of(n * rows, rows), rows)
        qz, fz, v, gz = q_ref[0, sl, :], f_ref[0, sl, :], i_ref[0, sl, :], g_ref[0, sl, :]
        f = lb + (1.0 - lb) * _sigmoid(fz)
        cum = _exact_dot01(tril, jnp.log(f))
        k = 1.0 - f
        q_in = (_silu(qz) * jnp.exp(cum)).astype(BF16)
        k_in = (k * jnp.exp(-cum)).astype(BF16)
        att = lax.dot_general(q_in, k_in, _NT, preferred_element_type=F32)
        att = jnp.where(causal, att, 0.0).astype(BF16)
        vb = v.astype(BF16)
        o_intra = _dot(att, vb)
        cum3 = cum.reshape(grp, c, dk)
        cl = cum3[:, c - 1:c, :]
        kdec = (k.reshape(grp, c, dk) * jnp.exp(cl - cum3)).astype(BF16)
        decay = jnp.exp(cl)
        kv = [lax.dot_general(vb[j * c:(j + 1) * c], kdec[j], _TN, preferred_element_type=F32)
              for j in range(grp)]
        o_inter = []
        for j in range(grp):
            o_inter.append(lax.dot_general(q_in[j * c:(j + 1) * c], st.astype(BF16), _NT,
                                           preferred_element_type=F32))
            st = decay[j] * st + kv[j]
        o = o_intra + jnp.concatenate(o_inter, axis=0)
        o = o * lax.rsqrt(jnp.mean(o * o, axis=-1, keepdims=True) + EPS)
        o_ref[0, sl, :] = (o * on * _silu(gz)).astype(o_ref.dtype)
        return st

    lax.fori_loop(0, seq // rows, body, jnp.zeros((dk, dk), F32), unroll=HGRN_UNROLL)


def _hgrn(proj, lbz, out_norm, layer, width):
    b, s, _ = proj.shape
    hd = HGRN_HEAD_DIM
    heads = width // hd
    sec = lambda k: pl.BlockSpec((1, s, hd), lambda i, h: (i, 0, k * heads + h))
    return pl.pallas_call(
        functools.partial(_hgrn_kernel, layer=layer),
        grid=(b, heads),
        in_specs=[sec(0), sec(1), sec(2), sec(3),
                  pl.BlockSpec((lbz.shape[0], hd), lambda i, h: (0, h)),
                  pl.BlockSpec((1, hd), lambda i, h: (0, h))],
        out_specs=pl.BlockSpec((1, s, hd), lambda i, h: (i, 0, h)),
        out_shape=jax.ShapeDtypeStruct((b, s, width), BF16),
        compiler_params=_params("parallel", "parallel"),
        name="hgrn2",
    )(proj, proj, proj, proj, lbz, out_norm.reshape(1, width))


def _conv_kernel(a_ref, b_ref, w_ref, db_ref, lg_ref, lb_ref, o_ref, cpad_ref, acc_ref):
    seq, ch = a_ref.shape[1], a_ref.shape[2]
    kw = w_ref.shape[0]
    pad = cpad_ref.shape[0] - seq
    tt = CONV_TILE
    cpad_ref[0:pad, :] = jnp.zeros((pad, ch), F32)

    def fill(t, carry):
        sl = pl.ds(pl.multiple_of(t * tt, tt), tt)
        cpad_ref[pl.ds(pl.multiple_of(pad + t * tt, SUBLANES), tt), :] = (
            a_ref[0, sl, :] * _sigmoid(b_ref[0, sl, :]))
        return carry

    lax.fori_loop(0, seq // tt, fill, 0)
    db, lg, lb = db_ref[...], lg_ref[...], lb_ref[...]

    offs = [pad - kw + 1 + k for k in range(kw)]
    sup = acc_ref.shape[0]
    lane_blocks = ch // LANES

    def conv(s, carry):
        sbase = pl.multiple_of(s * sup, sup)

        def block(i, c2):
            t, j = i // lane_blocks, i % lane_blocks
            base = pl.multiple_of(sbase + t * tt, tt)
            ls = pl.ds(pl.multiple_of(j * LANES, LANES), LANES)
            win = cpad_ref[pl.ds(base, tt + pad), ls]
            acc = jnp.zeros((tt, LANES), F32)
            for r in range(SUBLANES):
                taps = [k for k in range(kw) if offs[k] % SUBLANES == r]
                if not taps:
                    continue
                shifted = pltpu.roll(win, tt + pad - r, axis=0) if r else win
                for k in taps:
                    acc = acc + w_ref[k:k + 1, ls] * shifted[offs[k] - r:offs[k] - r + tt, :]
            acc_ref[pl.ds(pl.multiple_of(t * tt, tt), tt), ls] = acc
            return c2

        lax.fori_loop(0, (sup // tt) * lane_blocks, block, 0)
        acc = acc_ref[...] + db
        mu = jnp.mean(acc, axis=-1, keepdims=True)
        d = acc - mu
        y = d * lax.rsqrt(jnp.mean(d * d, axis=-1, keepdims=True) + EPS) * lg + lb
        o_ref[0, pl.ds(sbase, sup), :] = _silu(y).astype(o_ref.dtype)
        return carry

    lax.fori_loop(0, seq // sup, conv, 0)


def _conv_module(proj, col_a, col_b, dw_w, dw_b, ln_g, ln_b):
    b, s, _ = proj.shape
    kw, ch = dw_w.shape
    pad = -(-(kw - 1) // SUBLANES) * SUBLANES
    vec = lambda: pl.BlockSpec((1, ch), lambda i: (0, 0))
    return pl.pallas_call(
        _conv_kernel,
        grid=(b,),
        in_specs=[pl.BlockSpec((1, s, ch), lambda i: (i, 0, col_a)),
                  pl.BlockSpec((1, s, ch), lambda i: (i, 0, col_b)),
                  pl.BlockSpec((kw, ch), lambda i: (0, 0)), vec(), vec(), vec()],
        out_specs=pl.BlockSpec((1, s, ch), lambda i: (i, 0, 0)),
        out_shape=jax.ShapeDtypeStruct((b, s, ch), BF16),
        scratch_shapes=[pltpu.VMEM((pad + s, ch), F32), pltpu.VMEM((CONV_NORM_ROWS, ch), F32)],
        compiler_params=_params("parallel"),
        name="conv_module",
    )(proj, proj, dw_w, dw_b.reshape(1, ch), ln_g.reshape(1, ch), ln_b.reshape(1, ch))


def _moba_kernel(q_ref, k_ref, v_ref, o_ref, kaug_ref, vaug_ref):
    seq, w = q_ref.shape[1], q_ref.shape[2]
    hd, blk, topk = MOBA_HEAD_DIM, MOBA_BLOCK, MOBA_TOPK
    nblk = seq // blk
    scale = hd ** -0.5
    k = k_ref[0]
    v = v_ref[0]
    kmean = jnp.mean(k.reshape(nblk, blk, w), axis=1)
    lane_k = lax.broadcasted_iota(jnp.int32, (seq, w), 1)
    blk_k = lax.broadcasted_iota(jnp.int32, (seq, w), 0) // blk
    for e in range(2):
        in_head = (lane_k >= e * hd) & (lane_k < (e + 1) * hd)
        onehot = (lane_k - (1 - e) * hd == blk_k).astype(F32)
        kaug_ref[e] = jnp.where(in_head, k, onehot).astype(BF16)
        vaug_ref[e] = jnp.where(in_head, v, 1.0).astype(BF16)

    lane_q = lax.broadcasted_iota(jnp.int32, (blk, w), 1)
    lane_m = lax.broadcasted_iota(jnp.int32, (nblk, w), 1)
    ridx = lax.broadcasted_iota(jnp.int32, (nblk, blk), 0)
    own_causal = (lax.broadcasted_iota(jnp.int32, (blk, blk), 1)
                  <= lax.broadcasted_iota(jnp.int32, (blk, blk), 0))

    def scores(jq, e):
        q2 = q_ref[0, jq * blk:(jq + 1) * blk, :]
        lo = (1 - e) * hd
        if jq > topk:
            kme = jnp.where((lane_m >= e * hd) & (lane_m < (e + 1) * hd), kmean, 0.0)
            st = lax.dot_general(kme, q2, _NT, precision=lax.Precision.HIGHEST,
                                 preferred_element_type=F32)
            valid = ridx < jq
            rows = []
            for n in range(nblk):
                if n < jq:
                    sn = st[n:n + 1, :]
                    beats = valid & ((st > sn) | ((st == sn) & (ridx < n)))
                    rank = jnp.sum(beats.astype(F32), axis=0, keepdims=True)
                    rows.append(jnp.where(rank < topk, 0.0, NEG))
                else:
                    rows.append(jnp.zeros((1, blk), F32))
            pieces = [jnp.concatenate(rows, axis=0)]
            if lo:
                pieces.insert(0, jnp.zeros((lo, blk), F32))
            if w - lo - nblk:
                pieces.append(jnp.zeros((w - lo - nblk, blk), F32))
            bias_q = jnp.concatenate(pieces, axis=0).T
        else:
            bias_q = jnp.zeros((blk, w), F32)
        in_head = (lane_q >= e * hd) & (lane_q < (e + 1) * hd)
        q_aug = jnp.where(in_head, q2 * (scale * LOG2E), bias_q).astype(BF16)
        s = lax.dot_general(q_aug, kaug_ref[e, 0:(jq + 1) * blk, :], _NT, preferred_element_type=F32)
        s_own = jnp.where(own_causal, s[:, jq * blk:], NEG)
        m = jnp.max(s_own, axis=-1, keepdims=True)
        if not jq:
            return (s_own,), m
        s_past = s[:, :jq * blk]
        return (s_past, s_own), jnp.maximum(m, jnp.max(s_past, axis=-1, keepdims=True))

    def attend(jq, e, parts, m):
        ps = [jnp.exp2(s - m).astype(BF16) for s in parts]
        p = jnp.concatenate(ps, axis=-1) if len(ps) > 1 else ps[0]
        o = _dot(p, vaug_ref[e, 0:(jq + 1) * blk, :])
        lo = (1 - e) * hd
        return o * (1.0 / o[:, lo:lo + 1])

    units = [(jq, e) for jq in range(nblk) for e in range(2)]
    pending = scores(*units[0])
    outs = {}
    for u, (jq, e) in enumerate(units):
        ahead = scores(*units[u + 1]) if u + 1 < len(units) else None
        outs[e] = attend(jq, e, *pending)
        if e == 1:
            o_ref[0, jq * blk:(jq + 1) * blk, :] = jnp.where(lane_q < hd, outs[0], outs[1]).astype(o_ref.dtype)
        pending = ahead


def _moba(proj, width):
    b, s, _ = proj.shape
    w = 2 * MOBA_HEAD_DIM
    pairs = width // w
    sec = lambda k: pl.BlockSpec((1, s, w), lambda i, h: (i, 0, k * pairs + h))
    return pl.pallas_call(
        _moba_kernel,
        grid=(b, pairs),
        in_specs=[sec(0), sec(1), sec(2)],
        out_specs=pl.BlockSpec((1, s, w), lambda i, h: (i, 0, h)),
        out_shape=jax.ShapeDtypeStruct((b, s, width), BF16),
        scratch_shapes=[pltpu.VMEM((2, s, w), BF16), pltpu.VMEM((2, s, w), BF16)],
        compiler_params=_params("parallel", "parallel"),
        name="moba",
    )(proj, proj, proj)


def _sgu_kernel(u_ref, z_ref, lg_ref, lb_ref, w_ref, bias_ref, o_ref):
    ts, width = u_ref.shape[1], u_ref.shape[2]
    c = SGU_CHUNK
    gd = width // SGU_GROUPS
    row = lax.broadcasted_iota(jnp.int32, (c, c), 0)
    col = lax.broadcasted_iota(jnp.int32, (c, c), 1)
    for g in range(SGU_GROUPS):
        gs = slice(g * gd, (g + 1) * gd)
        z = _gelu(z_ref[0, :, gs])
        mu = jnp.mean(z, axis=-1, keepdims=True)
        d = z - mu
        zn = (d * lax.rsqrt(jnp.mean(d * d, axis=-1, keepdims=True) + EPS) * lg_ref[:, gs]
              + lb_ref[:, gs]).astype(BF16)
        wg = jnp.where(col <= row, w_ref[g], 0.0).astype(BF16)
        for n in range(ts // c):
            ts_ = slice(n * c, (n + 1) * c)
            mixed = _dot(wg, zn[ts_, :]) + bias_ref[:, gs]
            o_ref[0, ts_, gs] = (_gelu(u_ref[0, ts_, gs]) * mixed).astype(o_ref.dtype)


def _sgu(proj, col_u, col_z, ln_g, ln_b, w, bias):
    b, s, _ = proj.shape
    width = ln_g.shape[0]
    groups, c, _ = w.shape
    ts = ROW_TILE
    bias_full = jnp.repeat(bias.T, width // groups, axis=1)
    return pl.pallas_call(
        _sgu_kernel,
        grid=(b, s // ts),
        in_specs=[pl.BlockSpec((1, ts, width), lambda i, j: (i, j, col_u)),
                  pl.BlockSpec((1, ts, width), lambda i, j: (i, j, col_z)),
                  pl.BlockSpec((1, width), lambda i, j: (0, 0)),
                  pl.BlockSpec((1, width), lambda i, j: (0, 0)),
                  pl.BlockSpec((groups, c, c), lambda i, j: (0, 0, 0)),
                  pl.BlockSpec((c, width), lambda i, j: (0, 0))],
        out_specs=pl.BlockSpec((1, ts, width), lambda i, j: (i, j, 0)),
        out_shape=jax.ShapeDtypeStruct((b, s, width), BF16),
        compiler_params=_params("parallel", "parallel"),
        name="sgu",
    )(proj, proj, ln_g.reshape(1, width), ln_b.reshape(1, width), w, bias_full)


def _cross_attention(x, g, wq_ref, kv_ref, wo_ref):
    d = x.shape[1]
    hd = d // XA_HEADS
    q = _dot(_rms(x, g).astype(BF16), wq_ref[...])
    outs = []
    for h in range(XA_HEADS):
        hs = slice(h * hd, (h + 1) * hd)
        s = lax.dot_general(q[:, hs].astype(BF16), kv_ref[0, :, hs], _NT,
                            preferred_element_type=F32) * (hd ** -0.5)
        p = jnp.exp(s - jnp.max(s, axis=-1, keepdims=True))
        p = p / jnp.sum(p, axis=-1, keepdims=True)
        outs.append(_dot(p.astype(BF16), kv_ref[0, :, d + h * hd:d + (h + 1) * hd]).astype(BF16))
    return x + _dot(jnp.concatenate(outs, axis=-1), wo_ref[...])


def _swiglu(x, g, w1_ref, w2_ref):
    hidden = w2_ref.shape[0]
    h = _rms(x, g).astype(BF16)
    acc = x
    for c in range(hidden // FFN_CHUNK):
        cs = slice(c * FFN_CHUNK, (c + 1) * FFN_CHUNK)
        gs = slice(hidden + c * FFN_CHUNK, hidden + (c + 1) * FFN_CHUNK)
        u = (_silu(_dot(h, w1_ref[:, cs])) * _dot(h, w1_ref[:, gs])).astype(BF16)
        acc = acc + _dot(u, w2_ref[cs, :])
    return acc


def _post_mixer_kernel(x_ref, a_ref, b_ref, wa_ref, wb_ref, gx_ref, wq_ref, kv_ref, wo_ref,
                       gf_ref, w1_ref, w2_ref, fg_ref, o_ref, *, final):
    x = x_ref[...] + _dot(a_ref[...], wa_ref[...]) + _dot(b_ref[...], wb_ref[...])
    x = _cross_attention(x, gx_ref[...], wq_ref, kv_ref, wo_ref)
    x = _swiglu(x, gf_ref[...], w1_ref, w2_ref)
    o_ref[...] = _rms(x, fg_ref[...]) if final else x


def _post_mixer(x2d, a2d, b2d, w_mix, gx, wq, kv, wo, gf, w1, w2, final_gain, final, seq):
    t, d = x2d.shape
    ka, kb = a2d.shape[1], b2d.shape[1]
    assert ka == kb
    m = kv.shape[1]
    hidden = w2.shape[0]
    per_batch = seq // ROW_TILE
    row = lambda n: pl.BlockSpec((ROW_TILE, n), lambda i: (i, 0))
    vec = lambda: pl.BlockSpec((1, d), lambda i: (0, 0))
    resident = lambda shape, blk=0: pl.BlockSpec(shape, lambda i: (blk, 0), pipeline_mode=pl.Buffered(1))
    return pl.pallas_call(
        functools.partial(_post_mixer_kernel, final=final),
        grid=(t // ROW_TILE,),
        in_specs=[row(d), row(ka), row(kb), resident((ka, d)), resident((kb, d), 1),
                  vec(), resident((d, d)),
                  pl.BlockSpec((1, m, 2 * d), lambda i: (i // per_batch, 0, 0)), resident((d, d)),
                  vec(), resident((d, 2 * hidden)), resident((hidden, d)), vec()],
        out_specs=row(d),
        out_shape=jax.ShapeDtypeStruct((t, d), F32),
        compiler_params=_params("parallel"),
        name="post_mixer",
    )(x2d, a2d, b2d, w_mix, w_mix, gx.reshape(1, d), wq, kv, wo,
      gf.reshape(1, d), w1, w2, final_gain.reshape(1, d))


def kernel(x, mem, norm_mix, norm_xattn, norm_ffn, mem_norm, final_norm, w_in_ab, w_out_ab, hgrn_lower_bounds, hgrn_out_norm, conv_dw_w, conv_dw_b, conv_ln_g, conv_ln_b, w_in_cd, w_out_cd, sgu_ln_g, sgu_ln_b, sgu_w, sgu_b, xa_wq, xa_wkv, xa_wo, ffn_w_in, ffn_w_out):
    b, s, d = x.shape
    m = mem.shape[1]
    depth = norm_mix.shape[0]
    bf = lambda a: a.astype(BF16)
    x2d = x.reshape(b * s, d)
    mem2d = mem.reshape(b * m, d)
    for l in range(depth):
        if l % 2 == 0:
            e = l // 2
            a_width = hgrn_out_norm.shape[1]
            b_width = conv_dw_w.shape[2]
            proj = _norm_matmul(x2d, norm_mix[l], bf(w_in_ab[e]), F32).reshape(b, s, -1)
            o_a = _hgrn(proj, hgrn_lower_bounds, hgrn_out_norm[e], l, a_width)
            col = 4 * a_width // b_width
            o_b = _conv_module(proj, col, col + 1, conv_dw_w[e], conv_dw_b[e], conv_ln_g[e], conv_ln_b[e])
            w_out = bf(w_out_ab[e])
        else:
            o = l // 2
            d_width = sgu_ln_g.shape[1]
            c_width = w_out_cd.shape[1] - d_width
            proj = _norm_matmul(x2d, norm_mix[l], bf(w_in_cd[o]), F32).reshape(b, s, -1)
            o_a = _moba(proj, c_width)
            col = 3 * c_width // d_width
            o_b = _sgu(proj, col, col + 1, sgu_ln_g[o], sgu_ln_b[o], sgu_w[o], sgu_b[o])
            w_out = bf(w_out_cd[o])
        kv = _norm_matmul(mem2d, mem_norm, bf(xa_wkv[l]), BF16).reshape(b, m, 2 * d)
        x2d = _post_mixer(x2d, o_a.reshape(b * s, -1), o_b.reshape(b * s, -1), w_out,
                          norm_xattn[l], bf(xa_wq[l]), kv, bf(xa_wo[l]),
                          norm_ffn[l], bf(ffn_w_in[l]), bf(ffn_w_out[l]), final_norm, l == depth - 1, s)
    return x2d.reshape(b, s, d)
```

```python
import functools

import jax
import jax.numpy as jnp
from jax import lax
from jax.experimental import pallas as pl
from jax.experimental.pallas import tpu as pltpu

F32 = jnp.float32
BF16 = jnp.bfloat16
EPS = 1e-6
NEG = -1e30
LOG2E = 1.4426950408889634

V7X_VMEM_BYTES = 64 * 1024 * 1024
VMEM_LIMIT = V7X_VMEM_BYTES - 8 * 1024 * 1024
SUBLANES = 8
LANES = 128

HGRN_HEAD_DIM = 128
HGRN_CHUNK = 64
HGRN_GROUP = 4
HGRN_UNROLL = 8
CONV_TILE = 64
CONV_NORM_ROWS = 256
MOBA_HEAD_DIM = 64
MOBA_BLOCK = 256
MOBA_TOPK = 3
SGU_CHUNK = 128
SGU_GROUPS = 4
XA_HEADS = 4
ROW_TILE = 512
FFN_CHUNK = 256

_NT = (((1,), (1,)), ((), ()))
_TN = (((0,), (0,)), ((), ()))


def _params(*sem):
    return pltpu.CompilerParams(dimension_semantics=sem, vmem_limit_bytes=VMEM_LIMIT)


def _rms(x, g):
    return x * lax.rsqrt(jnp.mean(x * x, axis=-1, keepdims=True) + EPS) * g


def _sigmoid(x):
    return 1.0 / (1.0 + jnp.exp2(x * -LOG2E))


def _silu(x):
    return x * _sigmoid(x)


def _gelu(x):
    return 0.5 * x * (1.0 + lax.erf(x * (2.0 ** -0.5)))


def _dot(a, b):
    return jnp.dot(a, b, preferred_element_type=F32)


def _dot01_f32(m01, x):
    hi = x.astype(BF16)
    r1 = x - hi.astype(F32)
    mid = r1.astype(BF16)
    lo = (r1 - mid.astype(F32)).astype(BF16)
    return _dot(m01, hi) + _dot(m01, mid) + _dot(m01, lo)


def _norm_matmul_kernel(x_ref, g_ref, w_ref, o_ref):
    h = _rms(x_ref[...], g_ref[...]).astype(BF16)
    o_ref[...] = _dot(h, w_ref[...]).astype(o_ref.dtype)


def _norm_matmul(x2d, g, w, out_dtype):
    t, d = x2d.shape
    n = w.shape[1]
    return pl.pallas_call(
        _norm_matmul_kernel,
        grid=(t // ROW_TILE,),
        in_specs=[pl.BlockSpec((ROW_TILE, d), lambda i: (i, 0)),
                  pl.BlockSpec((1, d), lambda i: (0, 0)),
                  pl.BlockSpec((d, n), lambda i: (0, 0))],
        out_specs=pl.BlockSpec((ROW_TILE, n), lambda i: (i, 0)),
        out_shape=jax.ShapeDtypeStruct((t, n), out_dtype),
        compiler_params=_params("parallel"),
        name="norm_matmul",
    )(x2d, g.reshape(1, d), w)


def _hgrn_kernel(q_ref, f_ref, i_ref, g_ref, lbz_ref, on_ref, o_ref, *, layer):
    seq, dk = q_ref.shape[1], q_ref.shape[2]
    c, grp = HGRN_CHUNK, HGRN_GROUP
    rows = c * grp
    lbz = lbz_ref[...]
    e = jnp.exp(lbz - jnp.max(lbz, axis=0, keepdims=True))
    lb = jnp.sum(e[:layer + 1], axis=0, keepdims=True) / jnp.sum(e, axis=0, keepdims=True)
    on = on_ref[...]
    row = lax.broadcasted_iota(jnp.int32, (rows, rows), 0)
    col = lax.broadcasted_iota(jnp.int32, (rows, rows), 1)
    causal = (col <= row) & (col >= (row // c) * c)
    tril = causal.astype(BF16)

    def body(n, st):
        sl = pl.ds(pl.multiple_of(n * rows, rows), rows)
        qz, fz, v, gz = q_ref[0, sl, :], f_ref[0, sl, :], i_ref[0, sl, :], g_ref[0, sl, :]
        f = lb + (1.0 - lb) * _sigmoid(fz)
        cum = _dot01_f32(tril, jnp.log(f)) * LOG2E
        k = 1.0 - f
        q_in = (_silu(qz) * jnp.exp2(cum)).astype(BF16)
        k_in = (k * jnp.exp2(-cum)).astype(BF16)
        att = lax.dot_general(q_in, k_in, _NT, preferred_element_type=F32)
        att = jnp.where(causal, att, 0.0).astype(BF16)
        vb = v.astype(BF16)
        o_intra = _dot(att, vb)
        cum3 = cum.reshape(grp, c, dk)
        cl = cum3[:, c - 1:c, :]
        kdec = (k.reshape(grp, c, dk) * jnp.exp2(cl - cum3)).astype(BF16)
        decay = jnp.exp2(cl)
        kv = [lax.dot_general(vb[j * c:(j + 1) * c], kdec[j], _TN, preferred_element_type=F32)
              for j in range(grp)]
        o_inter = []
        for j in range(grp):
            o_inter.append(lax.dot_general(q_in[j * c:(j + 1) * c], st.astype(BF16), _NT,
                                           preferred_element_type=F32))
            st = decay[j] * st + kv[j]
        o = o_intra + jnp.concatenate(o_inter, axis=0)
        o = o * lax.rsqrt(jnp.mean(o * o, axis=-1, keepdims=True) + EPS)
        o_ref[0, sl, :] = (o * on * _silu(gz)).astype(o_ref.dtype)
        return st

    lax.fori_loop(0, seq // rows, body, jnp.zeros((dk, dk), F32), unroll=HGRN_UNROLL)


def _hgrn(proj, lbz, out_norm, layer, width):
    b, s, _ = proj.shape
    hd = HGRN_HEAD_DIM
    heads = width // hd
    sec = lambda k: pl.BlockSpec((1, s, hd), lambda i, h: (i, 0, k * heads + h))
    return pl.pallas_call(
        functools.partial(_hgrn_kernel, layer=layer),
        grid=(b, heads),
        in_specs=[sec(0), sec(1), sec(2), sec(3),
                  pl.BlockSpec((lbz.shape[0], hd), lambda i, h: (0, h)),
                  pl.BlockSpec((1, hd), lambda i, h: (0, h))],
        out_specs=pl.BlockSpec((1, s, hd), lambda i, h: (i, 0, h)),
        out_shape=jax.ShapeDtypeStruct((b, s, width), BF16),
        compiler_params=_params("parallel", "parallel"),
        name="hgrn2",
    )(proj, proj, proj, proj, lbz, out_norm.reshape(1, width))


def _conv_kernel(a_ref, b_ref, w_ref, db_ref, lg_ref, lb_ref, o_ref, cpad_ref, acc_ref):
    seq, ch = a_ref.shape[1], a_ref.shape[2]
    kw = w_ref.shape[0]
    pad = cpad_ref.shape[0] - seq
    tt = CONV_TILE
    cpad_ref[0:pad, :] = jnp.zeros((pad, ch), F32)

    def fill(t, carry):
        sl = pl.ds(pl.multiple_of(t * tt, tt), tt)
        cpad_ref[pl.ds(pl.multiple_of(pad + t * tt, SUBLANES), tt), :] = (
            a_ref[0, sl, :] * _sigmoid(b_ref[0, sl, :]))
        return carry

    lax.fori_loop(0, seq // tt, fill, 0)
    db, lg, lb = db_ref[...], lg_ref[...], lb_ref[...]

    offs = [pad - kw + 1 + k for k in range(kw)]
    sup = acc_ref.shape[0]
    lane_blocks = ch // LANES

    def conv(s, carry):
        sbase = pl.multiple_of(s * sup, sup)

        def block(i, c2):
            t, j = i // lane_blocks, i % lane_blocks
            base = pl.multiple_of(sbase + t * tt, tt)
            ls = pl.ds(pl.multiple_of(j * LANES, LANES), LANES)
            win = cpad_ref[pl.ds(base, tt + pad), ls]
            acc = jnp.zeros((tt, LANES), F32)
            for r in range(SUBLANES):
                taps = [k for k in range(kw) if offs[k] % SUBLANES == r]
                if not taps:
                    continue
                shifted = pltpu.roll(win, tt + pad - r, axis=0) if r else win
                for k in taps:
                    acc = acc + w_ref[k:k + 1, ls] * shifted[offs[k] - r:offs[k] - r + tt, :]
            acc_ref[pl.ds(pl.multiple_of(t * tt, tt), tt), ls] = acc
            return c2

        lax.fori_loop(0, (sup // tt) * lane_blocks, block, 0)
        acc = acc_ref[...] + db
        mu = jnp.mean(acc, axis=-1, keepdims=True)
        d = acc - mu
        y = d * lax.rsqrt(jnp.mean(d * d, axis=-1, keepdims=True) + EPS) * lg + lb
        o_ref[0, pl.ds(sbase, sup), :] = _silu(y).astype(o_ref.dtype)
        return carry

    lax.fori_loop(0, seq // sup, conv, 0)


def _conv_module(proj, col_a, col_b, dw_w, dw_b, ln_g, ln_b):
    b, s, _ = proj.shape
    kw, ch = dw_w.shape
    pad = -(-(kw - 1) // SUBLANES) * SUBLANES
    vec = lambda: pl.BlockSpec((1, ch), lambda i: (0, 0))
    return pl.pallas_call(
        _conv_kernel,
        grid=(b,),
        in_specs=[pl.BlockSpec((1, s, ch), lambda i: (i, 0, col_a)),
                  pl.BlockSpec((1, s, ch), lambda i: (i, 0, col_b)),
                  pl.BlockSpec((kw, ch), lambda i: (0, 0)), vec(), vec(), vec()],
        out_specs=pl.BlockSpec((1, s, ch), lambda i: (i, 0, 0)),
        out_shape=jax.ShapeDtypeStruct((b, s, ch), BF16),
        scratch_shapes=[pltpu.VMEM((pad + s, ch), F32), pltpu.VMEM((CONV_NORM_ROWS, ch), F32)],
        compiler_params=_params("parallel"),
        name="conv_module",
    )(proj, proj, dw_w, dw_b.reshape(1, ch), ln_g.reshape(1, ch), ln_b.reshape(1, ch))


def _moba_kernel(q_ref, k_ref, v_ref, o_ref, kaug_ref, vaug_ref):
    seq, w = q_ref.shape[1], q_ref.shape[2]
    hd, blk, topk = MOBA_HEAD_DIM, MOBA_BLOCK, MOBA_TOPK
    nblk = seq // blk
    scale = hd ** -0.5
    k = k_ref[0]
    v = v_ref[0]
    kmean = jnp.mean(k.reshape(nblk, blk, w), axis=1)
    lane_k = lax.broadcasted_iota(jnp.int32, (seq, w), 1)
    blk_k = lax.broadcasted_iota(jnp.int32, (seq, w), 0) // blk
    for e in range(2):
        in_head = (lane_k >= e * hd) & (lane_k < (e + 1) * hd)
        onehot = (lane_k - (1 - e) * hd == blk_k).astype(F32)
        kaug_ref[e] = jnp.where(in_head, k, onehot).astype(BF16)
        vaug_ref[e] = jnp.where(in_head, v, 1.0).astype(BF16)

    lane_q = lax.broadcasted_iota(jnp.int32, (blk, w), 1)
    lane_m = lax.broadcasted_iota(jnp.int32, (nblk, w), 1)
    ridx = lax.broadcasted_iota(jnp.int32, (nblk, blk), 0)
    own_causal = (lax.broadcasted_iota(jnp.int32, (blk, blk), 1)
                  <= lax.broadcasted_iota(jnp.int32, (blk, blk), 0))

    def scores(jq, e):
        q2 = q_ref[0, jq * blk:(jq + 1) * blk, :]
        lo = (1 - e) * hd
        if jq > topk:
            kme = jnp.where((lane_m >= e * hd) & (lane_m < (e + 1) * hd), kmean, 0.0)
            st = lax.dot_general(kme, q2, _NT, precision=lax.Precision.HIGHEST,
                                 preferred_element_type=F32)
            valid = ridx < jq
            rows = []
            for n in range(nblk):
                if n < jq:
                    sn = st[n:n + 1, :]
                    beats = valid & ((st > sn) | ((st == sn) & (ridx < n)))
                    rank = jnp.sum(beats.astype(F32), axis=0, keepdims=True)
                    rows.append(jnp.where(rank < topk, 0.0, NEG))
                else:
                    rows.append(jnp.zeros((1, blk), F32))
            pieces = [jnp.concatenate(rows, axis=0)]
            if lo:
                pieces.insert(0, jnp.zeros((lo, blk), F32))
            if w - lo - nblk:
                pieces.append(jnp.zeros((w - lo - nblk, blk), F32))
            bias_q = jnp.concatenate(pieces, axis=0).T
        else:
            bias_q = jnp.zeros((blk, w), F32)
        in_head = (lane_q >= e * hd) & (lane_q < (e + 1) * hd)
        q_aug = jnp.where(in_head, q2 * (scale * LOG2E), bias_q).astype(BF16)
        s = lax.dot_general(q_aug, kaug_ref[e, 0:(jq + 1) * blk, :], _NT, preferred_element_type=F32)
        s_own = jnp.where(own_causal, s[:, jq * blk:], NEG)
        m = jnp.max(s_own, axis=-1, keepdims=True)
        if not jq:
            return (s_own,), m
        s_past = s[:, :jq * blk]
        return (s_past, s_own), jnp.maximum(m, jnp.max(s_past, axis=-1, keepdims=True))

    def attend(jq, e, parts, m):
        ps = [jnp.exp2(s - m).astype(BF16) for s in parts]
        p = jnp.concatenate(ps, axis=-1) if len(ps) > 1 else ps[0]
        o = _dot(p, vaug_ref[e, 0:(jq + 1) * blk, :])
        lo = (1 - e) * hd
        return o * (1.0 / o[:, lo:lo + 1])

    units = [(jq, e) for jq in range(nblk) for e in range(2)]
    pending = scores(*units[0])
    outs = {}
    for u, (jq, e) in enumerate(units):
        ahead = scores(*units[u + 1]) if u + 1 < len(units) else None
        outs[e] = attend(jq, e, *pending)
        if e == 1:
            o_ref[0, jq * blk:(jq + 1) * blk, :] = jnp.where(lane_q < hd, outs[0], outs[1]).astype(o_ref.dtype)
        pending = ahead


def _moba(proj, width):
    b, s, _ = proj.shape
    w = 2 * MOBA_HEAD_DIM
    pairs = width // w
    sec = lambda k: pl.BlockSpec((1, s, w), lambda i, h: (i, 0, k * pairs + h))
    return pl.pallas_call(
        _moba_kernel,
        grid=(b, pairs),
        in_specs=[sec(0), sec(1), sec(2)],
        out_specs=pl.BlockSpec((1, s, w), lambda i, h: (i, 0, h)),
        out_shape=jax.ShapeDtypeStruct((b, s, width), BF16),
        scratch_shapes=[pltpu.VMEM((2, s, w), BF16), pltpu.VMEM((2, s, w), BF16)],
        compiler_params=_params("parallel", "parallel"),
        name="moba",
    )(proj, proj, proj)


def _sgu_kernel(u_ref, z_ref, lg_ref, lb_ref, w_ref, bias_ref, o_ref):
    ts, width = u_ref.shape[1], u_ref.shape[2]
    c = SGU_CHUNK
    gd = width // SGU_GROUPS
    row = lax.broadcasted_iota(jnp.int32, (c, c), 0)
    col = lax.broadcasted_iota(jnp.int32, (c, c), 1)
    for g in range(SGU_GROUPS):
        gs = slice(g * gd, (g + 1) * gd)
        z = _gelu(z_ref[0, :, gs])
        mu = jnp.mean(z, axis=-1, keepdims=True)
        d = z - mu
        zn = (d * lax.rsqrt(jnp.mean(d * d, axis=-1, keepdims=True) + EPS) * lg_ref[:, gs]
              + lb_ref[:, gs]).astype(BF16)
        wg = jnp.where(col <= row, w_ref[g], 0.0).astype(BF16)
        for n in range(ts // c):
            ts_ = slice(n * c, (n + 1) * c)
            mixed = _dot(wg, zn[ts_, :]) + bias_ref[:, gs]
            o_ref[0, ts_, gs] = (_gelu(u_ref[0, ts_, gs]) * mixed).astype(o_ref.dtype)


def _sgu(proj, col_u, col_z, ln_g, ln_b, w, bias):
    b, s, _ = proj.shape
    width = ln_g.shape[0]
    groups, c, _ = w.shape
    ts = ROW_TILE
    bias_full = jnp.repeat(bias.T, width // groups, axis=1)
    return pl.pallas_call(
        _sgu_kernel,
        grid=(b, s // ts),
        in_specs=[pl.BlockSpec((1, ts, width), lambda i, j: (i, j, col_u)),
                  pl.BlockSpec((1, ts, width), lambda i, j: (i, j, col_z)),
                  pl.BlockSpec((1, width), lambda i, j: (0, 0)),
                  pl.BlockSpec((1, width), lambda i, j: (0, 0)),
                  pl.BlockSpec((groups, c, c), lambda i, j: (0, 0, 0)),
                  pl.BlockSpec((c, width), lambda i, j: (0, 0))],
        out_specs=pl.BlockSpec((1, ts, width), lambda i, j: (i, j, 0)),
        out_shape=jax.ShapeDtypeStruct((b, s, width), BF16),
        compiler_params=_params("parallel", "parallel"),
        name="sgu",
    )(proj, proj, ln_g.reshape(1, width), ln_b.reshape(1, width), w, bias_full)


def _cross_attention(x, g, wq_ref, kv_ref, wo_ref):
    d = x.shape[1]
    hd = d // XA_HEADS
    q = (_dot(_rms(x, g).astype(BF16), wq_ref[...]) * (hd ** -0.5 * LOG2E)).astype(BF16)

    def logits(h):
        hs = slice(h * hd, (h + 1) * hd)
        s = lax.dot_general(q[:, hs], kv_ref[0, :, hs], _NT, preferred_element_type=F32)
        return s, jnp.max(s, axis=-1, keepdims=True)

    pending = logits(0)
    outs = []
    for h in range(XA_HEADS):
        ahead = logits(h + 1) if h + 1 < XA_HEADS else None
        s, m = pending
        p = jnp.exp2(s - m)
        o = _dot(p.astype(BF16), kv_ref[0, :, d + h * hd:d + (h + 1) * hd])
        outs.append((o * (1.0 / jnp.sum(p, axis=-1, keepdims=True))).astype(BF16))
        pending = ahead
    return x + _dot(jnp.concatenate(outs, axis=-1), wo_ref[...])


def _swiglu(x, g, w1_ref, w2_ref):
    hidden = w2_ref.shape[0]
    h = _rms(x, g).astype(BF16)
    acc = x
    for c in range(hidden // FFN_CHUNK):
        cs = slice(c * FFN_CHUNK, (c + 1) * FFN_CHUNK)
        gs = slice(hidden + c * FFN_CHUNK, hidden + (c + 1) * FFN_CHUNK)
        u = (_silu(_dot(h, w1_ref[:, cs])) * _dot(h, w1_ref[:, gs])).astype(BF16)
        acc = acc + _dot(u, w2_ref[cs, :])
    return acc


def _post_mixer_kernel(x_ref, a_ref, b_ref, wa_ref, wb_ref, gx_ref, wq_ref, kv_ref, wo_ref,
                       gf_ref, w1_ref, w2_ref, fg_ref, o_ref, *, final):
    x = x_ref[...] + _dot(a_ref[...], wa_ref[...]) + _dot(b_ref[...], wb_ref[...])
    x = _cross_attention(x, gx_ref[...], wq_ref, kv_ref, wo_ref)
    x = _swiglu(x, gf_ref[...], w1_ref, w2_ref)
    o_ref[...] = _rms(x, fg_ref[...]) if final else x


def _post_mixer(x2d, a2d, b2d, w_mix, gx, wq, kv, wo, gf, w1, w2, final_gain, final, seq):
    t, d = x2d.shape
    ka, kb = a2d.shape[1], b2d.shape[1]
    assert ka == kb
    m = kv.shape[1]
    hidden = w2.shape[0]
    per_batch = seq // ROW_TILE
    row = lambda n: pl.BlockSpec((ROW_TILE, n), lambda i: (i, 0))
    vec = lambda: pl.BlockSpec((1, d), lambda i: (0, 0))
    resident = lambda shape, blk=0: pl.BlockSpec(shape, lambda i: (blk, 0), pipeline_mode=pl.Buffered(1))
    return pl.pallas_call(
        functools.partial(_post_mixer_kernel, final=final),
        grid=(t // ROW_TILE,),
        in_specs=[row(d), row(ka), row(kb), resident((ka, d)), resident((kb, d), 1),
                  vec(), resident((d, d)),
                  pl.BlockSpec((1, m, 2 * d), lambda i: (i // per_batch, 0, 0)), resident((d, d)),
                  vec(), resident((d, 2 * hidden)), resident((hidden, d)), vec()],
        out_specs=row(d),
        out_shape=jax.ShapeDtypeStruct((t, d), F32),
        compiler_params=_params("parallel"),
        name="post_mixer",
    )(x2d, a2d, b2d, w_mix, w_mix, gx.reshape(1, d), wq, kv, wo,
      gf.reshape(1, d), w1, w2, final_gain.reshape(1, d))


def kernel(x, mem, norm_mix, norm_xattn, norm_ffn, mem_norm, final_norm, w_in_ab, w_out_ab, hgrn_lower_bounds, hgrn_out_norm, conv_dw_w, conv_dw_b, conv_ln_g, conv_ln_b, w_in_cd, w_out_cd, sgu_ln_g, sgu_ln_b, sgu_w, sgu_b, xa_wq, xa_wkv, xa_wo, ffn_w_in, ffn_w_out):
    b, s, d = x.shape
    m = mem.shape[1]
    depth = norm_mix.shape[0]
    bf = lambda a: a.astype(BF16)
    x2d = x.reshape(b * s, d)
    mem2d = mem.reshape(b * m, d)
    for l in range(depth):
        if l % 2 == 0:
            e = l // 2
            a_width = hgrn_out_norm.shape[1]
            b_width = conv_dw_w.shape[2]
            proj = _norm_matmul(x2d, norm_mix[l], bf(w_in_ab[e]), F32).reshape(b, s, -1)
            o_a = _hgrn(proj, hgrn_lower_bounds, hgrn_out_norm[e], l, a_width)
            col = 4 * a_width // b_width
            o_b = _conv_module(proj, col, col + 1, conv_dw_w[e], conv_dw_b[e], conv_ln_g[e], conv_ln_b[e])
            w_out = bf(w_out_ab[e])
        else:
            o = l // 2
            d_width = sgu_ln_g.shape[1]
            c_width = w_out_cd.shape[1] - d_width
            proj = _norm_matmul(x2d, norm_mix[l], bf(w_in_cd[o]), F32).reshape(b, s, -1)
            o_a = _moba(proj, c_width)
            col = 3 * c_width // d_width
            o_b = _sgu(proj, col, col + 1, sgu_ln_g[o], sgu_ln_b[o], sgu_w[o], sgu_b[o])
            w_out = bf(w_out_cd[o])
        kv = _norm_matmul(mem2d, mem_norm, bf(xa_wkv[l]), BF16).reshape(b, m, 2 * d)
        x2d = _post_mixer(x2d, o_a.reshape(b * s, -1), o_b.reshape(b * s, -1), w_out,
                          norm_xattn[l], bf(xa_wq[l]), kv, bf(xa_wo[l]),
                          norm_ffn[l], bf(ffn_w_in[l]), bf(ffn_w_out[l]), final_norm, l == depth - 1, s)
    return x2d.reshape(b, s, d)
```

```python
import functools

import jax
import jax.numpy as jnp
from jax import lax
from jax.experimental import pallas as pl
from jax.experimental.pallas import tpu as pltpu

F32 = jnp.float32
BF16 = jnp.bfloat16
EPS = 1e-6
NEG = -1e30
LOG2E = 1.4426950408889634

V7X_VMEM_BYTES = 64 * 1024 * 1024
VMEM_LIMIT = V7X_VMEM_BYTES - 8 * 1024 * 1024
SUBLANES = 8
LANES = 128

HGRN_HEAD_DIM = 128
HGRN_CHUNK = 64
HGRN_GROUP = 4
HGRN_UNROLL = 8
CONV_TILE = 64
CONV_NORM_ROWS = 256
MOBA_HEAD_DIM = 64
MOBA_BLOCK = 256
MOBA_TOPK = 3
SGU_CHUNK = 128
SGU_GROUPS = 4
SGU_ROWS = 1024
XA_HEADS = 4
ROW_TILE = 512
FFN_CHUNK = 256

_NT = (((1,), (1,)), ((), ()))
_TN = (((0,), (0,)), ((), ()))


def _params(*sem):
    return pltpu.CompilerParams(dimension_semantics=sem, vmem_limit_bytes=VMEM_LIMIT)


def _rms(x, g):
    return x * lax.rsqrt(jnp.mean(x * x, axis=-1, keepdims=True) + EPS) * g


def _sigmoid(x):
    return 1.0 / (1.0 + jnp.exp2(x * -LOG2E))


def _silu(x):
    return x * _sigmoid(x)


def _gelu(x):
    return 0.5 * x * (1.0 + lax.erf(x * (2.0 ** -0.5)))


def _dot(a, b):
    return jnp.dot(a, b, preferred_element_type=F32)


def _dot01_f32(m01, x):
    hi = x.astype(BF16)
    r1 = x - hi.astype(F32)
    mid = r1.astype(BF16)
    lo = (r1 - mid.astype(F32)).astype(BF16)
    return _dot(m01, hi) + _dot(m01, mid) + _dot(m01, lo)


def _norm_matmul_kernel(x_ref, g_ref, w_ref, o_ref):
    h = _rms(x_ref[...], g_ref[...]).astype(BF16)
    o_ref[...] = _dot(h, w_ref[...]).astype(o_ref.dtype)


def _norm_matmul(x2d, g, w, out_dtype):
    t, d = x2d.shape
    n = w.shape[1]
    return pl.pallas_call(
        _norm_matmul_kernel,
        grid=(t // ROW_TILE,),
        in_specs=[pl.BlockSpec((ROW_TILE, d), lambda i: (i, 0)),
                  pl.BlockSpec((1, d), lambda i: (0, 0)),
                  pl.BlockSpec((d, n), lambda i: (0, 0))],
        out_specs=pl.BlockSpec((ROW_TILE, n), lambda i: (i, 0)),
        out_shape=jax.ShapeDtypeStruct((t, n), out_dtype),
        compiler_params=_params("parallel"),
        name="norm_matmul",
    )(x2d, g.reshape(1, d), w)


def _hgrn_kernel(q_ref, f_ref, i_ref, g_ref, lbz_ref, on_ref, o_ref, *, layer):
    seq, dk = q_ref.shape[1], q_ref.shape[2]
    c, grp = HGRN_CHUNK, HGRN_GROUP
    rows = c * grp
    lbz = lbz_ref[...]
    e = jnp.exp(lbz - jnp.max(lbz, axis=0, keepdims=True))
    lb = jnp.sum(e[:layer + 1], axis=0, keepdims=True) / jnp.sum(e, axis=0, keepdims=True)
    on = on_ref[...]
    row = lax.broadcasted_iota(jnp.int32, (rows, rows), 0)
    col = lax.broadcasted_iota(jnp.int32, (rows, rows), 1)
    causal = (col <= row) & (col >= (row // c) * c)
    tril = causal.astype(BF16)

    def body(n, st):
        sl = pl.ds(pl.multiple_of(n * rows, rows), rows)
        qz, fz, v, gz = q_ref[0, sl, :], f_ref[0, sl, :], i_ref[0, sl, :], g_ref[0, sl, :]
        f = lb + (1.0 - lb) * _sigmoid(fz)
        cum = _dot01_f32(tril, jnp.log(f)) * LOG2E
        k = 1.0 - f
        q_in = (_silu(qz) * jnp.exp2(cum)).astype(BF16)
        k_in = (k * jnp.exp2(-cum)).astype(BF16)
        att = lax.dot_general(q_in, k_in, _NT, preferred_element_type=F32)
        att = jnp.where(causal, att, 0.0).astype(BF16)
        vb = v.astype(BF16)
        o_intra = _dot(att, vb)
        cum3 = cum.reshape(grp, c, dk)
        cl = cum3[:, c - 1:c, :]
        kdec = (k.reshape(grp, c, dk) * jnp.exp2(cl - cum3)).astype(BF16)
        decay = jnp.exp2(cl)
        kv = [lax.dot_general(vb[j * c:(j + 1) * c], kdec[j], _TN, preferred_element_type=F32)
              for j in range(grp)]
        o_inter = []
        for j in range(grp):
            o_inter.append(lax.dot_general(q_in[j * c:(j + 1) * c], st.astype(BF16), _NT,
                                           preferred_element_type=F32))
            st = decay[j] * st + kv[j]
        o = o_intra + jnp.concatenate(o_inter, axis=0)
        o = o * lax.rsqrt(jnp.mean(o * o, axis=-1, keepdims=True) + EPS)
        o_ref[0, sl, :] = (o * on * _silu(gz)).astype(o_ref.dtype)
        return st

    lax.fori_loop(0, seq // rows, body, jnp.zeros((dk, dk), F32), unroll=HGRN_UNROLL)


def _hgrn(proj, lbz, out_norm, layer, width):
    b, s, _ = proj.shape
    hd = HGRN_HEAD_DIM
    heads = width // hd
    sec = lambda k: pl.BlockSpec((1, s, hd), lambda i, h: (i, 0, k * heads + h))
    return pl.pallas_call(
        functools.partial(_hgrn_kernel, layer=layer),
        grid=(b, heads),
        in_specs=[sec(0), sec(1), sec(2), sec(3),
                  pl.BlockSpec((lbz.shape[0], hd), lambda i, h: (0, h)),
                  pl.BlockSpec((1, hd), lambda i, h: (0, h))],
        out_specs=pl.BlockSpec((1, s, hd), lambda i, h: (i, 0, h)),
        out_shape=jax.ShapeDtypeStruct((b, s, width), BF16),
        compiler_params=_params("parallel", "parallel"),
        name="hgrn2",
    )(proj, proj, proj, proj, lbz, out_norm.reshape(1, width))


def _conv_kernel(a_ref, b_ref, w_ref, db_ref, lg_ref, lb_ref, o_ref, cpad_ref, acc_ref):
    seq, ch = a_ref.shape[1], a_ref.shape[2]
    kw = w_ref.shape[0]
    pad = cpad_ref.shape[0] - seq
    tt = CONV_TILE
    cpad_ref[0:pad, :] = jnp.zeros((pad, ch), F32)

    def fill(t, carry):
        sl = pl.ds(pl.multiple_of(t * tt, tt), tt)
        cpad_ref[pl.ds(pl.multiple_of(pad + t * tt, SUBLANES), tt), :] = (
            a_ref[0, sl, :] * _sigmoid(b_ref[0, sl, :]))
        return carry

    lax.fori_loop(0, seq // tt, fill, 0)
    db, lg, lb = db_ref[...], lg_ref[...], lb_ref[...]

    offs = [pad - kw + 1 + k for k in range(kw)]
    sup = acc_ref.shape[0]
    lane_blocks = ch // LANES

    def conv(s, carry):
        sbase = pl.multiple_of(s * sup, sup)

        def block(i, c2):
            t, j = i // lane_blocks, i % lane_blocks
            base = pl.multiple_of(sbase + t * tt, tt)
            ls = pl.ds(pl.multiple_of(j * LANES, LANES), LANES)
            win = cpad_ref[pl.ds(base, tt + pad), ls]
            acc = jnp.zeros((tt, LANES), F32)
            for r in range(SUBLANES):
                taps = [k for k in range(kw) if offs[k] % SUBLANES == r]
                if not taps:
                    continue
                shifted = pltpu.roll(win, tt + pad - r, axis=0) if r else win
                for k in taps:
                    acc = acc + w_ref[k:k + 1, ls] * shifted[offs[k] - r:offs[k] - r + tt, :]
            acc_ref[pl.ds(pl.multiple_of(t * tt, tt), tt), ls] = acc
            return c2

        lax.fori_loop(0, (sup // tt) * lane_blocks, block, 0)
        acc = acc_ref[...] + db
        mu = jnp.mean(acc, axis=-1, keepdims=True)
        d = acc - mu
        y = d * lax.rsqrt(jnp.mean(d * d, axis=-1, keepdims=True) + EPS) * lg + lb
        o_ref[0, pl.ds(sbase, sup), :] = _silu(y).astype(o_ref.dtype)
        return carry

    lax.fori_loop(0, seq // sup, conv, 0)


def _conv_module(proj, col_a, col_b, dw_w, dw_b, ln_g, ln_b):
    b, s, _ = proj.shape
    kw, ch = dw_w.shape
    pad = -(-(kw - 1) // SUBLANES) * SUBLANES
    vec = lambda: pl.BlockSpec((1, ch), lambda i: (0, 0))
    return pl.pallas_call(
        _conv_kernel,
        grid=(b,),
        in_specs=[pl.BlockSpec((1, s, ch), lambda i: (i, 0, col_a)),
                  pl.BlockSpec((1, s, ch), lambda i: (i, 0, col_b)),
                  pl.BlockSpec((kw, ch), lambda i: (0, 0)), vec(), vec(), vec()],
        out_specs=pl.BlockSpec((1, s, ch), lambda i: (i, 0, 0)),
        out_shape=jax.ShapeDtypeStruct((b, s, ch), BF16),
        scratch_shapes=[pltpu.VMEM((pad + s, ch), F32), pltpu.VMEM((CONV_NORM_ROWS, ch), F32)],
        compiler_params=_params("parallel"),
        name="conv_module",
    )(proj, proj, dw_w, dw_b.reshape(1, ch), ln_g.reshape(1, ch), ln_b.reshape(1, ch))


def _moba_kernel(q_ref, k_ref, v_ref, o_ref, qaug_ref, kaug_ref, vaug_ref, s_ref, p_ref, m_ref, acc_ref):
    seq, w = q_ref.shape[1], q_ref.shape[2]
    hd, blk, topk = MOBA_HEAD_DIM, MOBA_BLOCK, MOBA_TOPK
    nblk = seq // blk
    half = blk // 2
    scale = hd ** -0.5
    k = k_ref[0]
    v = v_ref[0]
    kmean = jnp.mean(k.reshape(nblk, blk, w), axis=1)
    lane_k = lax.broadcasted_iota(jnp.int32, (seq, w), 1)
    blk_k = lax.broadcasted_iota(jnp.int32, (seq, w), 0) // blk
    for e in range(2):
        in_head = (lane_k >= e * hd) & (lane_k < (e + 1) * hd)
        onehot = (lane_k - (1 - e) * hd == blk_k).astype(F32)
        kaug_ref[e] = jnp.where(in_head, k, onehot).astype(BF16)
        vaug_ref[e] = jnp.where(in_head, v, 1.0).astype(BF16)

    lane_q = lax.broadcasted_iota(jnp.int32, (blk, w), 1)
    lane_m = lax.broadcasted_iota(jnp.int32, (nblk, w), 1)
    ridx = lax.broadcasted_iota(jnp.int32, (nblk, blk), 0)
    own_causal = (lax.broadcasted_iota(jnp.int32, (blk, blk), 1)
                  <= lax.broadcasted_iota(jnp.int32, (blk, blk), 0))

    for jq in range(nblk):
        qs = slice(jq * blk, (jq + 1) * blk)
        q2 = q_ref[0, qs, :]
        for e in range(2):
            lo = (1 - e) * hd
            if jq > topk:
                kme = jnp.where((lane_m >= e * hd) & (lane_m < (e + 1) * hd), kmean, 0.0)
                st = lax.dot_general(kme, q2, _NT, precision=lax.Precision.HIGHEST,
                                     preferred_element_type=F32)
                valid = ridx < jq
                rows = []
                for n in range(nblk):
                    if n < jq:
                        sn = st[n:n + 1, :]
                        beats = valid & ((st > sn) | ((st == sn) & (ridx < n)))
                        rank = jnp.sum(beats.astype(F32), axis=0, keepdims=True)
                        rows.append(jnp.where(rank < topk, 0.0, NEG))
                    else:
                        rows.append(jnp.zeros((1, blk), F32))
                pieces = [jnp.concatenate(rows, axis=0)]
                if lo:
                    pieces.insert(0, jnp.zeros((lo, blk), F32))
                if w - lo - nblk:
                    pieces.append(jnp.zeros((w - lo - nblk, blk), F32))
                bias_q = jnp.concatenate(pieces, axis=0).T
            else:
                bias_q = jnp.zeros((blk, w), F32)
            in_head = (lane_q >= e * hd) & (lane_q < (e + 1) * hd)
            qaug_ref[e, qs, :] = jnp.where(in_head, q2 * (scale * LOG2E), bias_q).astype(BF16)

    base = [sum(seq - i * blk for i in range(n)) for n in range(nblk)]

    def fold_max(e, rows, s, first):
        t = jnp.maximum(s[:, :half], s[:, half:])
        m_ref[e, rows, :] = t if first else jnp.maximum(m_ref[e, rows, :], t)

    def score(e, n):
        s = lax.dot_general(qaug_ref[e, n * blk:, :], kaug_ref[e, n * blk:(n + 1) * blk, :], _NT,
                            preferred_element_type=F32)
        own = jnp.where(own_causal, s[:blk], NEG)
        s_ref[e, base[n]:base[n] + blk, :] = own
        fold_max(e, slice(n * blk, (n + 1) * blk), own, n == 0)
        if n + 1 < nblk:
            s_ref[e, base[n] + blk:base[n] + seq - n * blk, :] = s[blk:]
            fold_max(e, slice((n + 1) * blk, seq), s[blk:], n == 0)

    def row_max(e):
        m_ref[e] = jnp.broadcast_to(jnp.max(m_ref[e], axis=-1, keepdims=True), (seq, half))

    def weights(e, n):
        rows = slice(base[n], base[n] + seq - n * blk)
        s = s_ref[e, rows, :]
        mb = m_ref[e, n * blk:, :]
        p_ref[e, rows, :half] = jnp.exp2(s[:, :half] - mb).astype(BF16)
        p_ref[e, rows, half:] = jnp.exp2(s[:, half:] - mb).astype(BF16)

    def values(e, n):
        rows = slice(base[n], base[n] + seq - n * blk)
        o = _dot(p_ref[e, rows, :], vaug_ref[e, n * blk:(n + 1) * blk, :])
        if n == 0:
            acc_ref[e] = o
        else:
            acc_ref[e, n * blk:, :] += o

    for n in range(nblk):
        score(0, n)
    row_max(0)
    for n in range(nblk):
        score(1, n)
        weights(0, n)
    row_max(1)
    for n in range(nblk):
        values(0, n)
        weights(1, n)
    for n in range(nblk):
        values(1, n)
    outs = []
    for e in range(2):
        lo = (1 - e) * hd
        o = acc_ref[e]
        outs.append(o * (1.0 / o[:, lo:lo + 1]))
    lane_o = lax.broadcasted_iota(jnp.int32, (seq, w), 1)
    o_ref[0] = jnp.where(lane_o < hd, outs[0], outs[1]).astype(o_ref.dtype)


def _moba(proj, width):
    b, s, _ = proj.shape
    w = 2 * MOBA_HEAD_DIM
    pairs = width // w
    nblk = s // MOBA_BLOCK
    tiles = nblk * (nblk + 1) // 2
    sec = lambda k: pl.BlockSpec((1, s, w), lambda i, h: (i, 0, k * pairs + h))
    return pl.pallas_call(
        _moba_kernel,
        grid=(b, pairs),
        in_specs=[sec(0), sec(1), sec(2)],
        out_specs=pl.BlockSpec((1, s, w), lambda i, h: (i, 0, h)),
        out_shape=jax.ShapeDtypeStruct((b, s, width), BF16),
        scratch_shapes=[pltpu.VMEM((2, s, w), BF16), pltpu.VMEM((2, s, w), BF16), pltpu.VMEM((2, s, w), BF16),
                        pltpu.VMEM((2, tiles * MOBA_BLOCK, MOBA_BLOCK), F32),
                        pltpu.VMEM((2, tiles * MOBA_BLOCK, MOBA_BLOCK), BF16),
                        pltpu.VMEM((2, s, MOBA_BLOCK // 2), F32),
                        pltpu.VMEM((2, s, w), F32)],
        compiler_params=_params("parallel", "parallel"),
        name="moba",
    )(proj, proj, proj)


def _sgu_kernel(u_ref, z_ref, lg_ref, lb_ref, w_ref, bias_ref, o_ref):
    ts, width = u_ref.shape[1], u_ref.shape[2]
    c = SGU_CHUNK
    gd = width // SGU_GROUPS
    row = lax.broadcasted_iota(jnp.int32, (c, c), 0)
    col = lax.broadcasted_iota(jnp.int32, (c, c), 1)
    for g in range(SGU_GROUPS):
        gs = slice(g * gd, (g + 1) * gd)
        z = _gelu(z_ref[0, :, gs])
        mu = jnp.mean(z, axis=-1, keepdims=True)
        d = z - mu
        zn = (d * lax.rsqrt(jnp.mean(d * d, axis=-1, keepdims=True) + EPS) * lg_ref[:, gs]
              + lb_ref[:, gs]).astype(BF16)
        wg = jnp.where(col <= row, w_ref[g], 0.0).astype(BF16)
        for n in range(ts // c):
            ts_ = slice(n * c, (n + 1) * c)
            mixed = _dot(wg, zn[ts_, :]) + bias_ref[:, gs]
            o_ref[0, ts_, gs] = (_gelu(u_ref[0, ts_, gs]) * mixed).astype(o_ref.dtype)


def _sgu(proj, col_u, col_z, ln_g, ln_b, w, bias):
    b, s, _ = proj.shape
    width = ln_g.shape[0]
    groups, c, _ = w.shape
    ts = SGU_ROWS
    bias_full = jnp.repeat(bias.T, width // groups, axis=1)
    return pl.pallas_call(
        _sgu_kernel,
        grid=(b, s // ts),
        in_specs=[pl.BlockSpec((1, ts, width), lambda i, j: (i, j, col_u)),
                  pl.BlockSpec((1, ts, width), lambda i, j: (i, j, col_z)),
                  pl.BlockSpec((1, width), lambda i, j: (0, 0)),
                  pl.BlockSpec((1, width), lambda i, j: (0, 0)),
                  pl.BlockSpec((groups, c, c), lambda i, j: (0, 0, 0)),
                  pl.BlockSpec((c, width), lambda i, j: (0, 0))],
        out_specs=pl.BlockSpec((1, ts, width), lambda i, j: (i, j, 0)),
        out_shape=jax.ShapeDtypeStruct((b, s, width), BF16),
        compiler_params=_params("parallel", "parallel"),
        name="sgu",
    )(proj, proj, ln_g.reshape(1, width), ln_b.reshape(1, width), w, bias_full)


def _cross_attention(x, g, wq_ref, kv_ref, wo_ref):
    d = x.shape[1]
    hd = d // XA_HEADS
    q = (_dot(_rms(x, g).astype(BF16), wq_ref[...]) * (hd ** -0.5 * LOG2E)).astype(BF16)

    def logits(h):
        hs = slice(h * hd, (h + 1) * hd)
        s = lax.dot_general(q[:, hs], kv_ref[0, :, hs], _NT, preferred_element_type=F32)
        return s, jnp.max(s, axis=-1, keepdims=True)

    pending = logits(0)
    outs = []
    for h in range(XA_HEADS):
        ahead = logits(h + 1) if h + 1 < XA_HEADS else None
        s, m = pending
        p = jnp.exp2(s - m)
        o = _dot(p.astype(BF16), kv_ref[0, :, d + h * hd:d + (h + 1) * hd])
        outs.append((o * (1.0 / jnp.sum(p, axis=-1, keepdims=True))).astype(BF16))
        pending = ahead
    return x + _dot(jnp.concatenate(outs, axis=-1), wo_ref[...])


def _swiglu(x, g, w1_ref, w2_ref):
    hidden = w2_ref.shape[0]
    h = _rms(x, g).astype(BF16)
    acc = x
    for c in range(hidden // FFN_CHUNK):
        cs = slice(c * FFN_CHUNK, (c + 1) * FFN_CHUNK)
        gs = slice(hidden + c * FFN_CHUNK, hidden + (c + 1) * FFN_CHUNK)
        u = (_silu(_dot(h, w1_ref[:, cs])) * _dot(h, w1_ref[:, gs])).astype(BF16)
        acc = acc + _dot(u, w2_ref[cs, :])
    return acc


def _post_mixer_kernel(x_ref, a_ref, b_ref, wa_ref, wb_ref, gx_ref, wq_ref, kv_ref, wo_ref,
                       gf_ref, w1_ref, w2_ref, fg_ref, o_ref, *, final):
    x = x_ref[...] + _dot(a_ref[...], wa_ref[...]) + _dot(b_ref[...], wb_ref[...])
    x = _cross_attention(x, gx_ref[...], wq_ref, kv_ref, wo_ref)
    x = _swiglu(x, gf_ref[...], w1_ref, w2_ref)
    o_ref[...] = _rms(x, fg_ref[...]) if final else x


def _post_mixer(x2d, a2d, b2d, w_mix, gx, wq, kv, wo, gf, w1, w2, final_gain, final, seq):
    t, d = x2d.shape
    ka, kb = a2d.shape[1], b2d.shape[1]
    assert ka == kb
    m = kv.shape[1]
    hidden = w2.shape[0]
    per_batch = seq // ROW_TILE
    row = lambda n: pl.BlockSpec((ROW_TILE, n), lambda i: (i, 0))
    vec = lambda: pl.BlockSpec((1, d), lambda i: (0, 0))
    resident = lambda shape, blk=0: pl.BlockSpec(shape, lambda i: (blk, 0), pipeline_mode=pl.Buffered(1))
    return pl.pallas_call(
        functools.partial(_post_mixer_kernel, final=final),
        grid=(t // ROW_TILE,),
        in_specs=[row(d), row(ka), row(kb), resident((ka, d)), resident((kb, d), 1),
                  vec(), resident((d, d)),
                  pl.BlockSpec((1, m, 2 * d), lambda i: (i // per_batch, 0, 0)), resident((d, d)),
                  vec(), resident((d, 2 * hidden)), resident((hidden, d)), vec()],
        out_specs=row(d),
        out_shape=jax.ShapeDtypeStruct((t, d), F32),
        compiler_params=_params("parallel"),
        name="post_mixer",
    )(x2d, a2d, b2d, w_mix, w_mix, gx.reshape(1, d), wq, kv, wo,
      gf.reshape(1, d), w1, w2, final_gain.reshape(1, d))


def kernel(x, mem, norm_mix, norm_xattn, norm_ffn, mem_norm, final_norm, w_in_ab, w_out_ab, hgrn_lower_bounds, hgrn_out_norm, conv_dw_w, conv_dw_b, conv_ln_g, conv_ln_b, w_in_cd, w_out_cd, sgu_ln_g, sgu_ln_b, sgu_w, sgu_b, xa_wq, xa_wkv, xa_wo, ffn_w_in, ffn_w_out):
    b, s, d = x.shape
    m = mem.shape[1]
    depth = norm_mix.shape[0]
    bf = lambda a: a.astype(BF16)
    x2d = x.reshape(b * s, d)
    mem2d = mem.reshape(b * m, d)
    for l in range(depth):
        if l % 2 == 0:
            e = l // 2
            a_width = hgrn_out_norm.shape[1]
            b_width = conv_dw_w.shape[2]
            proj = _norm_matmul(x2d, norm_mix[l], bf(w_in_ab[e]), F32).reshape(b, s, -1)
            o_a = _hgrn(proj, hgrn_lower_bounds, hgrn_out_norm[e], l, a_width)
            col = 4 * a_width // b_width
            o_b = _conv_module(proj, col, col + 1, conv_dw_w[e], conv_dw_b[e], conv_ln_g[e], conv_ln_b[e])
            w_out = bf(w_out_ab[e])
        else:
            o = l // 2
            d_width = sgu_ln_g.shape[1]
            c_width = w_out_cd.shape[1] - d_width
            proj = _norm_matmul(x2d, norm_mix[l], bf(w_in_cd[o]), F32).reshape(b, s, -1)
            o_a = _moba(proj, c_width)
            col = 3 * c_width // d_width
            o_b = _sgu(proj, col, col + 1, sgu_ln_g[o], sgu_ln_b[o], sgu_w[o], sgu_b[o])
            w_out = bf(w_out_cd[o])
        kv = _norm_matmul(mem2d, mem_norm, bf(xa_wkv[l]), BF16).reshape(b, m, 2 * d)
        x2d = _post_mixer(x2d, o_a.reshape(b * s, -1), o_b.reshape(b * s, -1), w_out,
                          norm_xattn[l], bf(xa_wq[l]), kv, bf(xa_wo[l]),
                          norm_ffn[l], bf(ffn_w_in[l]), bf(ffn_w_out[l]), final_norm, l == depth - 1, s)
    return x2d.reshape(b, s, d)
```

```python
import functools

import jax
import jax.numpy as jnp
from jax import lax
from jax.experimental import pallas as pl
from jax.experimental.pallas import tpu as pltpu

F32 = jnp.float32
BF16 = jnp.bfloat16
EPS = 1e-6
NEG = -1e30
LOG2E = 1.4426950408889634

V7X_VMEM_BYTES = 64 * 1024 * 1024
VMEM_LIMIT = V7X_VMEM_BYTES - 8 * 1024 * 1024
SUBLANES = 8
LANES = 128

HGRN_HEAD_DIM = 128
HGRN_CHUNK = 64
HGRN_GROUP = 4
HGRN_UNROLL = 8
CONV_TILE = 64
CONV_NORM_ROWS = 256
MOBA_HEAD_DIM = 64
MOBA_BLOCK = 256
MOBA_TOPK = 3
SGU_CHUNK = 128
SGU_GROUPS = 4
SGU_ROWS = 1024
XA_HEADS = 4
ROW_TILE = 512
FFN_CHUNK = 256

_NT = (((1,), (1,)), ((), ()))
_TN = (((0,), (0,)), ((), ()))


def _params(*sem):
    return pltpu.CompilerParams(dimension_semantics=sem, vmem_limit_bytes=VMEM_LIMIT)


def _rms(x, g):
    return x * lax.rsqrt(jnp.mean(x * x, axis=-1, keepdims=True) + EPS) * g


def _sigmoid(x):
    return 1.0 / (1.0 + jnp.exp2(x * -LOG2E))


def _silu(x):
    return x * _sigmoid(x)


def _gelu(x):
    return 0.5 * x * (1.0 + lax.erf(x * (2.0 ** -0.5)))


def _dot(a, b):
    return jnp.dot(a, b, preferred_element_type=F32)


def _dot01_f32(m01, x):
    hi = x.astype(BF16)
    r1 = x - hi.astype(F32)
    mid = r1.astype(BF16)
    lo = (r1 - mid.astype(F32)).astype(BF16)
    return _dot(m01, hi) + _dot(m01, mid) + _dot(m01, lo)


def _resident(stacked, layer):
    return pl.BlockSpec((None,) + stacked.shape[1:], lambda i: (layer, 0, 0), pipeline_mode=pl.Buffered(1))


def _cast_once(dst_ref, src_ref):
    @pl.when(pl.program_id(0) == 0)
    def _():
        dst_ref[...] = src_ref[...].astype(BF16)


def _norm_matmul_kernel(x_ref, g_ref, w_ref, o_ref, wb_ref):
    _cast_once(wb_ref, w_ref)
    h = _rms(x_ref[...], g_ref[...]).astype(BF16)
    o_ref[...] = _dot(h, wb_ref[...]).astype(o_ref.dtype)


def _norm_matmul(x2d, g, w_stacked, layer, out_dtype):
    t, d = x2d.shape
    n = w_stacked.shape[2]
    return pl.pallas_call(
        _norm_matmul_kernel,
        grid=(t // ROW_TILE,),
        in_specs=[pl.BlockSpec((ROW_TILE, d), lambda i: (i, 0)),
                  pl.BlockSpec((1, d), lambda i: (0, 0)),
                  _resident(w_stacked, layer)],
        out_specs=pl.BlockSpec((ROW_TILE, n), lambda i: (i, 0)),
        out_shape=jax.ShapeDtypeStruct((t, n), out_dtype),
        scratch_shapes=[pltpu.VMEM((d, n), BF16)],
        compiler_params=_params("arbitrary"),
        name="norm_matmul",
    )(x2d, g.reshape(1, d), w_stacked)


def _hgrn_kernel(q_ref, f_ref, i_ref, g_ref, lbz_ref, on_ref, o_ref, *, layer):
    seq, dk = q_ref.shape[1], q_ref.shape[2]
    c, grp = HGRN_CHUNK, HGRN_GROUP
    rows = c * grp
    lbz = lbz_ref[...]
    e = jnp.exp(lbz - jnp.max(lbz, axis=0, keepdims=True))
    lb = jnp.sum(e[:layer + 1], axis=0, keepdims=True) / jnp.sum(e, axis=0, keepdims=True)
    on = on_ref[...]
    row = lax.broadcasted_iota(jnp.int32, (rows, rows), 0)
    col = lax.broadcasted_iota(jnp.int32, (rows, rows), 1)
    causal = (col <= row) & (col >= (row // c) * c)
    tril = causal.astype(BF16)

    def body(n, st):
        sl = pl.ds(pl.multiple_of(n * rows, rows), rows)
        qz, fz, v, gz = q_ref[0, sl, :], f_ref[0, sl, :], i_ref[0, sl, :], g_ref[0, sl, :]
        f = lb + (1.0 - lb) * _sigmoid(fz)
        cum = _dot01_f32(tril, jnp.log(f)) * LOG2E
        k = 1.0 - f
        q_in = (_silu(qz) * jnp.exp2(cum)).astype(BF16)
        k_in = (k * jnp.exp2(-cum)).astype(BF16)
        att = lax.dot_general(q_in, k_in, _NT, preferred_element_type=F32)
        att = jnp.where(causal, att, 0.0).astype(BF16)
        vb = v.astype(BF16)
        o_intra = _dot(att, vb)
        cum3 = cum.reshape(grp, c, dk)
        cl = cum3[:, c - 1:c, :]
        kdec = (k.reshape(grp, c, dk) * jnp.exp2(cl - cum3)).astype(BF16)
        decay = jnp.exp2(cl)
        kv = [lax.dot_general(vb[j * c:(j + 1) * c], kdec[j], _TN, preferred_element_type=F32)
              for j in range(grp)]
        o_inter = []
        for j in range(grp):
            o_inter.append(lax.dot_general(q_in[j * c:(j + 1) * c], st.astype(BF16), _NT,
                                           preferred_element_type=F32))
            st = decay[j] * st + kv[j]
        o = o_intra + jnp.concatenate(o_inter, axis=0)
        o = o * lax.rsqrt(jnp.mean(o * o, axis=-1, keepdims=True) + EPS)
        o_ref[0, sl, :] = (o * on * _silu(gz)).astype(o_ref.dtype)
        return st

    lax.fori_loop(0, seq // rows, body, jnp.zeros((dk, dk), F32), unroll=HGRN_UNROLL)


def _hgrn(proj, lbz, out_norm, layer, width):
    b, s, _ = proj.shape
    hd = HGRN_HEAD_DIM
    heads = width // hd
    sec = lambda k: pl.BlockSpec((1, s, hd), lambda i, h: (i, 0, k * heads + h))
    return pl.pallas_call(
        functools.partial(_hgrn_kernel, layer=layer),
        grid=(b, heads),
        in_specs=[sec(0), sec(1), sec(2), sec(3),
                  pl.BlockSpec((lbz.shape[0], hd), lambda i, h: (0, h)),
                  pl.BlockSpec((1, hd), lambda i, h: (0, h))],
        out_specs=pl.BlockSpec((1, s, hd), lambda i, h: (i, 0, h)),
        out_shape=jax.ShapeDtypeStruct((b, s, width), BF16),
        compiler_params=_params("parallel", "parallel"),
        name="hgrn2",
    )(proj, proj, proj, proj, lbz, out_norm.reshape(1, width))


def _conv_kernel(a_ref, b_ref, w_ref, db_ref, lg_ref, lb_ref, o_ref, cpad_ref, acc_ref):
    seq, ch = a_ref.shape[1], a_ref.shape[2]
    kw = w_ref.shape[0]
    pad = cpad_ref.shape[0] - seq
    tt = CONV_TILE
    cpad_ref[0:pad, :] = jnp.zeros((pad, ch), F32)

    def fill(t, carry):
        sl = pl.ds(pl.multiple_of(t * tt, tt), tt)
        cpad_ref[pl.ds(pl.multiple_of(pad + t * tt, SUBLANES), tt), :] = (
            a_ref[0, sl, :] * _sigmoid(b_ref[0, sl, :]))
        return carry

    lax.fori_loop(0, seq // tt, fill, 0)
    db, lg, lb = db_ref[...], lg_ref[...], lb_ref[...]

    offs = [pad - kw + 1 + k for k in range(kw)]
    sup = acc_ref.shape[0]
    lane_blocks = ch // LANES

    def conv(s, carry):
        sbase = pl.multiple_of(s * sup, sup)

        def block(i, c2):
            t, j = i // lane_blocks, i % lane_blocks
            base = pl.multiple_of(sbase + t * tt, tt)
            ls = pl.ds(pl.multiple_of(j * LANES, LANES), LANES)
            win = cpad_ref[pl.ds(base, tt + pad), ls]
            acc = jnp.zeros((tt, LANES), F32)
            for r in range(SUBLANES):
                taps = [k for k in range(kw) if offs[k] % SUBLANES == r]
                if not taps:
                    continue
                shifted = pltpu.roll(win, tt + pad - r, axis=0) if r else win
                for k in taps:
                    acc = acc + w_ref[k:k + 1, ls] * shifted[offs[k] - r:offs[k] - r + tt, :]
            acc_ref[pl.ds(pl.multiple_of(t * tt, tt), tt), ls] = acc
            return c2

        lax.fori_loop(0, (sup // tt) * lane_blocks, block, 0)
        acc = acc_ref[...] + db
        mu = jnp.mean(acc, axis=-1, keepdims=True)
        d = acc - mu
        y = d * lax.rsqrt(jnp.mean(d * d, axis=-1, keepdims=True) + EPS) * lg + lb
        o_ref[0, pl.ds(sbase, sup), :] = _silu(y).astype(o_ref.dtype)
        return carry

    lax.fori_loop(0, seq // sup, conv, 0)


def _conv_module(proj, col_a, col_b, dw_w, dw_b, ln_g, ln_b):
    b, s, _ = proj.shape
    kw, ch = dw_w.shape
    pad = -(-(kw - 1) // SUBLANES) * SUBLANES
    vec = lambda: pl.BlockSpec((1, ch), lambda i: (0, 0))
    return pl.pallas_call(
        _conv_kernel,
        grid=(b,),
        in_specs=[pl.BlockSpec((1, s, ch), lambda i: (i, 0, col_a)),
                  pl.BlockSpec((1, s, ch), lambda i: (i, 0, col_b)),
                  pl.BlockSpec((kw, ch), lambda i: (0, 0)), vec(), vec(), vec()],
        out_specs=pl.BlockSpec((1, s, ch), lambda i: (i, 0, 0)),
        out_shape=jax.ShapeDtypeStruct((b, s, ch), BF16),
        scratch_shapes=[pltpu.VMEM((pad + s, ch), F32), pltpu.VMEM((CONV_NORM_ROWS, ch), F32)],
        compiler_params=_params("parallel"),
        name="conv_module",
    )(proj, proj, dw_w, dw_b.reshape(1, ch), ln_g.reshape(1, ch), ln_b.reshape(1, ch))


def _moba_kernel(q_ref, k_ref, v_ref, o_ref, qaug_ref, kaug_ref, vaug_ref, s_ref, p_ref, m_ref, acc_ref):
    seq, w = q_ref.shape[1], q_ref.shape[2]
    hd, blk, topk = MOBA_HEAD_DIM, MOBA_BLOCK, MOBA_TOPK
    nblk = seq // blk
    half = blk // 2
    scale = hd ** -0.5
    k = k_ref[0]
    v = v_ref[0]
    kmean = jnp.mean(k.reshape(nblk, blk, w), axis=1)
    lane_k = lax.broadcasted_iota(jnp.int32, (seq, w), 1)
    blk_k = lax.broadcasted_iota(jnp.int32, (seq, w), 0) // blk
    for e in range(2):
        in_head = (lane_k >= e * hd) & (lane_k < (e + 1) * hd)
        onehot = (lane_k - (1 - e) * hd == blk_k).astype(F32)
        kaug_ref[e] = jnp.where(in_head, k, onehot).astype(BF16)
        vaug_ref[e] = jnp.where(in_head, v, 1.0).astype(BF16)

    lane_q = lax.broadcasted_iota(jnp.int32, (blk, w), 1)
    lane_m = lax.broadcasted_iota(jnp.int32, (nblk, w), 1)
    ridx = lax.broadcasted_iota(jnp.int32, (nblk, blk), 0)
    own_causal = (lax.broadcasted_iota(jnp.int32, (blk, blk), 1)
                  <= lax.broadcasted_iota(jnp.int32, (blk, blk), 0))

    for jq in range(nblk):
        qs = slice(jq * blk, (jq + 1) * blk)
        q2 = q_ref[0, qs, :]
        for e in range(2):
            lo = (1 - e) * hd
            if jq > topk:
                kme = jnp.where((lane_m >= e * hd) & (lane_m < (e + 1) * hd), kmean, 0.0)
                st = lax.dot_general(kme, q2, _NT, precision=lax.Precision.HIGHEST,
                                     preferred_element_type=F32)
                valid = ridx < jq
                rows = []
                for n in range(nblk):
                    if n < jq:
                        sn = st[n:n + 1, :]
                        beats = valid & ((st > sn) | ((st == sn) & (ridx < n)))
                        rank = jnp.sum(beats.astype(F32), axis=0, keepdims=True)
                        rows.append(jnp.where(rank < topk, 0.0, NEG))
                    else:
                        rows.append(jnp.zeros((1, blk), F32))
                pieces = [jnp.concatenate(rows, axis=0)]
                if lo:
                    pieces.insert(0, jnp.zeros((lo, blk), F32))
                if w - lo - nblk:
                    pieces.append(jnp.zeros((w - lo - nblk, blk), F32))
                bias_q = jnp.concatenate(pieces, axis=0).T
            else:
                bias_q = jnp.zeros((blk, w), F32)
            in_head = (lane_q >= e * hd) & (lane_q < (e + 1) * hd)
            qaug_ref[e, qs, :] = jnp.where(in_head, q2 * (scale * LOG2E), bias_q).astype(BF16)

    base = [sum(seq - i * blk for i in range(n)) for n in range(nblk)]

    def fold_max(e, rows, s, first):
        t = jnp.maximum(s[:, :half], s[:, half:])
        m_ref[e, rows, :] = t if first else jnp.maximum(m_ref[e, rows, :], t)

    def score(e, n):
        s = lax.dot_general(qaug_ref[e, n * blk:, :], kaug_ref[e, n * blk:(n + 1) * blk, :], _NT,
                            preferred_element_type=F32)
        own = jnp.where(own_causal, s[:blk], NEG)
        s_ref[e, base[n]:base[n] + blk, :] = own
        fold_max(e, slice(n * blk, (n + 1) * blk), own, n == 0)
        if n + 1 < nblk:
            s_ref[e, base[n] + blk:base[n] + seq - n * blk, :] = s[blk:]
            fold_max(e, slice((n + 1) * blk, seq), s[blk:], n == 0)

    def row_max(e):
        m_ref[e] = jnp.broadcast_to(jnp.max(m_ref[e], axis=-1, keepdims=True), (seq, half))

    def weights(e, n):
        rows = slice(base[n], base[n] + seq - n * blk)
        s = s_ref[e, rows, :]
        mb = m_ref[e, n * blk:, :]
        p_ref[e, rows, :half] = jnp.exp2(s[:, :half] - mb).astype(BF16)
        p_ref[e, rows, half:] = jnp.exp2(s[:, half:] - mb).astype(BF16)

    def values(e, n):
        rows = slice(base[n], base[n] + seq - n * blk)
        o = _dot(p_ref[e, rows, :], vaug_ref[e, n * blk:(n + 1) * blk, :])
        if n == 0:
            acc_ref[e] = o
        else:
            acc_ref[e, n * blk:, :] += o

    for n in range(nblk):
        score(0, n)
    row_max(0)
    for n in range(nblk):
        score(1, n)
        weights(0, n)
    row_max(1)
    for n in range(nblk):
        values(0, n)
        weights(1, n)
    for n in range(nblk):
        values(1, n)
    outs = []
    for e in range(2):
        lo = (1 - e) * hd
        o = acc_ref[e]
        outs.append(o * (1.0 / o[:, lo:lo + 1]))
    lane_o = lax.broadcasted_iota(jnp.int32, (seq, w), 1)
    o_ref[0] = jnp.where(lane_o < hd, outs[0], outs[1]).astype(o_ref.dtype)


def _moba(proj, width):
    b, s, _ = proj.shape
    w = 2 * MOBA_HEAD_DIM
    pairs = width // w
    nblk = s // MOBA_BLOCK
    tiles = nblk * (nblk + 1) // 2
    sec = lambda k: pl.BlockSpec((1, s, w), lambda i, h: (i, 0, k * pairs + h))
    return pl.pallas_call(
        _moba_kernel,
        grid=(b, pairs),
        in_specs=[sec(0), sec(1), sec(2)],
        out_specs=pl.BlockSpec((1, s, w), lambda i, h: (i, 0, h)),
        out_shape=jax.ShapeDtypeStruct((b, s, width), BF16),
        scratch_shapes=[pltpu.VMEM((2, s, w), BF16), pltpu.VMEM((2, s, w), BF16), pltpu.VMEM((2, s, w), BF16),
                        pltpu.VMEM((2, tiles * MOBA_BLOCK, MOBA_BLOCK), F32),
                        pltpu.VMEM((2, tiles * MOBA_BLOCK, MOBA_BLOCK), BF16),
                        pltpu.VMEM((2, s, MOBA_BLOCK // 2), F32),
                        pltpu.VMEM((2, s, w), F32)],
        compiler_params=_params("parallel", "parallel"),
        name="moba",
    )(proj, proj, proj)


def _sgu_kernel(u_ref, z_ref, lg_ref, lb_ref, w_ref, bias_ref, o_ref):
    ts, width = u_ref.shape[1], u_ref.shape[2]
    c = SGU_CHUNK
    gd = width // SGU_GROUPS
    row = lax.broadcasted_iota(jnp.int32, (c, c), 0)
    col = lax.broadcasted_iota(jnp.int32, (c, c), 1)
    for g in range(SGU_GROUPS):
        gs = slice(g * gd, (g + 1) * gd)
        z = _gelu(z_ref[0, :, gs])
        mu = jnp.mean(z, axis=-1, keepdims=True)
        d = z - mu
        zn = (d * lax.rsqrt(jnp.mean(d * d, axis=-1, keepdims=True) + EPS) * lg_ref[:, gs]
              + lb_ref[:, gs]).astype(BF16)
        wg = jnp.where(col <= row, w_ref[g], 0.0).astype(BF16)
        for n in range(ts // c):
            ts_ = slice(n * c, (n + 1) * c)
            mixed = _dot(wg, zn[ts_, :]) + bias_ref[:, gs]
            o_ref[0, ts_, gs] = (_gelu(u_ref[0, ts_, gs]) * mixed).astype(o_ref.dtype)


def _sgu(proj, col_u, col_z, ln_g, ln_b, w, bias):
    b, s, _ = proj.shape
    width = ln_g.shape[0]
    groups, c, _ = w.shape
    ts = SGU_ROWS
    bias_full = jnp.repeat(bias.T, width // groups, axis=1)
    return pl.pallas_call(
        _sgu_kernel,
        grid=(b, s // ts),
        in_specs=[pl.BlockSpec((1, ts, width), lambda i, j: (i, j, col_u)),
                  pl.BlockSpec((1, ts, width), lambda i, j: (i, j, col_z)),
                  pl.BlockSpec((1, width), lambda i, j: (0, 0)),
                  pl.BlockSpec((1, width), lambda i, j: (0, 0)),
                  pl.BlockSpec((groups, c, c), lambda i, j: (0, 0, 0)),
                  pl.BlockSpec((c, width), lambda i, j: (0, 0))],
        out_specs=pl.BlockSpec((1, ts, width), lambda i, j: (i, j, 0)),
        out_shape=jax.ShapeDtypeStruct((b, s, width), BF16),
        compiler_params=_params("parallel", "parallel"),
        name="sgu",
    )(proj, proj, ln_g.reshape(1, width), ln_b.reshape(1, width), w, bias_full)


def _cross_attention(x, g, wq_ref, kv_ref, wo_ref):
    d = x.shape[1]
    hd = d // XA_HEADS
    q = (_dot(_rms(x, g).astype(BF16), wq_ref[...]) * (hd ** -0.5 * LOG2E)).astype(BF16)

    def logits(h):
        hs = slice(h * hd, (h + 1) * hd)
        s = lax.dot_general(q[:, hs], kv_ref[0, :, hs], _NT, preferred_element_type=F32)
        return s, jnp.max(s, axis=-1, keepdims=True)

    pending = logits(0)
    outs = []
    for h in range(XA_HEADS):
        ahead = logits(h + 1) if h + 1 < XA_HEADS else None
        s, m = pending
        p = jnp.exp2(s - m)
        o = _dot(p.astype(BF16), kv_ref[0, :, d + h * hd:d + (h + 1) * hd])
        outs.append((o * (1.0 / jnp.sum(p, axis=-1, keepdims=True))).astype(BF16))
        pending = ahead
    return x + _dot(jnp.concatenate(outs, axis=-1), wo_ref[...])


def _swiglu(x, g, w1_ref, w2_ref):
    hidden = w2_ref.shape[0]
    h = _rms(x, g).astype(BF16)
    acc = x
    for c in range(hidden // FFN_CHUNK):
        cs = slice(c * FFN_CHUNK, (c + 1) * FFN_CHUNK)
        gs = slice(hidden + c * FFN_CHUNK, hidden + (c + 1) * FFN_CHUNK)
        u = (_silu(_dot(h, w1_ref[:, cs])) * _dot(h, w1_ref[:, gs])).astype(BF16)
        acc = acc + _dot(u, w2_ref[cs, :])
    return acc


def _post_mixer_kernel(x_ref, a_ref, b_ref, wmix_ref, gx_ref, wq_ref, kv_ref, wo_ref,
                       gf_ref, w1_ref, w2_ref, fg_ref, o_ref, wmix_b, wq_b, wo_b, *, final):
    _cast_once(wmix_b, wmix_ref)
    _cast_once(wq_b, wq_ref)
    _cast_once(wo_b, wo_ref)
    ka = a_ref.shape[1]
    x = x_ref[...] + _dot(a_ref[...], wmix_b[0:ka, :]) + _dot(b_ref[...], wmix_b[ka:, :])
    x = _cross_attention(x, gx_ref[...], wq_b, kv_ref, wo_b)
    x = _swiglu(x, gf_ref[...], w1_ref, w2_ref)
    o_ref[...] = _rms(x, fg_ref[...]) if final else x


def _post_mixer(x2d, a2d, b2d, w_mix, mix_layer, gx, wq, kv, wo, layer, gf, w1, w2, final_gain, final, seq):
    t, d = x2d.shape
    ka, kb = a2d.shape[1], b2d.shape[1]
    m = kv.shape[1]
    hidden = w2.shape[0]
    per_batch = seq // ROW_TILE
    row = lambda n: pl.BlockSpec((ROW_TILE, n), lambda i: (i, 0))
    vec = lambda: pl.BlockSpec((1, d), lambda i: (0, 0))
    resident = lambda shape: pl.BlockSpec(shape, lambda i: (0, 0), pipeline_mode=pl.Buffered(1))
    return pl.pallas_call(
        functools.partial(_post_mixer_kernel, final=final),
        grid=(t // ROW_TILE,),
        in_specs=[row(d), row(ka), row(kb), _resident(w_mix, mix_layer),
                  vec(), _resident(wq, layer),
                  pl.BlockSpec((1, m, 2 * d), lambda i: (i // per_batch, 0, 0)), _resident(wo, layer),
                  vec(), resident((d, 2 * hidden)), resident((hidden, d)), vec()],
        out_specs=row(d),
        out_shape=jax.ShapeDtypeStruct((t, d), F32),
        scratch_shapes=[pltpu.VMEM((ka + kb, d), BF16), pltpu.VMEM((d, d), BF16), pltpu.VMEM((d, d), BF16)],
        compiler_params=_params("arbitrary"),
        name="post_mixer",
    )(x2d, a2d, b2d, w_mix, gx.reshape(1, d), wq, kv, wo,
      gf.reshape(1, d), w1, w2, final_gain.reshape(1, d))


def kernel(x, mem, norm_mix, norm_xattn, norm_ffn, mem_norm, final_norm, w_in_ab, w_out_ab, hgrn_lower_bounds, hgrn_out_norm, conv_dw_w, conv_dw_b, conv_ln_g, conv_ln_b, w_in_cd, w_out_cd, sgu_ln_g, sgu_ln_b, sgu_w, sgu_b, xa_wq, xa_wkv, xa_wo, ffn_w_in, ffn_w_out):
    b, s, d = x.shape
    m = mem.shape[1]
    depth = norm_mix.shape[0]
    bf = lambda a: a.astype(BF16)
    x2d = x.reshape(b * s, d)
    mem2d = mem.reshape(b * m, d)
    for l in range(depth):
        if l % 2 == 0:
            e = l // 2
            a_width = hgrn_out_norm.shape[1]
            b_width = conv_dw_w.shape[2]
            proj = _norm_matmul(x2d, norm_mix[l], w_in_ab, e, F32).reshape(b, s, -1)
            o_a = _hgrn(proj, hgrn_lower_bounds, hgrn_out_norm[e], l, a_width)
            col = 4 * a_width // b_width
            o_b = _conv_module(proj, col, col + 1, conv_dw_w[e], conv_dw_b[e], conv_ln_g[e], conv_ln_b[e])
            w_out, mix_layer = w_out_ab, e
        else:
            o = l // 2
            d_width = sgu_ln_g.shape[1]
            c_width = w_out_cd.shape[1] - d_width
            proj = _norm_matmul(x2d, norm_mix[l], w_in_cd, o, F32).reshape(b, s, -1)
            o_a = _moba(proj, c_width)
            col = 3 * c_width // d_width
            o_b = _sgu(proj, col, col + 1, sgu_ln_g[o], sgu_ln_b[o], sgu_w[o], sgu_b[o])
            w_out, mix_layer = w_out_cd, o
        kv = _norm_matmul(mem2d, mem_norm, xa_wkv, l, BF16).reshape(b, m, 2 * d)
        x2d = _post_mixer(x2d, o_a.reshape(b * s, -1), o_b.reshape(b * s, -1), w_out, mix_layer,
                          norm_xattn[l], xa_wq, kv, xa_wo, l,
                          norm_ffn[l], bf(ffn_w_in[l]), bf(ffn_w_out[l]), final_norm, l == depth - 1, s)
    return x2d.reshape(b, s, d)
```

```python
import functools

import jax
import jax.numpy as jnp
from jax import lax
from jax.experimental import pallas as pl
from jax.experimental.pallas import tpu as pltpu

F32 = jnp.float32
BF16 = jnp.bfloat16
EPS = 1e-6
NEG = -1e30
LOG2E = 1.4426950408889634

V7X_VMEM_BYTES = 64 * 1024 * 1024
VMEM_LIMIT = V7X_VMEM_BYTES - 8 * 1024 * 1024
SUBLANES = 8
LANES = 128

HGRN_HEAD_DIM = 128
HGRN_CHUNK = 64
HGRN_GROUP = 4
HGRN_UNROLL = 8
CONV_TILE = 64
CONV_NORM_ROWS = 256
MOBA_HEAD_DIM = 64
MOBA_BLOCK = 256
MOBA_TOPK = 3
SGU_CHUNK = 128
SGU_GROUPS = 4
SGU_ROWS = 1024
XA_HEADS = 4
ROW_TILE = 512
FFN_CHUNK = 256
POST_CAST_STEPS = 16

_NT = (((1,), (1,)), ((), ()))
_TN = (((0,), (0,)), ((), ()))


def _params(*sem):
    return pltpu.CompilerParams(dimension_semantics=sem, vmem_limit_bytes=VMEM_LIMIT)


def _rms(x, g):
    return x * lax.rsqrt(jnp.mean(x * x, axis=-1, keepdims=True) + EPS) * g


def _sigmoid(x):
    return 1.0 / (1.0 + jnp.exp2(x * -LOG2E))


def _silu(x):
    return x * _sigmoid(x)


def _gelu(x):
    return 0.5 * x * (1.0 + lax.erf(x * (2.0 ** -0.5)))


def _dot(a, b):
    return jnp.dot(a, b, preferred_element_type=F32)


def _dot01_f32(m01, x):
    hi = x.astype(BF16)
    r1 = x - hi.astype(F32)
    mid = r1.astype(BF16)
    lo = (r1 - mid.astype(F32)).astype(BF16)
    return _dot(m01, hi) + _dot(m01, mid) + _dot(m01, lo)


def _resident(stacked, layer):
    return pl.BlockSpec((None,) + stacked.shape[1:], lambda i: (layer, 0, 0), pipeline_mode=pl.Buffered(1))


def _cast_once(dst_ref, src_ref):
    @pl.when(pl.program_id(0) == 0)
    def _():
        dst_ref[...] = src_ref[...].astype(BF16)


def _norm_matmul_kernel(x_ref, g_ref, w_ref, o_ref, wb_ref):
    _cast_once(wb_ref, w_ref)
    h = _rms(x_ref[...], g_ref[...]).astype(BF16)
    o_ref[...] = _dot(h, wb_ref[...]).astype(o_ref.dtype)


def _norm_matmul(x2d, g, w_stacked, layer, out_dtype):
    t, d = x2d.shape
    n = w_stacked.shape[2]
    return pl.pallas_call(
        _norm_matmul_kernel,
        grid=(t // ROW_TILE,),
        in_specs=[pl.BlockSpec((ROW_TILE, d), lambda i: (i, 0)),
                  pl.BlockSpec((1, d), lambda i: (0, 0)),
                  _resident(w_stacked, layer)],
        out_specs=pl.BlockSpec((ROW_TILE, n), lambda i: (i, 0)),
        out_shape=jax.ShapeDtypeStruct((t, n), out_dtype),
        scratch_shapes=[pltpu.VMEM((d, n), BF16)],
        compiler_params=_params("arbitrary"),
        name="norm_matmul",
    )(x2d, g.reshape(1, d), w_stacked)


def _hgrn_kernel(q_ref, f_ref, i_ref, g_ref, lbz_ref, on_ref, o_ref, *, layer):
    seq, dk = q_ref.shape[1], q_ref.shape[2]
    c, grp = HGRN_CHUNK, HGRN_GROUP
    rows = c * grp
    lbz = lbz_ref[...]
    e = jnp.exp(lbz - jnp.max(lbz, axis=0, keepdims=True))
    lb = jnp.sum(e[:layer + 1], axis=0, keepdims=True) / jnp.sum(e, axis=0, keepdims=True)
    on = on_ref[...]
    row = lax.broadcasted_iota(jnp.int32, (rows, rows), 0)
    col = lax.broadcasted_iota(jnp.int32, (rows, rows), 1)
    causal = (col <= row) & (col >= (row // c) * c)
    tril = causal.astype(BF16)

    def body(n, st):
        sl = pl.ds(pl.multiple_of(n * rows, rows), rows)
        qz, fz, v, gz = q_ref[0, sl, :], f_ref[0, sl, :], i_ref[0, sl, :], g_ref[0, sl, :]
        f = lb + (1.0 - lb) * _sigmoid(fz)
        cum = _dot01_f32(tril, jnp.log(f)) * LOG2E
        k = 1.0 - f
        q_in = (_silu(qz) * jnp.exp2(cum)).astype(BF16)
        k_in = (k * jnp.exp2(-cum)).astype(BF16)
        att = lax.dot_general(q_in, k_in, _NT, preferred_element_type=F32)
        att = jnp.where(causal, att, 0.0).astype(BF16)
        vb = v.astype(BF16)
        o_intra = _dot(att, vb)
        cum3 = cum.reshape(grp, c, dk)
        cl = cum3[:, c - 1:c, :]
        kdec = (k.reshape(grp, c, dk) * jnp.exp2(cl - cum3)).astype(BF16)
        decay = jnp.exp2(cl)
        kv = [lax.dot_general(vb[j * c:(j + 1) * c], kdec[j], _TN, preferred_element_type=F32)
              for j in range(grp)]
        o_inter = []
        for j in range(grp):
            o_inter.append(lax.dot_general(q_in[j * c:(j + 1) * c], st.astype(BF16), _NT,
                                           preferred_element_type=F32))
            st = decay[j] * st + kv[j]
        o = o_intra + jnp.concatenate(o_inter, axis=0)
        o = o * lax.rsqrt(jnp.mean(o * o, axis=-1, keepdims=True) + EPS)
        o_ref[0, sl, :] = (o * on * _silu(gz)).astype(o_ref.dtype)
        return st

    lax.fori_loop(0, seq // rows, body, jnp.zeros((dk, dk), F32), unroll=HGRN_UNROLL)


def _hgrn(proj, lbz, out_norm, layer, width):
    b, s, _ = proj.shape
    hd = HGRN_HEAD_DIM
    heads = width // hd
    sec = lambda k: pl.BlockSpec((1, s, hd), lambda i, h: (i, 0, k * heads + h))
    return pl.pallas_call(
        functools.partial(_hgrn_kernel, layer=layer),
        grid=(b, heads),
        in_specs=[sec(0), sec(1), sec(2), sec(3),
                  pl.BlockSpec((lbz.shape[0], hd), lambda i, h: (0, h)),
                  pl.BlockSpec((1, hd), lambda i, h: (0, h))],
        out_specs=pl.BlockSpec((1, s, hd), lambda i, h: (i, 0, h)),
        out_shape=jax.ShapeDtypeStruct((b, s, width), BF16),
        compiler_params=_params("parallel", "parallel"),
        name="hgrn2",
    )(proj, proj, proj, proj, lbz, out_norm.reshape(1, width))


def _conv_kernel(a_ref, b_ref, w_ref, db_ref, lg_ref, lb_ref, o_ref, cpad_ref, acc_ref):
    seq, ch = a_ref.shape[1], a_ref.shape[2]
    kw = w_ref.shape[0]
    pad = cpad_ref.shape[0] - seq
    tt = CONV_TILE
    cpad_ref[0:pad, :] = jnp.zeros((pad, ch), F32)

    def fill(t, carry):
        sl = pl.ds(pl.multiple_of(t * tt, tt), tt)
        cpad_ref[pl.ds(pl.multiple_of(pad + t * tt, SUBLANES), tt), :] = (
            a_ref[0, sl, :] * _sigmoid(b_ref[0, sl, :]))
        return carry

    lax.fori_loop(0, seq // tt, fill, 0)
    db, lg, lb = db_ref[...], lg_ref[...], lb_ref[...]

    offs = [pad - kw + 1 + k for k in range(kw)]
    sup = acc_ref.shape[0]
    lane_blocks = ch // LANES

    def conv(s, carry):
        sbase = pl.multiple_of(s * sup, sup)

        def block(i, c2):
            t, j = i // lane_blocks, i % lane_blocks
            base = pl.multiple_of(sbase + t * tt, tt)
            ls = pl.ds(pl.multiple_of(j * LANES, LANES), LANES)
            win = cpad_ref[pl.ds(base, tt + pad), ls]
            acc = jnp.zeros((tt, LANES), F32)
            for r in range(SUBLANES):
                taps = [k for k in range(kw) if offs[k] % SUBLANES == r]
                if not taps:
                    continue
                shifted = pltpu.roll(win, tt + pad - r, axis=0) if r else win
                for k in taps:
                    acc = acc + w_ref[k:k + 1, ls] * shifted[offs[k] - r:offs[k] - r + tt, :]
            acc_ref[pl.ds(pl.multiple_of(t * tt, tt), tt), ls] = acc
            return c2

        lax.fori_loop(0, (sup // tt) * lane_blocks, block, 0)
        acc = acc_ref[...] + db
        mu = jnp.mean(acc, axis=-1, keepdims=True)
        d = acc - mu
        y = d * lax.rsqrt(jnp.mean(d * d, axis=-1, keepdims=True) + EPS) * lg + lb
        o_ref[0, pl.ds(sbase, sup), :] = _silu(y).astype(o_ref.dtype)
        return carry

    lax.fori_loop(0, seq // sup, conv, 0)


def _conv_module(proj, col_a, col_b, dw_w, dw_b, ln_g, ln_b):
    b, s, _ = proj.shape
    kw, ch = dw_w.shape
    pad = -(-(kw - 1) // SUBLANES) * SUBLANES
    vec = lambda: pl.BlockSpec((1, ch), lambda i: (0, 0))
    return pl.pallas_call(
        _conv_kernel,
        grid=(b,),
        in_specs=[pl.BlockSpec((1, s, ch), lambda i: (i, 0, col_a)),
                  pl.BlockSpec((1, s, ch), lambda i: (i, 0, col_b)),
                  pl.BlockSpec((kw, ch), lambda i: (0, 0)), vec(), vec(), vec()],
        out_specs=pl.BlockSpec((1, s, ch), lambda i: (i, 0, 0)),
        out_shape=jax.ShapeDtypeStruct((b, s, ch), BF16),
        scratch_shapes=[pltpu.VMEM((pad + s, ch), F32), pltpu.VMEM((CONV_NORM_ROWS, ch), F32)],
        compiler_params=_params("parallel"),
        name="conv_module",
    )(proj, proj, dw_w, dw_b.reshape(1, ch), ln_g.reshape(1, ch), ln_b.reshape(1, ch))


def _moba_kernel(q_ref, k_ref, v_ref, o_ref, qaug_ref, kaug_ref, vaug_ref, s_ref, p_ref, m_ref, acc_ref):
    seq, w = q_ref.shape[1], q_ref.shape[2]
    hd, blk, topk = MOBA_HEAD_DIM, MOBA_BLOCK, MOBA_TOPK
    nblk = seq // blk
    half = blk // 2
    scale = hd ** -0.5
    k = k_ref[0]
    v = v_ref[0]
    kmean = jnp.mean(k.reshape(nblk, blk, w), axis=1)
    lane_k = lax.broadcasted_iota(jnp.int32, (seq, w), 1)
    blk_k = lax.broadcasted_iota(jnp.int32, (seq, w), 0) // blk
    for e in range(2):
        in_head = (lane_k >= e * hd) & (lane_k < (e + 1) * hd)
        onehot = (lane_k - (1 - e) * hd == blk_k).astype(F32)
        kaug_ref[e] = jnp.where(in_head, k, onehot).astype(BF16)
        vaug_ref[e] = jnp.where(in_head, v, 1.0).astype(BF16)

    lane_q = lax.broadcasted_iota(jnp.int32, (blk, w), 1)
    lane_m = lax.broadcasted_iota(jnp.int32, (nblk, w), 1)
    ridx = lax.broadcasted_iota(jnp.int32, (nblk, blk), 0)
    own_causal = (lax.broadcasted_iota(jnp.int32, (blk, blk), 1)
                  <= lax.broadcasted_iota(jnp.int32, (blk, blk), 0))

    for jq in range(nblk):
        qs = slice(jq * blk, (jq + 1) * blk)
        q2 = q_ref[0, qs, :]
        for e in range(2):
            lo = (1 - e) * hd
            if jq > topk:
                kme = jnp.where((lane_m >= e * hd) & (lane_m < (e + 1) * hd), kmean, 0.0)
                st = lax.dot_general(kme, q2, _NT, precision=lax.Precision.HIGHEST,
                                     preferred_element_type=F32)
                valid = ridx < jq
                rows = []
                for n in range(nblk):
                    if n < jq:
                        sn = st[n:n + 1, :]
                        beats = valid & ((st > sn) | ((st == sn) & (ridx < n)))
                        rank = jnp.sum(beats.astype(F32), axis=0, keepdims=True)
                        rows.append(jnp.where(rank < topk, 0.0, NEG))
                    else:
                        rows.append(jnp.zeros((1, blk), F32))
                pieces = [jnp.concatenate(rows, axis=0)]
                if lo:
                    pieces.insert(0, jnp.zeros((lo, blk), F32))
                if w - lo - nblk:
                    pieces.append(jnp.zeros((w - lo - nblk, blk), F32))
                bias_q = jnp.concatenate(pieces, axis=0).T
            else:
                bias_q = jnp.zeros((blk, w), F32)
            in_head = (lane_q >= e * hd) & (lane_q < (e + 1) * hd)
            qaug_ref[e, qs, :] = jnp.where(in_head, q2 * (scale * LOG2E), bias_q).astype(BF16)

    base = [sum(seq - i * blk for i in range(n)) for n in range(nblk)]

    def fold_max(e, rows, s, first):
        t = jnp.maximum(s[:, :half], s[:, half:])
        m_ref[e, rows, :] = t if first else jnp.maximum(m_ref[e, rows, :], t)

    def score(e, n):
        s = lax.dot_general(qaug_ref[e, n * blk:, :], kaug_ref[e, n * blk:(n + 1) * blk, :], _NT,
                            preferred_element_type=F32)
        own = jnp.where(own_causal, s[:blk], NEG)
        s_ref[e, base[n]:base[n] + blk, :] = own
        fold_max(e, slice(n * blk, (n + 1) * blk), own, n == 0)
        if n + 1 < nblk:
            s_ref[e, base[n] + blk:base[n] + seq - n * blk, :] = s[blk:]
            fold_max(e, slice((n + 1) * blk, seq), s[blk:], n == 0)

    def row_max(e):
        m_ref[e] = jnp.broadcast_to(jnp.max(m_ref[e], axis=-1, keepdims=True), (seq, half))

    def weights(e, n):
        rows = slice(base[n], base[n] + seq - n * blk)
        s = s_ref[e, rows, :]
        mb = m_ref[e, n * blk:, :]
        p_ref[e, rows, :half] = jnp.exp2(s[:, :half] - mb).astype(BF16)
        p_ref[e, rows, half:] = jnp.exp2(s[:, half:] - mb).astype(BF16)

    def values(e, n):
        rows = slice(base[n], base[n] + seq - n * blk)
        o = _dot(p_ref[e, rows, :], vaug_ref[e, n * blk:(n + 1) * blk, :])
        if n == 0:
            acc_ref[e] = o
        else:
            acc_ref[e, n * blk:, :] += o

    for n in range(nblk):
        score(0, n)
    row_max(0)
    for n in range(nblk):
        score(1, n)
        weights(0, n)
    row_max(1)
    for n in range(nblk):
        values(0, n)
        weights(1, n)
    for n in range(nblk):
        values(1, n)
    outs = []
    for e in range(2):
        lo = (1 - e) * hd
        o = acc_ref[e]
        outs.append(o * (1.0 / o[:, lo:lo + 1]))
    lane_o = lax.broadcasted_iota(jnp.int32, (seq, w), 1)
    o_ref[0] = jnp.where(lane_o < hd, outs[0], outs[1]).astype(o_ref.dtype)


def _moba(proj, width):
    b, s, _ = proj.shape
    w = 2 * MOBA_HEAD_DIM
    pairs = width // w
    nblk = s // MOBA_BLOCK
    tiles = nblk * (nblk + 1) // 2
    sec = lambda k: pl.BlockSpec((1, s, w), lambda i, h: (i, 0, k * pairs + h))
    return pl.pallas_call(
        _moba_kernel,
        grid=(b, pairs),
        in_specs=[sec(0), sec(1), sec(2)],
        out_specs=pl.BlockSpec((1, s, w), lambda i, h: (i, 0, h)),
        out_shape=jax.ShapeDtypeStruct((b, s, width), BF16),
        scratch_shapes=[pltpu.VMEM((2, s, w), BF16), pltpu.VMEM((2, s, w), BF16), pltpu.VMEM((2, s, w), BF16),
                        pltpu.VMEM((2, tiles * MOBA_BLOCK, MOBA_BLOCK), F32),
                        pltpu.VMEM((2, tiles * MOBA_BLOCK, MOBA_BLOCK), BF16),
                        pltpu.VMEM((2, s, MOBA_BLOCK // 2), F32),
                        pltpu.VMEM((2, s, w), F32)],
        compiler_params=_params("parallel", "parallel"),
        name="moba",
    )(proj, proj, proj)


def _sgu_kernel(u_ref, z_ref, lg_ref, lb_ref, w_ref, bias_ref, o_ref):
    ts, width = u_ref.shape[1], u_ref.shape[2]
    c = SGU_CHUNK
    gd = width // SGU_GROUPS
    row = lax.broadcasted_iota(jnp.int32, (c, c), 0)
    col = lax.broadcasted_iota(jnp.int32, (c, c), 1)
    for g in range(SGU_GROUPS):
        gs = slice(g * gd, (g + 1) * gd)
        z = _gelu(z_ref[0, :, gs])
        mu = jnp.mean(z, axis=-1, keepdims=True)
        d = z - mu
        zn = (d * lax.rsqrt(jnp.mean(d * d, axis=-1, keepdims=True) + EPS) * lg_ref[:, gs]
              + lb_ref[:, gs]).astype(BF16)
        wg = jnp.where(col <= row, w_ref[g], 0.0).astype(BF16)
        for n in range(ts // c):
            ts_ = slice(n * c, (n + 1) * c)
            mixed = _dot(wg, zn[ts_, :]) + bias_ref[:, gs]
            o_ref[0, ts_, gs] = (_gelu(u_ref[0, ts_, gs]) * mixed).astype(o_ref.dtype)


def _sgu(proj, col_u, col_z, ln_g, ln_b, w, bias):
    b, s, _ = proj.shape
    width = ln_g.shape[0]
    groups, c, _ = w.shape
    ts = SGU_ROWS
    bias_full = jnp.repeat(bias.T, width // groups, axis=1)
    return pl.pallas_call(
        _sgu_kernel,
        grid=(b, s // ts),
        in_specs=[pl.BlockSpec((1, ts, width), lambda i, j: (i, j, col_u)),
                  pl.BlockSpec((1, ts, width), lambda i, j: (i, j, col_z)),
                  pl.BlockSpec((1, width), lambda i, j: (0, 0)),
                  pl.BlockSpec((1, width), lambda i, j: (0, 0)),
                  pl.BlockSpec((groups, c, c), lambda i, j: (0, 0, 0)),
                  pl.BlockSpec((c, width), lambda i, j: (0, 0))],
        out_specs=pl.BlockSpec((1, ts, width), lambda i, j: (i, j, 0)),
        out_shape=jax.ShapeDtypeStruct((b, s, width), BF16),
        compiler_params=_params("parallel", "parallel"),
        name="sgu",
    )(proj, proj, ln_g.reshape(1, width), ln_b.reshape(1, width), w, bias_full)


def _cross_attention(x, g, wq_ref, kv_ref, wo_ref):
    d = x.shape[1]
    hd = d // XA_HEADS
    q = (_dot(_rms(x, g).astype(BF16), wq_ref[...]) * (hd ** -0.5 * LOG2E)).astype(BF16)

    def logits(h):
        hs = slice(h * hd, (h + 1) * hd)
        s = lax.dot_general(q[:, hs], kv_ref[0, :, hs], _NT, preferred_element_type=F32)
        return s, jnp.max(s, axis=-1, keepdims=True)

    pending = logits(0)
    outs = []
    for h in range(XA_HEADS):
        ahead = logits(h + 1) if h + 1 < XA_HEADS else None
        s, m = pending
        p = jnp.exp2(s - m)
        o = _dot(p.astype(BF16), kv_ref[0, :, d + h * hd:d + (h + 1) * hd])
        outs.append((o * (1.0 / jnp.sum(p, axis=-1, keepdims=True))).astype(BF16))
        pending = ahead
    return x + _dot(jnp.concatenate(outs, axis=-1), wo_ref[...])


def _swiglu(x, g, w1_ref, w2_ref):
    hidden = w2_ref.shape[0]
    h = _rms(x, g).astype(BF16)
    acc = x
    for c in range(hidden // FFN_CHUNK):
        cs = slice(c * FFN_CHUNK, (c + 1) * FFN_CHUNK)
        gs = slice(hidden + c * FFN_CHUNK, hidden + (c + 1) * FFN_CHUNK)
        u = (_silu(_dot(h, w1_ref[:, cs])) * _dot(h, w1_ref[:, gs])).astype(BF16)
        acc = acc + _dot(u, w2_ref[cs, :])
    return acc


def _post_mixer_kernel(x_ref, a_ref, b_ref, gx_ref, kv_ref, gf_ref, fg_ref,
                       wmix_c, wq_c, wo_c, w1_c, w2_c, o_ref,
                       wmix_b, wq_b, wo_b, w1_b, w2_b, *, final):
    i = pl.program_id(0)

    @pl.when(i < POST_CAST_STEPS)
    def _():
        for src, dst in ((wmix_c, wmix_b), (wq_c, wq_b), (wo_c, wo_b), (w1_c, w1_b), (w2_c, w2_b)):
            rows = src.shape[0]
            dst[pl.ds(pl.multiple_of(i * rows, rows), rows), :] = src[...].astype(BF16)

    @pl.when(i >= POST_CAST_STEPS)
    def _():
        ka = a_ref.shape[1]
        x = x_ref[...] + _dot(a_ref[...], wmix_b[0:ka, :]) + _dot(b_ref[...], wmix_b[ka:, :])
        x = _cross_attention(x, gx_ref[...], wq_b, kv_ref, wo_b)
        x = _swiglu(x, gf_ref[...], w1_b, w2_b)
        o_ref[...] = _rms(x, fg_ref[...]) if final else x


def _post_mixer(x2d, a2d, b2d, w_mix, mix_layer, gx, wq, kv, wo, gf, w1, w2, layer, final_gain, final, seq):
    t, d = x2d.shape
    ka, kb = a2d.shape[1], b2d.shape[1]
    m = kv.shape[1]
    nc = POST_CAST_STEPS
    per_batch = seq // ROW_TILE
    tile = lambda i: jnp.maximum(i - nc, 0)
    row = lambda n: pl.BlockSpec((ROW_TILE, n), lambda i: (tile(i), 0))
    vec = lambda: pl.BlockSpec((1, d), lambda i: (0, 0))

    def chunk(w, idx):
        return pl.BlockSpec((None, w.shape[1] // nc, w.shape[2]), lambda i: (idx, jnp.minimum(i, nc - 1), 0))

    weights = ((w_mix, mix_layer), (wq, layer), (wo, layer), (w1, layer), (w2, layer))
    return pl.pallas_call(
        functools.partial(_post_mixer_kernel, final=final),
        grid=(nc + t // ROW_TILE,),
        in_specs=[row(d), row(ka), row(kb), vec(),
                  pl.BlockSpec((1, m, 2 * d), lambda i: (tile(i) // per_batch, 0, 0)), vec(), vec()]
                 + [chunk(w, idx) for w, idx in weights],
        out_specs=row(d),
        out_shape=jax.ShapeDtypeStruct((t, d), F32),
        scratch_shapes=[pltpu.VMEM(w.shape[1:], BF16) for w, _ in weights],
        compiler_params=_params("arbitrary"),
        name="post_mixer",
    )(x2d, a2d, b2d, gx.reshape(1, d), kv, gf.reshape(1, d), final_gain.reshape(1, d),
      *[w for w, _ in weights])


def kernel(x, mem, norm_mix, norm_xattn, norm_ffn, mem_norm, final_norm, w_in_ab, w_out_ab, hgrn_lower_bounds, hgrn_out_norm, conv_dw_w, conv_dw_b, conv_ln_g, conv_ln_b, w_in_cd, w_out_cd, sgu_ln_g, sgu_ln_b, sgu_w, sgu_b, xa_wq, xa_wkv, xa_wo, ffn_w_in, ffn_w_out):
    b, s, d = x.shape
    m = mem.shape[1]
    depth = norm_mix.shape[0]
    x2d = x.reshape(b * s, d)
    mem2d = mem.reshape(b * m, d)
    for l in range(depth):
        if l % 2 == 0:
            e = l // 2
            a_width = hgrn_out_norm.shape[1]
            b_width = conv_dw_w.shape[2]
            proj = _norm_matmul(x2d, norm_mix[l], w_in_ab, e, F32).reshape(b, s, -1)
            o_a = _hgrn(proj, hgrn_lower_bounds, hgrn_out_norm[e], l, a_width)
            col = 4 * a_width // b_width
            o_b = _conv_module(proj, col, col + 1, conv_dw_w[e], conv_dw_b[e], conv_ln_g[e], conv_ln_b[e])
            w_out, mix_layer = w_out_ab, e
        else:
            o = l // 2
            d_width = sgu_ln_g.shape[1]
            c_width = w_out_cd.shape[1] - d_width
            proj = _norm_matmul(x2d, norm_mix[l], w_in_cd, o, F32).reshape(b, s, -1)
            o_a = _moba(proj, c_width)
            col = 3 * c_width // d_width
            o_b = _sgu(proj, col, col + 1, sgu_ln_g[o], sgu_ln_b[o], sgu_w[o], sgu_b[o])
            w_out, mix_layer = w_out_cd, o
        kv = _norm_matmul(mem2d, mem_norm, xa_wkv, l, BF16).reshape(b, m, 2 * d)
        x2d = _post_mixer(x2d, o_a.reshape(b * s, -1), o_b.reshape(b * s, -1), w_out, mix_layer,
                          norm_xattn[l], xa_wq, kv, xa_wo, norm_ffn[l], ffn_w_in, ffn_w_out, l,
                          final_norm, l == depth - 1, s)
    return x2d.reshape(b, s, d)
```

```python
import functools

import jax
import jax.numpy as jnp
from jax import lax
from jax.experimental import pallas as pl
from jax.experimental.pallas import tpu as pltpu

F32 = jnp.float32
BF16 = jnp.bfloat16
EPS = 1e-6
NEG = -1e30
LOG2E = 1.4426950408889634

V7X_VMEM_BYTES = 64 * 1024 * 1024
VMEM_LIMIT = V7X_VMEM_BYTES - 8 * 1024 * 1024
SUBLANES = 8
LANES = 128

HGRN_HEAD_DIM = 128
HGRN_CHUNK = 64
HGRN_GROUP = 4
CONV_TILE = 64
CONV_NORM_ROWS = 256
MOBA_HEAD_DIM = 64
MOBA_BLOCK = 256
MOBA_TOPK = 3
SGU_CHUNK = 128
SGU_GROUPS = 4
SGU_ROWS = 1024
XA_HEADS = 4
ROW_TILE = 512
FFN_CHUNK = 256
POST_CAST_STEPS = 16

_NT = (((1,), (1,)), ((), ()))
_TN = (((0,), (0,)), ((), ()))


def _params(*sem):
    return pltpu.CompilerParams(dimension_semantics=sem, vmem_limit_bytes=VMEM_LIMIT)


def _rms(x, g):
    return x * lax.rsqrt(jnp.mean(x * x, axis=-1, keepdims=True) + EPS) * g


def _sigmoid(x):
    return 1.0 / (1.0 + jnp.exp2(x * -LOG2E))


def _silu(x):
    return x * _sigmoid(x)


def _gelu(x):
    return 0.5 * x * (1.0 + lax.erf(x * (2.0 ** -0.5)))


def _dot(a, b):
    return jnp.dot(a, b, preferred_element_type=F32)


def _dot01_f32(m01, x):
    hi = x.astype(BF16)
    lo = (x - hi.astype(F32)).astype(BF16)
    return _dot(m01, hi) + _dot(m01, lo)


def _resident(stacked, layer):
    return pl.BlockSpec((None,) + stacked.shape[1:], lambda i: (layer, 0, 0), pipeline_mode=pl.Buffered(1))


def _cast_once(dst_ref, src_ref):
    @pl.when(pl.program_id(0) == 0)
    def _():
        dst_ref[...] = src_ref[...].astype(BF16)


def _norm_matmul_kernel(x_ref, g_ref, w_ref, o_ref, wb_ref):
    _cast_once(wb_ref, w_ref)
    h = _rms(x_ref[...], g_ref[...]).astype(BF16)
    o_ref[...] = _dot(h, wb_ref[...]).astype(o_ref.dtype)


def _norm_matmul(x2d, g, w_stacked, layer, out_dtype):
    t, d = x2d.shape
    n = w_stacked.shape[2]
    return pl.pallas_call(
        _norm_matmul_kernel,
        grid=(t // ROW_TILE,),
        in_specs=[pl.BlockSpec((ROW_TILE, d), lambda i: (i, 0)),
                  pl.BlockSpec((1, d), lambda i: (0, 0)),
                  _resident(w_stacked, layer)],
        out_specs=pl.BlockSpec((ROW_TILE, n), lambda i: (i, 0)),
        out_shape=jax.ShapeDtypeStruct((t, n), out_dtype),
        scratch_shapes=[pltpu.VMEM((d, n), BF16)],
        compiler_params=_params("arbitrary"),
        name="norm_matmul",
    )(x2d, g.reshape(1, d), w_stacked)


def _hgrn_kernel(q_ref, f_ref, i_ref, g_ref, lbz_ref, on_ref, o_ref, *, layer):
    seq, dk = q_ref.shape[1], q_ref.shape[2]
    c, grp = HGRN_CHUNK, HGRN_GROUP
    rows = c * grp
    lbz = lbz_ref[...]
    e = jnp.exp(lbz - jnp.max(lbz, axis=0, keepdims=True))
    lb = jnp.sum(e[:layer + 1], axis=0, keepdims=True) / jnp.sum(e, axis=0, keepdims=True)
    on = on_ref[...]
    row = lax.broadcasted_iota(jnp.int32, (rows, rows), 0)
    col = lax.broadcasted_iota(jnp.int32, (rows, rows), 1)
    causal = (col <= row) & (col >= (row // c) * c)
    tril = causal.astype(BF16)

    def group(n):
        sl = slice(n * rows, (n + 1) * rows)
        f = lb + (1.0 - lb) * _sigmoid(f_ref[0, sl, :])
        cum = _dot01_f32(tril, jnp.log(f))
        yield
        cum = cum * LOG2E
        k = 1.0 - f
        q_in = (_silu(q_ref[0, sl, :]) * jnp.exp2(cum)).astype(BF16)
        k_in = (k * jnp.exp2(-cum)).astype(BF16)
        att = lax.dot_general(q_in, k_in, _NT, preferred_element_type=F32)
        vb = i_ref[0, sl, :].astype(BF16)
        cum3 = cum.reshape(grp, c, dk)
        cl = cum3[:, c - 1:c, :]
        kdec = (k.reshape(grp, c, dk) * jnp.exp2(cl - cum3)).astype(BF16)
        decay = jnp.exp2(cl)
        kv = [lax.dot_general(vb[j * c:(j + 1) * c], kdec[j], _TN, preferred_element_type=F32)
              for j in range(grp)]
        yield
        att = jnp.where(causal, att, 0.0).astype(BF16)
        yield q_in, decay, kv, _dot(att, vb)

    gens = [group(n) for n in range(seq // rows)]
    for _ in range(2):
        for gen in gens:
            next(gen)
    st = jnp.zeros((dk, dk), F32)
    for n, (q_in, decay, kv, o_intra) in enumerate([next(gen) for gen in gens]):
        sl = slice(n * rows, (n + 1) * rows)
        o_inter = []
        for j in range(grp):
            o_inter.append(lax.dot_general(q_in[j * c:(j + 1) * c], st.astype(BF16), _NT,
                                           preferred_element_type=F32))
            st = decay[j] * st + kv[j]
        o = o_intra + jnp.concatenate(o_inter, axis=0)
        o = o * lax.rsqrt(jnp.mean(o * o, axis=-1, keepdims=True) + EPS)
        o_ref[0, sl, :] = (o * on * _silu(g_ref[0, sl, :])).astype(o_ref.dtype)


def _hgrn(proj, lbz, out_norm, layer, width):
    b, s, _ = proj.shape
    hd = HGRN_HEAD_DIM
    heads = width // hd
    sec = lambda k: pl.BlockSpec((1, s, hd), lambda i, h: (i, 0, k * heads + h))
    return pl.pallas_call(
        functools.partial(_hgrn_kernel, layer=layer),
        grid=(b, heads),
        in_specs=[sec(0), sec(1), sec(2), sec(3),
                  pl.BlockSpec((lbz.shape[0], hd), lambda i, h: (0, h)),
                  pl.BlockSpec((1, hd), lambda i, h: (0, h))],
        out_specs=pl.BlockSpec((1, s, hd), lambda i, h: (i, 0, h)),
        out_shape=jax.ShapeDtypeStruct((b, s, width), BF16),
        compiler_params=_params("parallel", "parallel"),
        name="hgrn2",
    )(proj, proj, proj, proj, lbz, out_norm.reshape(1, width))


def _conv_kernel(a_ref, b_ref, w_ref, db_ref, lg_ref, lb_ref, o_ref, cpad_ref, acc_ref):
    seq, ch = a_ref.shape[1], a_ref.shape[2]
    kw = w_ref.shape[0]
    pad = cpad_ref.shape[0] - seq
    tt = CONV_TILE
    cpad_ref[0:pad, :] = jnp.zeros((pad, ch), F32)

    def fill(t, carry):
        sl = pl.ds(pl.multiple_of(t * tt, tt), tt)
        cpad_ref[pl.ds(pl.multiple_of(pad + t * tt, SUBLANES), tt), :] = (
            a_ref[0, sl, :] * _sigmoid(b_ref[0, sl, :]))
        return carry

    lax.fori_loop(0, seq // tt, fill, 0)
    db, lg, lb = db_ref[...], lg_ref[...], lb_ref[...]

    offs = [pad - kw + 1 + k for k in range(kw)]
    sup = acc_ref.shape[0]
    lane_blocks = ch // LANES

    def conv(s, carry):
        sbase = pl.multiple_of(s * sup, sup)

        def block(i, c2):
            t, j = i // lane_blocks, i % lane_blocks
            base = pl.multiple_of(sbase + t * tt, tt)
            ls = pl.ds(pl.multiple_of(j * LANES, LANES), LANES)
            win = cpad_ref[pl.ds(base, tt + pad), ls]
            acc = jnp.zeros((tt, LANES), F32)
            for r in range(SUBLANES):
                taps = [k for k in range(kw) if offs[k] % SUBLANES == r]
                if not taps:
                    continue
                shifted = pltpu.roll(win, tt + pad - r, axis=0) if r else win
                for k in taps:
                    acc = acc + w_ref[k:k + 1, ls] * shifted[offs[k] - r:offs[k] - r + tt, :]
            acc_ref[pl.ds(pl.multiple_of(t * tt, tt), tt), ls] = acc
            return c2

        lax.fori_loop(0, (sup // tt) * lane_blocks, block, 0)
        acc = acc_ref[...] + db
        mu = jnp.mean(acc, axis=-1, keepdims=True)
        d = acc - mu
        y = d * lax.rsqrt(jnp.mean(d * d, axis=-1, keepdims=True) + EPS) * lg + lb
        o_ref[0, pl.ds(sbase, sup), :] = _silu(y).astype(o_ref.dtype)
        return carry

    lax.fori_loop(0, seq // sup, conv, 0)


def _conv_module(proj, col_a, col_b, dw_w, dw_b, ln_g, ln_b):
    b, s, _ = proj.shape
    kw, ch = dw_w.shape
    pad = -(-(kw - 1) // SUBLANES) * SUBLANES
    vec = lambda: pl.BlockSpec((1, ch), lambda i: (0, 0))
    return pl.pallas_call(
        _conv_kernel,
        grid=(b,),
        in_specs=[pl.BlockSpec((1, s, ch), lambda i: (i, 0, col_a)),
                  pl.BlockSpec((1, s, ch), lambda i: (i, 0, col_b)),
                  pl.BlockSpec((kw, ch), lambda i: (0, 0)), vec(), vec(), vec()],
        out_specs=pl.BlockSpec((1, s, ch), lambda i: (i, 0, 0)),
        out_shape=jax.ShapeDtypeStruct((b, s, ch), BF16),
        scratch_shapes=[pltpu.VMEM((pad + s, ch), F32), pltpu.VMEM((CONV_NORM_ROWS, ch), F32)],
        compiler_params=_params("parallel"),
        name="conv_module",
    )(proj, proj, dw_w, dw_b.reshape(1, ch), ln_g.reshape(1, ch), ln_b.reshape(1, ch))


def _moba_kernel(q_ref, k_ref, v_ref, o_ref, qaug_ref, kaug_ref, vaug_ref, s_ref, p_ref, m_ref, acc_ref):
    seq, w = q_ref.shape[1], q_ref.shape[2]
    hd, blk, topk = MOBA_HEAD_DIM, MOBA_BLOCK, MOBA_TOPK
    nblk = seq // blk
    half = blk // 2
    scale = hd ** -0.5
    k = k_ref[0]
    v = v_ref[0]
    kmean = jnp.mean(k.reshape(nblk, blk, w), axis=1)
    lane_k = lax.broadcasted_iota(jnp.int32, (seq, w), 1)
    blk_k = lax.broadcasted_iota(jnp.int32, (seq, w), 0) // blk
    for e in range(2):
        in_head = (lane_k >= e * hd) & (lane_k < (e + 1) * hd)
        onehot = (lane_k - (1 - e) * hd == blk_k).astype(F32)
        kaug_ref[e] = jnp.where(in_head, k, onehot).astype(BF16)
        vaug_ref[e] = jnp.where(in_head, v, 1.0).astype(BF16)

    lane_q = lax.broadcasted_iota(jnp.int32, (blk, w), 1)
    lane_m = lax.broadcasted_iota(jnp.int32, (nblk, w), 1)
    ridx = lax.broadcasted_iota(jnp.int32, (nblk, blk), 0)
    own_causal = (lax.broadcasted_iota(jnp.int32, (blk, blk), 1)
                  <= lax.broadcasted_iota(jnp.int32, (blk, blk), 0))

    for jq in range(nblk):
        qs = slice(jq * blk, (jq + 1) * blk)
        q2 = q_ref[0, qs, :]
        for e in range(2):
            lo = (1 - e) * hd
            if jq > topk:
                kme = jnp.where((lane_m >= e * hd) & (lane_m < (e + 1) * hd), kmean, 0.0)
                st = lax.dot_general(kme, q2, _NT, precision=lax.Precision.HIGHEST,
                                     preferred_element_type=F32)
                valid = ridx < jq
                rows = []
                for n in range(nblk):
                    if n < jq:
                        sn = st[n:n + 1, :]
                        beats = valid & ((st > sn) | ((st == sn) & (ridx < n)))
                        rank = jnp.sum(beats.astype(F32), axis=0, keepdims=True)
                        rows.append(jnp.where(rank < topk, 0.0, NEG))
                    else:
                        rows.append(jnp.zeros((1, blk), F32))
                pieces = [jnp.concatenate(rows, axis=0)]
                if lo:
                    pieces.insert(0, jnp.zeros((lo, blk), F32))
                if w - lo - nblk:
                    pieces.append(jnp.zeros((w - lo - nblk, blk), F32))
                bias_q = jnp.concatenate(pieces, axis=0).T
            else:
                bias_q = jnp.zeros((blk, w), F32)
            in_head = (lane_q >= e * hd) & (lane_q < (e + 1) * hd)
            qaug_ref[e, qs, :] = jnp.where(in_head, q2 * (scale * LOG2E), bias_q).astype(BF16)

    base = [sum(seq - i * blk for i in range(n)) for n in range(nblk)]

    def fold_max(e, rows, s, first):
        t = jnp.maximum(s[:, :half], s[:, half:])
        m_ref[e, rows, :] = t if first else jnp.maximum(m_ref[e, rows, :], t)

    def score(e, n):
        s = lax.dot_general(qaug_ref[e, n * blk:, :], kaug_ref[e, n * blk:(n + 1) * blk, :], _NT,
                            preferred_element_type=F32)
        own = jnp.where(own_causal, s[:blk], NEG)
        s_ref[e, base[n]:base[n] + blk, :] = own
        fold_max(e, slice(n * blk, (n + 1) * blk), own, n == 0)
        if n + 1 < nblk:
            s_ref[e, base[n] + blk:base[n] + seq - n * blk, :] = s[blk:]
            fold_max(e, slice((n + 1) * blk, seq), s[blk:], n == 0)

    def row_max(e):
        m_ref[e] = jnp.broadcast_to(jnp.max(m_ref[e], axis=-1, keepdims=True), (seq, half))

    def weights(e, n):
        rows = slice(base[n], base[n] + seq - n * blk)
        s = s_ref[e, rows, :]
        mb = m_ref[e, n * blk:, :]
        p_ref[e, rows, :half] = jnp.exp2(s[:, :half] - mb).astype(BF16)
        p_ref[e, rows, half:] = jnp.exp2(s[:, half:] - mb).astype(BF16)

    def values(e, n):
        rows = slice(base[n], base[n] + seq - n * blk)
        o = _dot(p_ref[e, rows, :], vaug_ref[e, n * blk:(n + 1) * blk, :])
        if n == 0:
            acc_ref[e] = o
        else:
            acc_ref[e, n * blk:, :] += o

    for n in range(nblk):
        score(0, n)
    row_max(0)
    for n in range(nblk):
        score(1, n)
        weights(0, n)
    row_max(1)
    for n in range(nblk):
        values(0, n)
        weights(1, n)
    for n in range(nblk):
        values(1, n)
    outs = []
    for e in range(2):
        lo = (1 - e) * hd
        o = acc_ref[e]
        outs.append(o * (1.0 / o[:, lo:lo + 1]))
    lane_o = lax.broadcasted_iota(jnp.int32, (seq, w), 1)
    o_ref[0] = jnp.where(lane_o < hd, outs[0], outs[1]).astype(o_ref.dtype)


def _moba(proj, width):
    b, s, _ = proj.shape
    w = 2 * MOBA_HEAD_DIM
    pairs = width // w
    nblk = s // MOBA_BLOCK
    tiles = nblk * (nblk + 1) // 2
    sec = lambda k: pl.BlockSpec((1, s, w), lambda i, h: (i, 0, k * pairs + h))
    return pl.pallas_call(
        _moba_kernel,
        grid=(b, pairs),
        in_specs=[sec(0), sec(1), sec(2)],
        out_specs=pl.BlockSpec((1, s, w), lambda i, h: (i, 0, h)),
        out_shape=jax.ShapeDtypeStruct((b, s, width), BF16),
        scratch_shapes=[pltpu.VMEM((2, s, w), BF16), pltpu.VMEM((2, s, w), BF16), pltpu.VMEM((2, s, w), BF16),
                        pltpu.VMEM((2, tiles * MOBA_BLOCK, MOBA_BLOCK), F32),
                        pltpu.VMEM((2, tiles * MOBA_BLOCK, MOBA_BLOCK), BF16),
                        pltpu.VMEM((2, s, MOBA_BLOCK // 2), F32),
                        pltpu.VMEM((2, s, w), F32)],
        compiler_params=_params("parallel", "parallel"),
        name="moba",
    )(proj, proj, proj)


def _sgu_kernel(u_ref, z_ref, lg_ref, lb_ref, w_ref, bias_ref, o_ref):
    ts, width = u_ref.shape[1], u_ref.shape[2]
    c = SGU_CHUNK
    gd = width // SGU_GROUPS
    row = lax.broadcasted_iota(jnp.int32, (c, c), 0)
    col = lax.broadcasted_iota(jnp.int32, (c, c), 1)
    for g in range(SGU_GROUPS):
        gs = slice(g * gd, (g + 1) * gd)
        z = _gelu(z_ref[0, :, gs])
        mu = jnp.mean(z, axis=-1, keepdims=True)
        d = z - mu
        zn = (d * lax.rsqrt(jnp.mean(d * d, axis=-1, keepdims=True) + EPS) * lg_ref[:, gs]
              + lb_ref[:, gs]).astype(BF16)
        wg = jnp.where(col <= row, w_ref[g], 0.0).astype(BF16)
        for n in range(ts // c):
            ts_ = slice(n * c, (n + 1) * c)
            mixed = _dot(wg, zn[ts_, :]) + bias_ref[:, gs]
            o_ref[0, ts_, gs] = (_gelu(u_ref[0, ts_, gs]) * mixed).astype(o_ref.dtype)


def _sgu(proj, col_u, col_z, ln_g, ln_b, w, bias):
    b, s, _ = proj.shape
    width = ln_g.shape[0]
    groups, c, _ = w.shape
    ts = SGU_ROWS
    bias_full = jnp.repeat(bias.T, width // groups, axis=1)
    return pl.pallas_call(
        _sgu_kernel,
        grid=(b, s // ts),
        in_specs=[pl.BlockSpec((1, ts, width), lambda i, j: (i, j, col_u)),
                  pl.BlockSpec((1, ts, width), lambda i, j: (i, j, col_z)),
                  pl.BlockSpec((1, width), lambda i, j: (0, 0)),
                  pl.BlockSpec((1, width), lambda i, j: (0, 0)),
                  pl.BlockSpec((groups, c, c), lambda i, j: (0, 0, 0)),
                  pl.BlockSpec((c, width), lambda i, j: (0, 0))],
        out_specs=pl.BlockSpec((1, ts, width), lambda i, j: (i, j, 0)),
        out_shape=jax.ShapeDtypeStruct((b, s, width), BF16),
        compiler_params=_params("parallel", "parallel"),
        name="sgu",
    )(proj, proj, ln_g.reshape(1, width), ln_b.reshape(1, width), w, bias_full)


def _cross_attention(x, g, wq_ref, kv_ref, wo_ref):
    d = x.shape[1]
    hd = d // XA_HEADS
    q = (_dot(_rms(x, g).astype(BF16), wq_ref[...]) * (hd ** -0.5 * LOG2E)).astype(BF16)

    def logits(h):
        hs = slice(h * hd, (h + 1) * hd)
        s = lax.dot_general(q[:, hs], kv_ref[0, :, hs], _NT, preferred_element_type=F32)
        return s, jnp.max(s, axis=-1, keepdims=True)

    pending = logits(0)
    outs = []
    for h in range(XA_HEADS):
        ahead = logits(h + 1) if h + 1 < XA_HEADS else None
        s, m = pending
        p = jnp.exp2(s - m)
        o = _dot(p.astype(BF16), kv_ref[0, :, d + h * hd:d + (h + 1) * hd])
        outs.append((o * (1.0 / jnp.sum(p, axis=-1, keepdims=True))).astype(BF16))
        pending = ahead
    return x + _dot(jnp.concatenate(outs, axis=-1), wo_ref[...])


def _swiglu(x, g, w1_ref, w2_ref):
    hidden = w2_ref.shape[0]
    h = _rms(x, g).astype(BF16)
    acc = x
    for c in range(hidden // FFN_CHUNK):
        cs = slice(c * FFN_CHUNK, (c + 1) * FFN_CHUNK)
        gs = slice(hidden + c * FFN_CHUNK, hidden + (c + 1) * FFN_CHUNK)
        u = (_silu(_dot(h, w1_ref[:, cs])) * _dot(h, w1_ref[:, gs])).astype(BF16)
        acc = acc + _dot(u, w2_ref[cs, :])
    return acc


def _post_mixer_kernel(x_ref, a_ref, b_ref, gx_ref, kv_ref, gf_ref, fg_ref,
                       wmix_c, wq_c, wo_c, w1_c, w2_c, o_ref,
                       wmix_b, wq_b, wo_b, w1_b, w2_b, *, final):
    i = pl.program_id(0)

    @pl.when(i < POST_CAST_STEPS)
    def _():
        for src, dst in ((wmix_c, wmix_b), (wq_c, wq_b), (wo_c, wo_b), (w1_c, w1_b), (w2_c, w2_b)):
            rows = src.shape[0]
            dst[pl.ds(pl.multiple_of(i * rows, rows), rows), :] = src[...].astype(BF16)

    @pl.when(i >= POST_CAST_STEPS)
    def _():
        ka = a_ref.shape[1]
        x = x_ref[...] + _dot(a_ref[...], wmix_b[0:ka, :]) + _dot(b_ref[...], wmix_b[ka:, :])
        x = _cross_attention(x, gx_ref[...], wq_b, kv_ref, wo_b)
        x = _swiglu(x, gf_ref[...], w1_b, w2_b)
        o_ref[...] = _rms(x, fg_ref[...]) if final else x


def _post_mixer(x2d, a2d, b2d, w_mix, mix_layer, gx, wq, kv, wo, gf, w1, w2, layer, final_gain, final, seq):
    t, d = x2d.shape
    ka, kb = a2d.shape[1], b2d.shape[1]
    m = kv.shape[1]
    nc = POST_CAST_STEPS
    per_batch = seq // ROW_TILE
    tile = lambda i: jnp.maximum(i - nc, 0)
    row = lambda n: pl.BlockSpec((ROW_TILE, n), lambda i: (tile(i), 0))
    vec = lambda: pl.BlockSpec((1, d), lambda i: (0, 0))

    def chunk(w, idx):
        return pl.BlockSpec((None, w.shape[1] // nc, w.shape[2]), lambda i: (idx, jnp.minimum(i, nc - 1), 0))

    weights = ((w_mix, mix_layer), (wq, layer), (wo, layer), (w1, layer), (w2, layer))
    return pl.pallas_call(
        functools.partial(_post_mixer_kernel, final=final),
        grid=(nc + t // ROW_TILE,),
        in_specs=[row(d), row(ka), row(kb), vec(),
                  pl.BlockSpec((1, m, 2 * d), lambda i: (tile(i) // per_batch, 0, 0)), vec(), vec()]
                 + [chunk(w, idx) for w, idx in weights],
        out_specs=row(d),
        out_shape=jax.ShapeDtypeStruct((t, d), F32),
        scratch_shapes=[pltpu.VMEM(w.shape[1:], BF16) for w, _ in weights],
        compiler_params=_params("arbitrary"),
        name="post_mixer",
    )(x2d, a2d, b2d, gx.reshape(1, d), kv, gf.reshape(1, d), final_gain.reshape(1, d),
      *[w for w, _ in weights])


def kernel(x, mem, norm_mix, norm_xattn, norm_ffn, mem_norm, final_norm, w_in_ab, w_out_ab, hgrn_lower_bounds, hgrn_out_norm, conv_dw_w, conv_dw_b, conv_ln_g, conv_ln_b, w_in_cd, w_out_cd, sgu_ln_g, sgu_ln_b, sgu_w, sgu_b, xa_wq, xa_wkv, xa_wo, ffn_w_in, ffn_w_out):
    b, s, d = x.shape
    m = mem.shape[1]
    depth = norm_mix.shape[0]
    x2d = x.reshape(b * s, d)
    mem2d = mem.reshape(b * m, d)
    for l in range(depth):
        if l % 2 == 0:
            e = l // 2
            a_width = hgrn_out_norm.shape[1]
            b_width = conv_dw_w.shape[2]
            proj = _norm_matmul(x2d, norm_mix[l], w_in_ab, e, F32).reshape(b, s, -1)
            o_a = _hgrn(proj, hgrn_lower_bounds, hgrn_out_norm[e], l, a_width)
            col = 4 * a_width // b_width
            o_b = _conv_module(proj, col, col + 1, conv_dw_w[e], conv_dw_b[e], conv_ln_g[e], conv_ln_b[e])
            w_out, mix_layer = w_out_ab, e
        else:
            o = l // 2
            d_width = sgu_ln_g.shape[1]
            c_width = w_out_cd.shape[1] - d_width
            proj = _norm_matmul(x2d, norm_mix[l], w_in_cd, o, F32).reshape(b, s, -1)
            o_a = _moba(proj, c_width)
            col = 3 * c_width // d_width
            o_b = _sgu(proj, col, col + 1, sgu_ln_g[o], sgu_ln_b[o], sgu_w[o], sgu_b[o])
            w_out, mix_layer = w_out_cd, o
        kv = _norm_matmul(mem2d, mem_norm, xa_wkv, l, BF16).reshape(b, m, 2 * d)
        x2d = _post_mixer(x2d, o_a.reshape(b * s, -1), o_b.reshape(b * s, -1), w_out, mix_layer,
                          norm_xattn[l], xa_wq, kv, xa_wo, norm_ffn[l], ffn_w_in, ffn_w_out, l,
                          final_norm, l == depth - 1, s)
    return x2d.reshape(b, s, d)
```

```python
import functools

import jax
import jax.numpy as jnp
from jax import lax
from jax.experimental import pallas as pl
from jax.experimental.pallas import tpu as pltpu

F32 = jnp.float32
BF16 = jnp.bfloat16
EPS = 1e-6
NEG = -1e30
LOG2E = 1.4426950408889634

V7X_VMEM_BYTES = 64 * 1024 * 1024
VMEM_LIMIT = V7X_VMEM_BYTES - 8 * 1024 * 1024
SUBLANES = 8
LANES = 128

HGRN_HEAD_DIM = 128
HGRN_CHUNK = 64
HGRN_GROUP = 4
CONV_TILE = 512
CONV_NORM_ROWS = 512
MOBA_HEAD_DIM = 64
MOBA_BLOCK = 256
MOBA_TOPK = 3
SGU_CHUNK = 128
SGU_GROUPS = 4
SGU_ROWS = 2048
XA_HEADS = 4
ROW_TILE = 512
FFN_CHUNK = 256
POST_CAST_STEPS = 16

_NT = (((1,), (1,)), ((), ()))
_TN = (((0,), (0,)), ((), ()))


def _params(*sem):
    return pltpu.CompilerParams(dimension_semantics=sem, vmem_limit_bytes=VMEM_LIMIT)


def _rms(x, g):
    return x * lax.rsqrt(jnp.mean(x * x, axis=-1, keepdims=True) + EPS) * g


def _sigmoid(x):
    return 1.0 / (1.0 + jnp.exp2(x * -LOG2E))


def _silu(x):
    return x * _sigmoid(x)


def _gelu(x):
    return 0.5 * x * (1.0 + lax.erf(x * (2.0 ** -0.5)))


def _dot(a, b):
    return jnp.dot(a, b, preferred_element_type=F32)


def _dot01_f32(m01, x):
    hi = x.astype(BF16)
    lo = (x - hi.astype(F32)).astype(BF16)
    return _dot(m01, hi) + _dot(m01, lo)


def _resident(stacked, layer):
    return pl.BlockSpec((None,) + stacked.shape[1:], lambda i: (layer, 0, 0), pipeline_mode=pl.Buffered(1))


def _cast_once(dst_ref, src_ref):
    @pl.when(pl.program_id(0) == 0)
    def _():
        dst_ref[...] = src_ref[...].astype(BF16)


def _norm_matmul_kernel(x_ref, g_ref, w_ref, o_ref, wb_ref):
    _cast_once(wb_ref, w_ref)
    h = _rms(x_ref[...], g_ref[...]).astype(BF16)
    o_ref[...] = _dot(h, wb_ref[...]).astype(o_ref.dtype)


def _norm_matmul(x2d, g, w_stacked, layer, out_dtype):
    t, d = x2d.shape
    n = w_stacked.shape[2]
    return pl.pallas_call(
        _norm_matmul_kernel,
        grid=(t // ROW_TILE,),
        in_specs=[pl.BlockSpec((ROW_TILE, d), lambda i: (i, 0)),
                  pl.BlockSpec((1, d), lambda i: (0, 0)),
                  _resident(w_stacked, layer)],
        out_specs=pl.BlockSpec((ROW_TILE, n), lambda i: (i, 0)),
        out_shape=jax.ShapeDtypeStruct((t, n), out_dtype),
        scratch_shapes=[pltpu.VMEM((d, n), BF16)],
        compiler_params=_params("arbitrary"),
        name="norm_matmul",
    )(x2d, g.reshape(1, d), w_stacked)


def _hgrn_kernel(q_ref, f_ref, i_ref, g_ref, lbz_ref, on_ref, o_ref, *, layer):
    seq, dk = q_ref.shape[1], q_ref.shape[2]
    c, grp = HGRN_CHUNK, HGRN_GROUP
    rows = c * grp
    lbz = lbz_ref[...]
    e = jnp.exp(lbz - jnp.max(lbz, axis=0, keepdims=True))
    lb = jnp.sum(e[:layer + 1], axis=0, keepdims=True) / jnp.sum(e, axis=0, keepdims=True)
    on = on_ref[...]
    row = lax.broadcasted_iota(jnp.int32, (rows, rows), 0)
    col = lax.broadcasted_iota(jnp.int32, (rows, rows), 1)
    causal = (col <= row) & (col >= (row // c) * c)
    tril = causal.astype(BF16)

    def group(n):
        sl = slice(n * rows, (n + 1) * rows)
        f = lb + (1.0 - lb) * _sigmoid(f_ref[0, sl, :])
        cum = _dot01_f32(tril, jnp.log(f))
        yield
        cum = cum * LOG2E
        k = 1.0 - f
        q_in = (_silu(q_ref[0, sl, :]) * jnp.exp2(cum)).astype(BF16)
        k_in = (k * jnp.exp2(-cum)).astype(BF16)
        att = lax.dot_general(q_in, k_in, _NT, preferred_element_type=F32)
        vb = i_ref[0, sl, :].astype(BF16)
        cum3 = cum.reshape(grp, c, dk)
        cl = cum3[:, c - 1:c, :]
        kdec = (k.reshape(grp, c, dk) * jnp.exp2(cl - cum3)).astype(BF16)
        decay = jnp.exp2(cl)
        kv = [lax.dot_general(vb[j * c:(j + 1) * c], kdec[j], _TN, preferred_element_type=F32)
              for j in range(grp)]
        yield
        att = jnp.where(causal, att, 0.0).astype(BF16)
        yield q_in, decay, kv, _dot(att, vb)

    gens = [group(n) for n in range(seq // rows)]
    for _ in range(2):
        for gen in gens:
            next(gen)
    st = jnp.zeros((dk, dk), F32)
    for n, (q_in, decay, kv, o_intra) in enumerate([next(gen) for gen in gens]):
        sl = slice(n * rows, (n + 1) * rows)
        o_inter = []
        for j in range(grp):
            o_inter.append(lax.dot_general(q_in[j * c:(j + 1) * c], st.astype(BF16), _NT,
                                           preferred_element_type=F32))
            st = decay[j] * st + kv[j]
        o = o_intra + jnp.concatenate(o_inter, axis=0)
        o = o * lax.rsqrt(jnp.mean(o * o, axis=-1, keepdims=True) + EPS)
        o_ref[0, sl, :] = (o * on * _silu(g_ref[0, sl, :])).astype(o_ref.dtype)


def _hgrn(proj, lbz, out_norm, layer, width):
    b, s, _ = proj.shape
    hd = HGRN_HEAD_DIM
    heads = width // hd
    sec = lambda k: pl.BlockSpec((1, s, hd), lambda i, h: (i, 0, k * heads + h))
    return pl.pallas_call(
        functools.partial(_hgrn_kernel, layer=layer),
        grid=(b, heads),
        in_specs=[sec(0), sec(1), sec(2), sec(3),
                  pl.BlockSpec((lbz.shape[0], hd), lambda i, h: (0, h)),
                  pl.BlockSpec((1, hd), lambda i, h: (0, h))],
        out_specs=pl.BlockSpec((1, s, hd), lambda i, h: (i, 0, h)),
        out_shape=jax.ShapeDtypeStruct((b, s, width), BF16),
        compiler_params=_params("parallel", "parallel"),
        name="hgrn2",
    )(proj, proj, proj, proj, lbz, out_norm.reshape(1, width))


def _conv_kernel(a_ref, b_ref, w_ref, db_ref, lg_ref, lb_ref, o_ref, cpad_ref, acc_ref):
    seq, ch = a_ref.shape[1], a_ref.shape[2]
    kw = w_ref.shape[0]
    pad = cpad_ref.shape[0] - seq
    tt = CONV_TILE
    cpad_ref[0:pad, :] = jnp.zeros((pad, ch), F32)

    def fill(t, carry):
        sl = pl.ds(pl.multiple_of(t * tt, tt), tt)
        cpad_ref[pl.ds(pl.multiple_of(pad + t * tt, SUBLANES), tt), :] = (
            a_ref[0, sl, :] * _sigmoid(b_ref[0, sl, :]))
        return carry

    lax.fori_loop(0, seq // tt, fill, 0)
    db, lg, lb = db_ref[...], lg_ref[...], lb_ref[...]

    offs = [pad - kw + 1 + k for k in range(kw)]
    sup = acc_ref.shape[0]
    lane_blocks = ch // LANES

    def conv(s, carry):
        sbase = pl.multiple_of(s * sup, sup)

        def block(i, c2):
            t, j = i // lane_blocks, i % lane_blocks
            base = pl.multiple_of(sbase + t * tt, tt)
            ls = pl.ds(pl.multiple_of(j * LANES, LANES), LANES)
            win = cpad_ref[pl.ds(base, tt + pad), ls]
            acc = jnp.zeros((tt, LANES), F32)
            for r in range(SUBLANES):
                taps = [k for k in range(kw) if offs[k] % SUBLANES == r]
                if not taps:
                    continue
                shifted = pltpu.roll(win, tt + pad - r, axis=0) if r else win
                for k in taps:
                    acc = acc + w_ref[k:k + 1, ls] * shifted[offs[k] - r:offs[k] - r + tt, :]
            acc_ref[pl.ds(pl.multiple_of(t * tt, tt), tt), ls] = acc
            return c2

        lax.fori_loop(0, (sup // tt) * lane_blocks, block, 0)
        acc = acc_ref[...] + db
        mu = jnp.mean(acc, axis=-1, keepdims=True)
        d = acc - mu
        y = d * lax.rsqrt(jnp.mean(d * d, axis=-1, keepdims=True) + EPS) * lg + lb
        o_ref[0, pl.ds(sbase, sup), :] = _silu(y).astype(o_ref.dtype)
        return carry

    lax.fori_loop(0, seq // sup, conv, 0)


def _conv_module(proj, col_a, col_b, dw_w, dw_b, ln_g, ln_b):
    b, s, _ = proj.shape
    kw, ch = dw_w.shape
    pad = -(-(kw - 1) // SUBLANES) * SUBLANES
    vec = lambda: pl.BlockSpec((1, ch), lambda i: (0, 0))
    return pl.pallas_call(
        _conv_kernel,
        grid=(b,),
        in_specs=[pl.BlockSpec((1, s, ch), lambda i: (i, 0, col_a)),
                  pl.BlockSpec((1, s, ch), lambda i: (i, 0, col_b)),
                  pl.BlockSpec((kw, ch), lambda i: (0, 0)), vec(), vec(), vec()],
        out_specs=pl.BlockSpec((1, s, ch), lambda i: (i, 0, 0)),
        out_shape=jax.ShapeDtypeStruct((b, s, ch), BF16),
        scratch_shapes=[pltpu.VMEM((pad + s, ch), F32), pltpu.VMEM((CONV_NORM_ROWS, ch), F32)],
        compiler_params=_params("parallel"),
        name="conv_module",
    )(proj, proj, dw_w, dw_b.reshape(1, ch), ln_g.reshape(1, ch), ln_b.reshape(1, ch))


def _moba_kernel(q_ref, k_ref, v_ref, o_ref, qaug_ref, kaug_ref, vaug_ref, s_ref, p_ref, m_ref, acc_ref):
    seq, w = q_ref.shape[1], q_ref.shape[2]
    hd, blk, topk = MOBA_HEAD_DIM, MOBA_BLOCK, MOBA_TOPK
    nblk = seq // blk
    half = blk // 2
    scale = hd ** -0.5
    k = k_ref[0]
    v = v_ref[0]
    kmean = jnp.mean(k.reshape(nblk, blk, w), axis=1)
    lane_k = lax.broadcasted_iota(jnp.int32, (seq, w), 1)
    blk_k = lax.broadcasted_iota(jnp.int32, (seq, w), 0) // blk
    for e in range(2):
        in_head = (lane_k >= e * hd) & (lane_k < (e + 1) * hd)
        onehot = (lane_k - (1 - e) * hd == blk_k).astype(F32)
        kaug_ref[e] = jnp.where(in_head, k, onehot).astype(BF16)
        vaug_ref[e] = jnp.where(in_head, v, 1.0).astype(BF16)

    lane_q = lax.broadcasted_iota(jnp.int32, (blk, w), 1)
    lane_m = lax.broadcasted_iota(jnp.int32, (nblk, w), 1)
    ridx = lax.broadcasted_iota(jnp.int32, (nblk, blk), 0)
    own_causal = (lax.broadcasted_iota(jnp.int32, (blk, blk), 1)
                  <= lax.broadcasted_iota(jnp.int32, (blk, blk), 0))

    for jq in range(nblk):
        qs = slice(jq * blk, (jq + 1) * blk)
        q2 = q_ref[0, qs, :]
        for e in range(2):
            lo = (1 - e) * hd
            if jq > topk:
                kme = jnp.where((lane_m >= e * hd) & (lane_m < (e + 1) * hd), kmean, 0.0)
                st = lax.dot_general(kme, q2, _NT, precision=lax.Precision.HIGHEST,
                                     preferred_element_type=F32)
                valid = ridx < jq
                rows = []
                for n in range(nblk):
                    if n < jq:
                        sn = st[n:n + 1, :]
                        beats = valid & ((st > sn) | ((st == sn) & (ridx < n)))
                        rank = jnp.sum(beats.astype(F32), axis=0, keepdims=True)
                        rows.append(jnp.where(rank < topk, 0.0, NEG))
                    else:
                        rows.append(jnp.zeros((1, blk), F32))
                pieces = [jnp.concatenate(rows, axis=0)]
                if lo:
                    pieces.insert(0, jnp.zeros((lo, blk), F32))
                if w - lo - nblk:
                    pieces.append(jnp.zeros((w - lo - nblk, blk), F32))
                bias_q = jnp.concatenate(pieces, axis=0).T
            else:
                bias_q = jnp.zeros((blk, w), F32)
            in_head = (lane_q >= e * hd) & (lane_q < (e + 1) * hd)
            qaug_ref[e, qs, :] = jnp.where(in_head, q2 * (scale * LOG2E), bias_q).astype(BF16)

    base = [sum(seq - i * blk for i in range(n)) for n in range(nblk)]

    def fold_max(e, rows, s, first):
        t = jnp.maximum(s[:, :half], s[:, half:])
        m_ref[e, rows, :] = t if first else jnp.maximum(m_ref[e, rows, :], t)

    def score(e, n):
        s = lax.dot_general(qaug_ref[e, n * blk:, :], kaug_ref[e, n * blk:(n + 1) * blk, :], _NT,
                            preferred_element_type=F32)
        own = jnp.where(own_causal, s[:blk], NEG)
        s_ref[e, base[n]:base[n] + blk, :] = own
        fold_max(e, slice(n * blk, (n + 1) * blk), own, n == 0)
        if n + 1 < nblk:
            s_ref[e, base[n] + blk:base[n] + seq - n * blk, :] = s[blk:]
            fold_max(e, slice((n + 1) * blk, seq), s[blk:], n == 0)

    def row_max(e):
        m_ref[e] = jnp.broadcast_to(jnp.max(m_ref[e], axis=-1, keepdims=True), (seq, half))

    def weights(e, n):
        rows = slice(base[n], base[n] + seq - n * blk)
        s = s_ref[e, rows, :]
        mb = m_ref[e, n * blk:, :]
        p_ref[e, rows, :half] = jnp.exp2(s[:, :half] - mb).astype(BF16)
        p_ref[e, rows, half:] = jnp.exp2(s[:, half:] - mb).astype(BF16)

    def values(e, n):
        rows = slice(base[n], base[n] + seq - n * blk)
        o = _dot(p_ref[e, rows, :], vaug_ref[e, n * blk:(n + 1) * blk, :])
        if n == 0:
            acc_ref[e] = o
        else:
            acc_ref[e, n * blk:, :] += o

    for n in range(nblk):
        score(0, n)
    row_max(0)
    for n in range(nblk):
        score(1, n)
        weights(0, n)
    row_max(1)
    for n in range(nblk):
        values(0, n)
        weights(1, n)
    for n in range(nblk):
        values(1, n)
    outs = []
    for e in range(2):
        lo = (1 - e) * hd
        o = acc_ref[e]
        outs.append(o * (1.0 / o[:, lo:lo + 1]))
    lane_o = lax.broadcasted_iota(jnp.int32, (seq, w), 1)
    o_ref[0] = jnp.where(lane_o < hd, outs[0], outs[1]).astype(o_ref.dtype)


def _moba(proj, width):
    b, s, _ = proj.shape
    w = 2 * MOBA_HEAD_DIM
    pairs = width // w
    nblk = s // MOBA_BLOCK
    tiles = nblk * (nblk + 1) // 2
    sec = lambda k: pl.BlockSpec((1, s, w), lambda i, h: (i, 0, k * pairs + h))
    return pl.pallas_call(
        _moba_kernel,
        grid=(b, pairs),
        in_specs=[sec(0), sec(1), sec(2)],
        out_specs=pl.BlockSpec((1, s, w), lambda i, h: (i, 0, h)),
        out_shape=jax.ShapeDtypeStruct((b, s, width), BF16),
        scratch_shapes=[pltpu.VMEM((2, s, w), BF16), pltpu.VMEM((2, s, w), BF16), pltpu.VMEM((2, s, w), BF16),
                        pltpu.VMEM((2, tiles * MOBA_BLOCK, MOBA_BLOCK), F32),
                        pltpu.VMEM((2, tiles * MOBA_BLOCK, MOBA_BLOCK), BF16),
                        pltpu.VMEM((2, s, MOBA_BLOCK // 2), F32),
                        pltpu.VMEM((2, s, w), F32)],
        compiler_params=_params("parallel", "parallel"),
        name="moba",
    )(proj, proj, proj)


def _sgu_kernel(u_ref, z_ref, lg_ref, lb_ref, w_ref, bias_ref, o_ref):
    ts, width = u_ref.shape[1], u_ref.shape[2]
    c = SGU_CHUNK
    gd = width // SGU_GROUPS
    row = lax.broadcasted_iota(jnp.int32, (c, c), 0)
    col = lax.broadcasted_iota(jnp.int32, (c, c), 1)
    for g in range(SGU_GROUPS):
        gs = slice(g * gd, (g + 1) * gd)
        z = _gelu(z_ref[0, :, gs])
        mu = jnp.mean(z, axis=-1, keepdims=True)
        d = z - mu
        zn = (d * lax.rsqrt(jnp.mean(d * d, axis=-1, keepdims=True) + EPS) * lg_ref[:, gs]
              + lb_ref[:, gs]).astype(BF16)
        wg = jnp.where(col <= row, w_ref[g], 0.0).astype(BF16)
        for n in range(ts // c):
            ts_ = slice(n * c, (n + 1) * c)
            mixed = _dot(wg, zn[ts_, :]) + bias_ref[:, gs]
            o_ref[0, ts_, gs] = (_gelu(u_ref[0, ts_, gs]) * mixed).astype(o_ref.dtype)


def _sgu(proj, col_u, col_z, ln_g, ln_b, w, bias):
    b, s, _ = proj.shape
    width = ln_g.shape[0]
    groups, c, _ = w.shape
    ts = SGU_ROWS
    bias_full = jnp.repeat(bias.T, width // groups, axis=1)
    return pl.pallas_call(
        _sgu_kernel,
        grid=(b, s // ts),
        in_specs=[pl.BlockSpec((1, ts, width), lambda i, j: (i, j, col_u)),
                  pl.BlockSpec((1, ts, width), lambda i, j: (i, j, col_z)),
                  pl.BlockSpec((1, width), lambda i, j: (0, 0)),
                  pl.BlockSpec((1, width), lambda i, j: (0, 0)),
                  pl.BlockSpec((groups, c, c), lambda i, j: (0, 0, 0)),
                  pl.BlockSpec((c, width), lambda i, j: (0, 0))],
        out_specs=pl.BlockSpec((1, ts, width), lambda i, j: (i, j, 0)),
        out_shape=jax.ShapeDtypeStruct((b, s, width), BF16),
        compiler_params=_params("parallel", "parallel"),
        name="sgu",
    )(proj, proj, ln_g.reshape(1, width), ln_b.reshape(1, width), w, bias_full)


def _cross_attention(x, g, wq_ref, kv_ref, wo_ref):
    d = x.shape[1]
    hd = d // XA_HEADS
    q = (_dot(_rms(x, g).astype(BF16), wq_ref[...]) * (hd ** -0.5 * LOG2E)).astype(BF16)

    def logits(h):
        hs = slice(h * hd, (h + 1) * hd)
        s = lax.dot_general(q[:, hs], kv_ref[0, :, hs], _NT, preferred_element_type=F32)
        return s, jnp.max(s, axis=-1, keepdims=True)

    pending = logits(0)
    outs = []
    for h in range(XA_HEADS):
        ahead = logits(h + 1) if h + 1 < XA_HEADS else None
        s, m = pending
        p = jnp.exp2(s - m)
        o = _dot(p.astype(BF16), kv_ref[0, :, d + h * hd:d + (h + 1) * hd])
        outs.append((o * (1.0 / jnp.sum(p, axis=-1, keepdims=True))).astype(BF16))
        pending = ahead
    return x + _dot(jnp.concatenate(outs, axis=-1), wo_ref[...])


def _swiglu(x, g, w1_ref, w2_ref):
    hidden = w2_ref.shape[0]
    h = _rms(x, g).astype(BF16)
    acc = x
    for c in range(hidden // FFN_CHUNK):
        cs = slice(c * FFN_CHUNK, (c + 1) * FFN_CHUNK)
        gs = slice(hidden + c * FFN_CHUNK, hidden + (c + 1) * FFN_CHUNK)
        u = (_silu(_dot(h, w1_ref[:, cs])) * _dot(h, w1_ref[:, gs])).astype(BF16)
        acc = acc + _dot(u, w2_ref[cs, :])
    return acc


def _post_mixer_kernel(x_ref, a_ref, b_ref, gx_ref, kv_ref, gf_ref, fg_ref,
                       wmix_c, wq_c, wo_c, w1_c, w2_c, o_ref,
                       wmix_b, wq_b, wo_b, w1_b, w2_b, *, final):
    i = pl.program_id(0)

    @pl.when(i < POST_CAST_STEPS)
    def _():
        for src, dst in ((wmix_c, wmix_b), (wq_c, wq_b), (wo_c, wo_b), (w1_c, w1_b), (w2_c, w2_b)):
            rows = src.shape[0]
            dst[pl.ds(pl.multiple_of(i * rows, rows), rows), :] = src[...].astype(BF16)

    @pl.when(i >= POST_CAST_STEPS)
    def _():
        ka = a_ref.shape[1]
        x = x_ref[...] + _dot(a_ref[...], wmix_b[0:ka, :]) + _dot(b_ref[...], wmix_b[ka:, :])
        x = _cross_attention(x, gx_ref[...], wq_b, kv_ref, wo_b)
        x = _swiglu(x, gf_ref[...], w1_b, w2_b)
        o_ref[...] = _rms(x, fg_ref[...]) if final else x


def _post_mixer(x2d, a2d, b2d, w_mix, mix_layer, gx, wq, kv, wo, gf, w1, w2, layer, final_gain, final, seq):
    t, d = x2d.shape
    ka, kb = a2d.shape[1], b2d.shape[1]
    m = kv.shape[1]
    nc = POST_CAST_STEPS
    per_batch = seq // ROW_TILE
    tile = lambda i: jnp.maximum(i - nc, 0)
    row = lambda n: pl.BlockSpec((ROW_TILE, n), lambda i: (tile(i), 0))
    vec = lambda: pl.BlockSpec((1, d), lambda i: (0, 0))

    def chunk(w, idx):
        return pl.BlockSpec((None, w.shape[1] // nc, w.shape[2]), lambda i: (idx, jnp.minimum(i, nc - 1), 0))

    weights = ((w_mix, mix_layer), (wq, layer), (wo, layer), (w1, layer), (w2, layer))
    return pl.pallas_call(
        functools.partial(_post_mixer_kernel, final=final),
        grid=(nc + t // ROW_TILE,),
        in_specs=[row(d), row(ka), row(kb), vec(),
                  pl.BlockSpec((1, m, 2 * d), lambda i: (tile(i) // per_batch, 0, 0)), vec(), vec()]
                 + [chunk(w, idx) for w, idx in weights],
        out_specs=row(d),
        out_shape=jax.ShapeDtypeStruct((t, d), F32),
        scratch_shapes=[pltpu.VMEM(w.shape[1:], BF16) for w, _ in weights],
        compiler_params=_params("arbitrary"),
        name="post_mixer",
    )(x2d, a2d, b2d, gx.reshape(1, d), kv, gf.reshape(1, d), final_gain.reshape(1, d),
      *[w for w, _ in weights])


def kernel(x, mem, norm_mix, norm_xattn, norm_ffn, mem_norm, final_norm, w_in_ab, w_out_ab, hgrn_lower_bounds, hgrn_out_norm, conv_dw_w, conv_dw_b, conv_ln_g, conv_ln_b, w_in_cd, w_out_cd, sgu_ln_g, sgu_ln_b, sgu_w, sgu_b, xa_wq, xa_wkv, xa_wo, ffn_w_in, ffn_w_out):
    b, s, d = x.shape
    m = mem.shape[1]
    depth = norm_mix.shape[0]
    x2d = x.reshape(b * s, d)
    mem2d = mem.reshape(b * m, d)
    for l in range(depth):
        if l % 2 == 0:
            e = l // 2
            a_width = hgrn_out_norm.shape[1]
            b_width = conv_dw_w.shape[2]
            proj = _norm_matmul(x2d, norm_mix[l], w_in_ab, e, F32).reshape(b, s, -1)
            o_a = _hgrn(proj, hgrn_lower_bounds, hgrn_out_norm[e], l, a_width)
            col = 4 * a_width // b_width
            o_b = _conv_module(proj, col, col + 1, conv_dw_w[e], conv_dw_b[e], conv_ln_g[e], conv_ln_b[e])
            w_out, mix_layer = w_out_ab, e
        else:
            o = l // 2
            d_width = sgu_ln_g.shape[1]
            c_width = w_out_cd.shape[1] - d_width
            proj = _norm_matmul(x2d, norm_mix[l], w_in_cd, o, F32).reshape(b, s, -1)
            o_a = _moba(proj, c_width)
            col = 3 * c_width // d_width
            o_b = _sgu(proj, col, col + 1, sgu_ln_g[o], sgu_ln_b[o], sgu_w[o], sgu_b[o])
            w_out, mix_layer = w_out_cd, o
        kv = _norm_matmul(mem2d, mem_norm, xa_wkv, l, BF16).reshape(b, m, 2 * d)
        x2d = _post_mixer(x2d, o_a.reshape(b * s, -1), o_b.reshape(b * s, -1), w_out, mix_layer,
                          norm_xattn[l], xa_wq, kv, xa_wo, norm_ffn[l], ffn_w_in, ffn_w_out, l,
                          final_norm, l == depth - 1, s)
    return x2d.reshape(b, s, d)
```

```python
import functools

import jax
import jax.numpy as jnp
from jax import lax
from jax.experimental import pallas as pl
from jax.experimental.pallas import tpu as pltpu

F32 = jnp.float32
BF16 = jnp.bfloat16
EPS = 1e-6
NEG = -1e30
LOG2E = 1.4426950408889634

V7X_VMEM_BYTES = 64 * 1024 * 1024
VMEM_LIMIT = V7X_VMEM_BYTES - 8 * 1024 * 1024
SUBLANES = 8
LANES = 128

HGRN_HEAD_DIM = 128
HGRN_CHUNK = 64
HGRN_GROUP = 4
CONV_TILE = 512
CONV_NORM_ROWS = 512
MOBA_HEAD_DIM = 64
MOBA_BLOCK = 256
MOBA_TOPK = 3
SGU_CHUNK = 128
SGU_GROUPS = 4
SGU_ROWS = 2048
XA_HEADS = 4
ROW_TILE = 512
FFN_CHUNK = 256
POST_CAST_STEPS = 16

_NT = (((1,), (1,)), ((), ()))
_TN = (((0,), (0,)), ((), ()))


def _params(*sem):
    return pltpu.CompilerParams(dimension_semantics=sem, vmem_limit_bytes=VMEM_LIMIT)


def _rms(x, g):
    return x * lax.rsqrt(jnp.mean(x * x, axis=-1, keepdims=True) + EPS) * g


def _sigmoid(x):
    return 1.0 / (1.0 + jnp.exp2(x * -LOG2E))


def _silu(x):
    return x * _sigmoid(x)


def _gelu(x):
    return 0.5 * x * (1.0 + lax.erf(x * (2.0 ** -0.5)))


def _dot(a, b):
    return jnp.dot(a, b, preferred_element_type=F32)


def _dot01_f32(m01, x):
    hi = x.astype(BF16)
    lo = (x - hi.astype(F32)).astype(BF16)
    return _dot(m01, hi) + _dot(m01, lo)


def _resident(stacked, layer):
    return pl.BlockSpec((None,) + stacked.shape[1:], lambda i: (layer, 0, 0), pipeline_mode=pl.Buffered(1))


def _cast_once(dst_ref, src_ref):
    @pl.when(pl.program_id(0) == 0)
    def _():
        dst_ref[...] = src_ref[...].astype(BF16)


def _norm_matmul_kernel(x_ref, g_ref, w_ref, *refs):
    *o_refs, wb_ref = refs
    _cast_once(wb_ref, w_ref)
    h = _rms(x_ref[...], g_ref[...]).astype(BF16)
    p = _dot(h, wb_ref[...])
    col = 0
    for o_ref in o_refs:
        o_ref[...] = p[:, col:col + o_ref.shape[1]].astype(o_ref.dtype)
        col += o_ref.shape[1]


def _norm_matmul(x2d, g, w_stacked, layer, outs):
    t, d = x2d.shape
    n = w_stacked.shape[2]
    assert sum(width for width, _ in outs) == n
    return pl.pallas_call(
        _norm_matmul_kernel,
        grid=(t // ROW_TILE,),
        in_specs=[pl.BlockSpec((ROW_TILE, d), lambda i: (i, 0)),
                  pl.BlockSpec((1, d), lambda i: (0, 0)),
                  _resident(w_stacked, layer)],
        out_specs=[pl.BlockSpec((ROW_TILE, width), lambda i: (i, 0)) for width, _ in outs],
        out_shape=[jax.ShapeDtypeStruct((t, width), dtype) for width, dtype in outs],
        scratch_shapes=[pltpu.VMEM((d, n), BF16)],
        compiler_params=_params("arbitrary"),
        name="norm_matmul",
    )(x2d, g.reshape(1, d), w_stacked)


def _hgrn_kernel(q_ref, f_ref, i_ref, g_ref, lbz_ref, on_ref, o_ref, *, layer):
    seq, dk = q_ref.shape[1], q_ref.shape[2]
    c, grp = HGRN_CHUNK, HGRN_GROUP
    rows = c * grp
    lbz = lbz_ref[...]
    e = jnp.exp(lbz - jnp.max(lbz, axis=0, keepdims=True))
    lb = jnp.sum(e[:layer + 1], axis=0, keepdims=True) / jnp.sum(e, axis=0, keepdims=True)
    on = on_ref[...]
    row = lax.broadcasted_iota(jnp.int32, (rows, rows), 0)
    col = lax.broadcasted_iota(jnp.int32, (rows, rows), 1)
    causal = (col <= row) & (col >= (row // c) * c)
    tril = causal.astype(BF16)

    def group(n):
        sl = slice(n * rows, (n + 1) * rows)
        f = lb + (1.0 - lb) * _sigmoid(f_ref[0, sl, :])
        cum = _dot01_f32(tril, jnp.log(f))
        yield
        cum = cum * LOG2E
        k = 1.0 - f
        q_in = (_silu(q_ref[0, sl, :]) * jnp.exp2(cum)).astype(BF16)
        k_in = (k * jnp.exp2(-cum)).astype(BF16)
        att = lax.dot_general(q_in, k_in, _NT, preferred_element_type=F32)
        vb = i_ref[0, sl, :].astype(BF16)
        cum3 = cum.reshape(grp, c, dk)
        cl = cum3[:, c - 1:c, :]
        kdec = (k.reshape(grp, c, dk) * jnp.exp2(cl - cum3)).astype(BF16)
        decay = jnp.exp2(cl)
        kv = [lax.dot_general(vb[j * c:(j + 1) * c], kdec[j], _TN, preferred_element_type=F32)
              for j in range(grp)]
        yield
        att = jnp.where(causal, att, 0.0).astype(BF16)
        yield q_in, decay, kv, _dot(att, vb)

    gens = [group(n) for n in range(seq // rows)]
    for _ in range(2):
        for gen in gens:
            next(gen)
    st = jnp.zeros((dk, dk), F32)
    for n, (q_in, decay, kv, o_intra) in enumerate([next(gen) for gen in gens]):
        sl = slice(n * rows, (n + 1) * rows)
        o_inter = []
        for j in range(grp):
            o_inter.append(lax.dot_general(q_in[j * c:(j + 1) * c], st.astype(BF16), _NT,
                                           preferred_element_type=F32))
            st = decay[j] * st + kv[j]
        o = o_intra + jnp.concatenate(o_inter, axis=0)
        o = o * lax.rsqrt(jnp.mean(o * o, axis=-1, keepdims=True) + EPS)
        o_ref[0, sl, :] = (o * on * _silu(g_ref[0, sl, :])).astype(o_ref.dtype)


def _hgrn(proj, lbz, out_norm, layer, width):
    b, s, _ = proj.shape
    hd = HGRN_HEAD_DIM
    heads = width // hd
    sec = lambda k: pl.BlockSpec((1, s, hd), lambda i, h: (i, 0, k * heads + h))
    return pl.pallas_call(
        functools.partial(_hgrn_kernel, layer=layer),
        grid=(b, heads),
        in_specs=[sec(0), sec(1), sec(2), sec(3),
                  pl.BlockSpec((lbz.shape[0], hd), lambda i, h: (0, h)),
                  pl.BlockSpec((1, hd), lambda i, h: (0, h))],
        out_specs=pl.BlockSpec((1, s, hd), lambda i, h: (i, 0, h)),
        out_shape=jax.ShapeDtypeStruct((b, s, width), BF16),
        compiler_params=_params("parallel", "parallel"),
        name="hgrn2",
    )(proj, proj, proj, proj, lbz, out_norm.reshape(1, width))


def _conv_kernel(a_ref, b_ref, w_ref, db_ref, lg_ref, lb_ref, o_ref, cpad_ref, acc_ref):
    seq, ch = a_ref.shape[1], a_ref.shape[2]
    kw = w_ref.shape[0]
    pad = cpad_ref.shape[0] - seq
    tt = CONV_TILE
    cpad_ref[0:pad, :] = jnp.zeros((pad, ch), F32)

    def fill(t, carry):
        sl = pl.ds(pl.multiple_of(t * tt, tt), tt)
        cpad_ref[pl.ds(pl.multiple_of(pad + t * tt, SUBLANES), tt), :] = (
            a_ref[0, sl, :] * _sigmoid(b_ref[0, sl, :]))
        return carry

    lax.fori_loop(0, seq // tt, fill, 0)
    db, lg, lb = db_ref[...], lg_ref[...], lb_ref[...]

    offs = [pad - kw + 1 + k for k in range(kw)]
    sup = acc_ref.shape[0]
    lane_blocks = ch // LANES

    def conv(s, carry):
        sbase = pl.multiple_of(s * sup, sup)

        def block(i, c2):
            t, j = i // lane_blocks, i % lane_blocks
            base = pl.multiple_of(sbase + t * tt, tt)
            ls = pl.ds(pl.multiple_of(j * LANES, LANES), LANES)
            win = cpad_ref[pl.ds(base, tt + pad), ls]
            acc = jnp.zeros((tt, LANES), F32)
            for r in range(SUBLANES):
                taps = [k for k in range(kw) if offs[k] % SUBLANES == r]
                if not taps:
                    continue
                shifted = pltpu.roll(win, tt + pad - r, axis=0) if r else win
                for k in taps:
                    acc = acc + w_ref[k:k + 1, ls] * shifted[offs[k] - r:offs[k] - r + tt, :]
            acc_ref[pl.ds(pl.multiple_of(t * tt, tt), tt), ls] = acc
            return c2

        lax.fori_loop(0, (sup // tt) * lane_blocks, block, 0)
        acc = acc_ref[...] + db
        mu = jnp.mean(acc, axis=-1, keepdims=True)
        d = acc - mu
        y = d * lax.rsqrt(jnp.mean(d * d, axis=-1, keepdims=True) + EPS) * lg + lb
        o_ref[0, pl.ds(sbase, sup), :] = _silu(y).astype(o_ref.dtype)
        return carry

    lax.fori_loop(0, seq // sup, conv, 0)


def _conv_module(proj, col_a, col_b, dw_w, dw_b, ln_g, ln_b):
    b, s, _ = proj.shape
    kw, ch = dw_w.shape
    pad = -(-(kw - 1) // SUBLANES) * SUBLANES
    vec = lambda: pl.BlockSpec((1, ch), lambda i: (0, 0))
    return pl.pallas_call(
        _conv_kernel,
        grid=(b,),
        in_specs=[pl.BlockSpec((1, s, ch), lambda i: (i, 0, col_a)),
                  pl.BlockSpec((1, s, ch), lambda i: (i, 0, col_b)),
                  pl.BlockSpec((kw, ch), lambda i: (0, 0)), vec(), vec(), vec()],
        out_specs=pl.BlockSpec((1, s, ch), lambda i: (i, 0, 0)),
        out_shape=jax.ShapeDtypeStruct((b, s, ch), BF16),
        scratch_shapes=[pltpu.VMEM((pad + s, ch), F32), pltpu.VMEM((CONV_NORM_ROWS, ch), F32)],
        compiler_params=_params("parallel"),
        name="conv_module",
    )(proj, proj, dw_w, dw_b.reshape(1, ch), ln_g.reshape(1, ch), ln_b.reshape(1, ch))


def _moba_kernel(q_ref, k_ref, v_ref, o_ref, qaug_ref, kaug_ref, vaug_ref, s_ref, p_ref, m_ref, acc_ref):
    seq, w = q_ref.shape[1], q_ref.shape[2]
    hd, blk, topk = MOBA_HEAD_DIM, MOBA_BLOCK, MOBA_TOPK
    nblk = seq // blk
    half = blk // 2
    scale = hd ** -0.5
    k = k_ref[0]
    v = v_ref[0]
    kmean = jnp.mean(k.reshape(nblk, blk, w), axis=1)
    lane_k = lax.broadcasted_iota(jnp.int32, (seq, w), 1)
    blk_k = lax.broadcasted_iota(jnp.int32, (seq, w), 0) // blk
    for e in range(2):
        in_head = (lane_k >= e * hd) & (lane_k < (e + 1) * hd)
        onehot = (lane_k - (1 - e) * hd == blk_k).astype(F32)
        kaug_ref[e] = jnp.where(in_head, k, onehot).astype(BF16)
        vaug_ref[e] = jnp.where(in_head, v, 1.0).astype(BF16)

    lane_q = lax.broadcasted_iota(jnp.int32, (blk, w), 1)
    lane_m = lax.broadcasted_iota(jnp.int32, (nblk, w), 1)
    ridx = lax.broadcasted_iota(jnp.int32, (nblk, blk), 0)
    own_causal = (lax.broadcasted_iota(jnp.int32, (blk, blk), 1)
                  <= lax.broadcasted_iota(jnp.int32, (blk, blk), 0))

    for jq in range(nblk):
        qs = slice(jq * blk, (jq + 1) * blk)
        q2 = q_ref[0, qs, :]
        for e in range(2):
            lo = (1 - e) * hd
            if jq > topk:
                kme = jnp.where((lane_m >= e * hd) & (lane_m < (e + 1) * hd), kmean, 0.0)
                st = lax.dot_general(kme, q2, _NT, precision=lax.Precision.HIGHEST,
                                     preferred_element_type=F32)
                valid = ridx < jq
                rows = []
                for n in range(nblk):
                    if n < jq:
                        sn = st[n:n + 1, :]
                        beats = valid & ((st > sn) | ((st == sn) & (ridx < n)))
                        rank = jnp.sum(beats.astype(F32), axis=0, keepdims=True)
                        rows.append(jnp.where(rank < topk, 0.0, NEG))
                    else:
                        rows.append(jnp.zeros((1, blk), F32))
                pieces = [jnp.concatenate(rows, axis=0)]
                if lo:
                    pieces.insert(0, jnp.zeros((lo, blk), F32))
                if w - lo - nblk:
                    pieces.append(jnp.zeros((w - lo - nblk, blk), F32))
                bias_q = jnp.concatenate(pieces, axis=0).T
            else:
                bias_q = jnp.zeros((blk, w), F32)
            in_head = (lane_q >= e * hd) & (lane_q < (e + 1) * hd)
            qaug_ref[e, qs, :] = jnp.where(in_head, q2 * (scale * LOG2E), bias_q).astype(BF16)

    base = [sum(seq - i * blk for i in range(n)) for n in range(nblk)]

    def fold_max(e, rows, s, first):
        t = jnp.maximum(s[:, :half], s[:, half:])
        m_ref[e, rows, :] = t if first else jnp.maximum(m_ref[e, rows, :], t)

    def score(e, n):
        s = lax.dot_general(qaug_ref[e, n * blk:, :], kaug_ref[e, n * blk:(n + 1) * blk, :], _NT,
                            preferred_element_type=F32)
        own = jnp.where(own_causal, s[:blk], NEG)
        s_ref[e, base[n]:base[n] + blk, :] = own
        fold_max(e, slice(n * blk, (n + 1) * blk), own, n == 0)
        if n + 1 < nblk:
            s_ref[e, base[n] + blk:base[n] + seq - n * blk, :] = s[blk:]
            fold_max(e, slice((n + 1) * blk, seq), s[blk:], n == 0)

    def row_max(e):
        m_ref[e] = jnp.broadcast_to(jnp.max(m_ref[e], axis=-1, keepdims=True), (seq, half))

    def weights(e, n):
        rows = slice(base[n], base[n] + seq - n * blk)
        s = s_ref[e, rows, :]
        mb = m_ref[e, n * blk:, :]
        p_ref[e, rows, :half] = jnp.exp2(s[:, :half] - mb).astype(BF16)
        p_ref[e, rows, half:] = jnp.exp2(s[:, half:] - mb).astype(BF16)

    def values(e, n):
        rows = slice(base[n], base[n] + seq - n * blk)
        o = _dot(p_ref[e, rows, :], vaug_ref[e, n * blk:(n + 1) * blk, :])
        if n == 0:
            acc_ref[e] = o
        else:
            acc_ref[e, n * blk:, :] += o

    for n in range(nblk):
        score(0, n)
    row_max(0)
    for n in range(nblk):
        score(1, n)
        weights(0, n)
    row_max(1)
    for n in range(nblk):
        values(0, n)
        weights(1, n)
    for n in range(nblk):
        values(1, n)
    outs = []
    for e in range(2):
        lo = (1 - e) * hd
        o = acc_ref[e]
        outs.append(o * (1.0 / o[:, lo:lo + 1]))
    lane_o = lax.broadcasted_iota(jnp.int32, (seq, w), 1)
    o_ref[0] = jnp.where(lane_o < hd, outs[0], outs[1]).astype(o_ref.dtype)


def _moba(proj, width):
    b, s, _ = proj.shape
    w = 2 * MOBA_HEAD_DIM
    pairs = width // w
    nblk = s // MOBA_BLOCK
    tiles = nblk * (nblk + 1) // 2
    sec = lambda k: pl.BlockSpec((1, s, w), lambda i, h: (i, 0, k * pairs + h))
    return pl.pallas_call(
        _moba_kernel,
        grid=(b, pairs),
        in_specs=[sec(0), sec(1), sec(2)],
        out_specs=pl.BlockSpec((1, s, w), lambda i, h: (i, 0, h)),
        out_shape=jax.ShapeDtypeStruct((b, s, width), BF16),
        scratch_shapes=[pltpu.VMEM((2, s, w), BF16), pltpu.VMEM((2, s, w), BF16), pltpu.VMEM((2, s, w), BF16),
                        pltpu.VMEM((2, tiles * MOBA_BLOCK, MOBA_BLOCK), F32),
                        pltpu.VMEM((2, tiles * MOBA_BLOCK, MOBA_BLOCK), BF16),
                        pltpu.VMEM((2, s, MOBA_BLOCK // 2), F32),
                        pltpu.VMEM((2, s, w), F32)],
        compiler_params=_params("parallel", "parallel"),
        name="moba",
    )(proj, proj, proj)


def _sgu_kernel(u_ref, z_ref, lg_ref, lb_ref, w_ref, bias_ref, o_ref):
    ts, width = u_ref.shape[1], u_ref.shape[2]
    c = SGU_CHUNK
    gd = width // SGU_GROUPS
    row = lax.broadcasted_iota(jnp.int32, (c, c), 0)
    col = lax.broadcasted_iota(jnp.int32, (c, c), 1)
    for g in range(SGU_GROUPS):
        gs = slice(g * gd, (g + 1) * gd)
        z = _gelu(z_ref[0, :, gs].astype(F32))
        mu = jnp.mean(z, axis=-1, keepdims=True)
        d = z - mu
        zn = (d * lax.rsqrt(jnp.mean(d * d, axis=-1, keepdims=True) + EPS) * lg_ref[:, gs]
              + lb_ref[:, gs]).astype(BF16)
        wg = jnp.where(col <= row, w_ref[g], 0.0).astype(BF16)
        for n in range(ts // c):
            ts_ = slice(n * c, (n + 1) * c)
            mixed = _dot(wg, zn[ts_, :]) + bias_ref[:, gs]
            o_ref[0, ts_, gs] = (_gelu(u_ref[0, ts_, gs].astype(F32)) * mixed).astype(o_ref.dtype)


def _sgu(proj, col_u, col_z, ln_g, ln_b, w, bias):
    b, s, _ = proj.shape
    width = ln_g.shape[0]
    groups, c, _ = w.shape
    ts = SGU_ROWS
    bias_full = jnp.repeat(bias.T, width // groups, axis=1)
    return pl.pallas_call(
        _sgu_kernel,
        grid=(b, s // ts),
        in_specs=[pl.BlockSpec((1, ts, width), lambda i, j: (i, j, col_u)),
                  pl.BlockSpec((1, ts, width), lambda i, j: (i, j, col_z)),
                  pl.BlockSpec((1, width), lambda i, j: (0, 0)),
                  pl.BlockSpec((1, width), lambda i, j: (0, 0)),
                  pl.BlockSpec((groups, c, c), lambda i, j: (0, 0, 0)),
                  pl.BlockSpec((c, width), lambda i, j: (0, 0))],
        out_specs=pl.BlockSpec((1, ts, width), lambda i, j: (i, j, 0)),
        out_shape=jax.ShapeDtypeStruct((b, s, width), BF16),
        compiler_params=_params("parallel", "parallel"),
        name="sgu",
    )(proj, proj, ln_g.reshape(1, width), ln_b.reshape(1, width), w, bias_full)


def _cross_attention(x, g, wq_ref, kv_ref, wo_ref):
    d = x.shape[1]
    hd = d // XA_HEADS
    q = (_dot(_rms(x, g).astype(BF16), wq_ref[...]) * (hd ** -0.5 * LOG2E)).astype(BF16)

    def logits(h):
        hs = slice(h * hd, (h + 1) * hd)
        s = lax.dot_general(q[:, hs], kv_ref[0, :, hs], _NT, preferred_element_type=F32)
        return s, jnp.max(s, axis=-1, keepdims=True)

    pending = logits(0)
    outs = []
    for h in range(XA_HEADS):
        ahead = logits(h + 1) if h + 1 < XA_HEADS else None
        s, m = pending
        p = jnp.exp2(s - m)
        o = _dot(p.astype(BF16), kv_ref[0, :, d + h * hd:d + (h + 1) * hd])
        outs.append((o * (1.0 / jnp.sum(p, axis=-1, keepdims=True))).astype(BF16))
        pending = ahead
    return x + _dot(jnp.concatenate(outs, axis=-1), wo_ref[...])


def _swiglu(x, g, w1_ref, w2_ref):
    hidden = w2_ref.shape[0]
    h = _rms(x, g).astype(BF16)
    acc = x
    for c in range(hidden // FFN_CHUNK):
        cs = slice(c * FFN_CHUNK, (c + 1) * FFN_CHUNK)
        gs = slice(hidden + c * FFN_CHUNK, hidden + (c + 1) * FFN_CHUNK)
        u = (_silu(_dot(h, w1_ref[:, cs])) * _dot(h, w1_ref[:, gs])).astype(BF16)
        acc = acc + _dot(u, w2_ref[cs, :])
    return acc


def _post_mixer_kernel(x_ref, a_ref, b_ref, gx_ref, kv_ref, gf_ref, fg_ref,
                       wmix_c, wq_c, wo_c, w1_c, w2_c, o_ref,
                       wmix_b, wq_b, wo_b, w1_b, w2_b, *, final):
    i = pl.program_id(0)

    @pl.when(i < POST_CAST_STEPS)
    def _():
        for src, dst in ((wmix_c, wmix_b), (wq_c, wq_b), (wo_c, wo_b), (w1_c, w1_b), (w2_c, w2_b)):
            rows = src.shape[0]
            dst[pl.ds(pl.multiple_of(i * rows, rows), rows), :] = src[...].astype(BF16)

    @pl.when(i >= POST_CAST_STEPS)
    def _():
        ka = a_ref.shape[1]
        x = x_ref[...] + _dot(a_ref[...], wmix_b[0:ka, :]) + _dot(b_ref[...], wmix_b[ka:, :])
        x = _cross_attention(x, gx_ref[...], wq_b, kv_ref, wo_b)
        x = _swiglu(x, gf_ref[...], w1_b, w2_b)
        o_ref[...] = _rms(x, fg_ref[...]) if final else x


def _post_mixer(x2d, a2d, b2d, w_mix, mix_layer, gx, wq, kv, wo, gf, w1, w2, layer, final_gain, final, seq):
    t, d = x2d.shape
    ka, kb = a2d.shape[1], b2d.shape[1]
    m = kv.shape[1]
    nc = POST_CAST_STEPS
    per_batch = seq // ROW_TILE
    tile = lambda i: jnp.maximum(i - nc, 0)
    row = lambda n: pl.BlockSpec((ROW_TILE, n), lambda i: (tile(i), 0))
    vec = lambda: pl.BlockSpec((1, d), lambda i: (0, 0))

    def chunk(w, idx):
        return pl.BlockSpec((None, w.shape[1] // nc, w.shape[2]), lambda i: (idx, jnp.minimum(i, nc - 1), 0))

    weights = ((w_mix, mix_layer), (wq, layer), (wo, layer), (w1, layer), (w2, layer))
    return pl.pallas_call(
        functools.partial(_post_mixer_kernel, final=final),
        grid=(nc + t // ROW_TILE,),
        in_specs=[row(d), row(ka), row(kb), vec(),
                  pl.BlockSpec((1, m, 2 * d), lambda i: (tile(i) // per_batch, 0, 0)), vec(), vec()]
                 + [chunk(w, idx) for w, idx in weights],
        out_specs=row(d),
        out_shape=jax.ShapeDtypeStruct((t, d), F32),
        scratch_shapes=[pltpu.VMEM(w.shape[1:], BF16) for w, _ in weights],
        compiler_params=_params("arbitrary"),
        name="post_mixer",
    )(x2d, a2d, b2d, gx.reshape(1, d), kv, gf.reshape(1, d), final_gain.reshape(1, d),
      *[w for w, _ in weights])


def kernel(x, mem, norm_mix, norm_xattn, norm_ffn, mem_norm, final_norm, w_in_ab, w_out_ab, hgrn_lower_bounds, hgrn_out_norm, conv_dw_w, conv_dw_b, conv_ln_g, conv_ln_b, w_in_cd, w_out_cd, sgu_ln_g, sgu_ln_b, sgu_w, sgu_b, xa_wq, xa_wkv, xa_wo, ffn_w_in, ffn_w_out):
    b, s, d = x.shape
    m = mem.shape[1]
    depth = norm_mix.shape[0]
    x2d = x.reshape(b * s, d)
    mem2d = mem.reshape(b * m, d)
    for l in range(depth):
        if l % 2 == 0:
            e = l // 2
            a_width = hgrn_out_norm.shape[1]
            b_width = conv_dw_w.shape[2]
            (proj,) = _norm_matmul(x2d, norm_mix[l], w_in_ab, e, [(w_in_ab.shape[2], F32)])
            proj = proj.reshape(b, s, -1)
            o_a = _hgrn(proj, hgrn_lower_bounds, hgrn_out_norm[e], l, a_width)
            col = 4 * a_width // b_width
            o_b = _conv_module(proj, col, col + 1, conv_dw_w[e], conv_dw_b[e], conv_ln_g[e], conv_ln_b[e])
            w_out, mix_layer = w_out_ab, e
        else:
            o = l // 2
            d_width = sgu_ln_g.shape[1]
            c_width = w_out_cd.shape[1] - d_width
            qkv, uz = _norm_matmul(x2d, norm_mix[l], w_in_cd, o, [(3 * c_width, F32), (2 * d_width, BF16)])
            o_a = _moba(qkv.reshape(b, s, -1), c_width)
            o_b = _sgu(uz.reshape(b, s, -1), 0, 1, sgu_ln_g[o], sgu_ln_b[o], sgu_w[o], sgu_b[o])
            w_out, mix_layer = w_out_cd, o
        (kv,) = _norm_matmul(mem2d, mem_norm, xa_wkv, l, [(2 * d, BF16)])
        kv = kv.reshape(b, m, 2 * d)
        x2d = _post_mixer(x2d, o_a.reshape(b * s, -1), o_b.reshape(b * s, -1), w_out, mix_layer,
                          norm_xattn[l], xa_wq, kv, xa_wo, norm_ffn[l], ffn_w_in, ffn_w_out, l,
                          final_norm, l == depth - 1, s)
    return x2d.reshape(b, s, d)
```

```python
import functools

import jax
import jax.numpy as jnp
from jax import lax
from jax.experimental import pallas as pl
from jax.experimental.pallas import tpu as pltpu

F32 = jnp.float32
BF16 = jnp.bfloat16
EPS = 1e-6
NEG = -1e30
LOG2E = 1.4426950408889634

V7X_VMEM_BYTES = 64 * 1024 * 1024
VMEM_LIMIT = V7X_VMEM_BYTES - 8 * 1024 * 1024
SUBLANES = 8
LANES = 128

HGRN_HEAD_DIM = 128
HGRN_CHUNK = 64
HGRN_GROUP = 4
CONV_TILE = 512
CONV_NORM_ROWS = 512
MOBA_HEAD_DIM = 64
MOBA_BLOCK = 256
MOBA_TOPK = 3
SGU_CHUNK = 128
SGU_GROUPS = 4
SGU_ROWS = 2048
XA_HEADS = 4
ROW_TILE = 512
FFN_CHUNK = 256
POST_CAST_STEPS = 16

_NT = (((1,), (1,)), ((), ()))
_TN = (((0,), (0,)), ((), ()))


def _params(*sem):
    return pltpu.CompilerParams(dimension_semantics=sem, vmem_limit_bytes=VMEM_LIMIT)


def _rms(x, g):
    return x * lax.rsqrt(jnp.mean(x * x, axis=-1, keepdims=True) + EPS) * g


def _sigmoid(x):
    return 1.0 / (1.0 + jnp.exp2(x * -LOG2E))


def _silu(x):
    return x * _sigmoid(x)


def _gelu(x):
    return 0.5 * x * (1.0 + lax.erf(x * (2.0 ** -0.5)))


def _dot(a, b):
    return jnp.dot(a, b, preferred_element_type=F32)


def _dot01_f32(m01, x):
    hi = x.astype(BF16)
    lo = (x - hi.astype(F32)).astype(BF16)
    return _dot(m01, hi) + _dot(m01, lo)


def _resident(stacked, layer):
    return pl.BlockSpec((None,) + stacked.shape[1:], lambda i: (layer, 0, 0), pipeline_mode=pl.Buffered(1))


def _cast_once(dst_ref, src_ref):
    @pl.when(pl.program_id(0) == 0)
    def _():
        dst_ref[...] = src_ref[...].astype(BF16)


def _norm_matmul_kernel(x_ref, g_ref, w_ref, o_ref, wb_ref):
    _cast_once(wb_ref, w_ref)
    h = _rms(x_ref[...], g_ref[...]).astype(BF16)
    o_ref[...] = _dot(h, wb_ref[...]).astype(o_ref.dtype)


def _norm_matmul(x2d, g, w_stacked, layer, out_dtype):
    t, d = x2d.shape
    n = w_stacked.shape[2]
    return pl.pallas_call(
        _norm_matmul_kernel,
        grid=(t // ROW_TILE,),
        in_specs=[pl.BlockSpec((ROW_TILE, d), lambda i: (i, 0)),
                  pl.BlockSpec((1, d), lambda i: (0, 0)),
                  _resident(w_stacked, layer)],
        out_specs=pl.BlockSpec((ROW_TILE, n), lambda i: (i, 0)),
        out_shape=jax.ShapeDtypeStruct((t, n), out_dtype),
        scratch_shapes=[pltpu.VMEM((d, n), BF16)],
        compiler_params=_params("arbitrary"),
        name="norm_matmul",
    )(x2d, g.reshape(1, d), w_stacked)


def _hgrn_kernel(q_ref, f_ref, i_ref, g_ref, lbz_ref, on_ref, o_ref, *, layer):
    seq, dk = q_ref.shape[1], q_ref.shape[2]
    c, grp = HGRN_CHUNK, HGRN_GROUP
    rows = c * grp
    lbz = lbz_ref[...]
    e = jnp.exp(lbz - jnp.max(lbz, axis=0, keepdims=True))
    lb = jnp.sum(e[:layer + 1], axis=0, keepdims=True) / jnp.sum(e, axis=0, keepdims=True)
    on = on_ref[...]
    row = lax.broadcasted_iota(jnp.int32, (rows, rows), 0)
    col = lax.broadcasted_iota(jnp.int32, (rows, rows), 1)
    causal = (col <= row) & (col >= (row // c) * c)
    tril = causal.astype(BF16)

    def group(n):
        sl = slice(n * rows, (n + 1) * rows)
        f = lb + (1.0 - lb) * _sigmoid(f_ref[0, sl, :])
        cum = _dot01_f32(tril, jnp.log(f))
        yield
        cum = cum * LOG2E
        k = 1.0 - f
        q_in = (_silu(q_ref[0, sl, :]) * jnp.exp2(cum)).astype(BF16)
        k_in = (k * jnp.exp2(-cum)).astype(BF16)
        att = lax.dot_general(q_in, k_in, _NT, preferred_element_type=F32)
        vb = i_ref[0, sl, :].astype(BF16)
        cum3 = cum.reshape(grp, c, dk)
        cl = cum3[:, c - 1:c, :]
        kdec = (k.reshape(grp, c, dk) * jnp.exp2(cl - cum3)).astype(BF16)
        decay = jnp.exp2(cl)
        kv = [lax.dot_general(vb[j * c:(j + 1) * c], kdec[j], _TN, preferred_element_type=F32)
              for j in range(grp)]
        yield
        att = jnp.where(causal, att, 0.0).astype(BF16)
        yield q_in, decay, kv, _dot(att, vb)

    gens = [group(n) for n in range(seq // rows)]
    for _ in range(2):
        for gen in gens:
            next(gen)
    st = jnp.zeros((dk, dk), F32)
    for n, (q_in, decay, kv, o_intra) in enumerate([next(gen) for gen in gens]):
        sl = slice(n * rows, (n + 1) * rows)
        o_inter = []
        for j in range(grp):
            o_inter.append(lax.dot_general(q_in[j * c:(j + 1) * c], st.astype(BF16), _NT,
                                           preferred_element_type=F32))
            st = decay[j] * st + kv[j]
        o = o_intra + jnp.concatenate(o_inter, axis=0)
        o = o * lax.rsqrt(jnp.mean(o * o, axis=-1, keepdims=True) + EPS)
        o_ref[0, sl, :] = (o * on * _silu(g_ref[0, sl, :])).astype(o_ref.dtype)


def _hgrn(proj, lbz, out_norm, layer, width):
    b, s, _ = proj.shape
    hd = HGRN_HEAD_DIM
    heads = width // hd
    sec = lambda k: pl.BlockSpec((1, s, hd), lambda i, h: (i, 0, k * heads + h))
    return pl.pallas_call(
        functools.partial(_hgrn_kernel, layer=layer),
        grid=(b, heads),
        in_specs=[sec(0), sec(1), sec(2), sec(3),
                  pl.BlockSpec((lbz.shape[0], hd), lambda i, h: (0, h)),
                  pl.BlockSpec((1, hd), lambda i, h: (0, h))],
        out_specs=pl.BlockSpec((1, s, hd), lambda i, h: (i, 0, h)),
        out_shape=jax.ShapeDtypeStruct((b, s, width), BF16),
        compiler_params=_params("parallel", "parallel"),
        name="hgrn2",
    )(proj, proj, proj, proj, lbz, out_norm.reshape(1, width))


def _conv_kernel(a_ref, b_ref, w_ref, db_ref, lg_ref, lb_ref, o_ref, cpad_ref, acc_ref):
    seq, ch = a_ref.shape[1], a_ref.shape[2]
    kw = w_ref.shape[0]
    pad = cpad_ref.shape[0] - seq
    tt = CONV_TILE
    cpad_ref[0:pad, :] = jnp.zeros((pad, ch), F32)

    def fill(t, carry):
        sl = pl.ds(pl.multiple_of(t * tt, tt), tt)
        cpad_ref[pl.ds(pl.multiple_of(pad + t * tt, SUBLANES), tt), :] = (
            a_ref[0, sl, :] * _sigmoid(b_ref[0, sl, :]))
        return carry

    lax.fori_loop(0, seq // tt, fill, 0)
    db, lg, lb = db_ref[...], lg_ref[...], lb_ref[...]

    offs = [pad - kw + 1 + k for k in range(kw)]
    sup = acc_ref.shape[0]
    lane_blocks = ch // LANES

    def conv(s, carry):
        sbase = pl.multiple_of(s * sup, sup)

        def block(i, c2):
            t, j = i // lane_blocks, i % lane_blocks
            base = pl.multiple_of(sbase + t * tt, tt)
            ls = pl.ds(pl.multiple_of(j * LANES, LANES), LANES)
            win = cpad_ref[pl.ds(base, tt + pad), ls]
            acc = jnp.zeros((tt, LANES), F32)
            for r in range(SUBLANES):
                taps = [k for k in range(kw) if offs[k] % SUBLANES == r]
                if not taps:
                    continue
                shifted = pltpu.roll(win, tt + pad - r, axis=0) if r else win
                for k in taps:
                    acc = acc + w_ref[k:k + 1, ls] * shifted[offs[k] - r:offs[k] - r + tt, :]
            acc_ref[pl.ds(pl.multiple_of(t * tt, tt), tt), ls] = acc
            return c2

        lax.fori_loop(0, (sup // tt) * lane_blocks, block, 0)
        acc = acc_ref[...] + db
        mu = jnp.mean(acc, axis=-1, keepdims=True)
        d = acc - mu
        y = d * lax.rsqrt(jnp.mean(d * d, axis=-1, keepdims=True) + EPS) * lg + lb
        o_ref[0, pl.ds(sbase, sup), :] = _silu(y).astype(o_ref.dtype)
        return carry

    lax.fori_loop(0, seq // sup, conv, 0)


def _conv_module(proj, col_a, col_b, dw_w, dw_b, ln_g, ln_b):
    b, s, _ = proj.shape
    kw, ch = dw_w.shape
    pad = -(-(kw - 1) // SUBLANES) * SUBLANES
    vec = lambda: pl.BlockSpec((1, ch), lambda i: (0, 0))
    return pl.pallas_call(
        _conv_kernel,
        grid=(b,),
        in_specs=[pl.BlockSpec((1, s, ch), lambda i: (i, 0, col_a)),
                  pl.BlockSpec((1, s, ch), lambda i: (i, 0, col_b)),
                  pl.BlockSpec((kw, ch), lambda i: (0, 0)), vec(), vec(), vec()],
        out_specs=pl.BlockSpec((1, s, ch), lambda i: (i, 0, 0)),
        out_shape=jax.ShapeDtypeStruct((b, s, ch), BF16),
        scratch_shapes=[pltpu.VMEM((pad + s, ch), F32), pltpu.VMEM((CONV_NORM_ROWS, ch), F32)],
        compiler_params=_params("parallel"),
        name="conv_module",
    )(proj, proj, dw_w, dw_b.reshape(1, ch), ln_g.reshape(1, ch), ln_b.reshape(1, ch))


def _moba_kernel(q_ref, k_ref, v_ref, o_ref, qaug_ref, kaug_ref, vaug_ref, s_ref, p_ref, m_ref, acc_ref):
    seq, w = q_ref.shape[1], q_ref.shape[2]
    hd, blk, topk = MOBA_HEAD_DIM, MOBA_BLOCK, MOBA_TOPK
    nblk = seq // blk
    half = blk // 2
    scale = hd ** -0.5
    k = k_ref[0]
    v = v_ref[0]
    kmean = jnp.mean(k.reshape(nblk, blk, w), axis=1)
    lane_k = lax.broadcasted_iota(jnp.int32, (seq, w), 1)
    blk_k = lax.broadcasted_iota(jnp.int32, (seq, w), 0) // blk
    for e in range(2):
        in_head = (lane_k >= e * hd) & (lane_k < (e + 1) * hd)
        onehot = (lane_k - (1 - e) * hd == blk_k).astype(F32)
        kaug_ref[e] = jnp.where(in_head, k, onehot).astype(BF16)
        vaug_ref[e] = jnp.where(in_head, v, 1.0).astype(BF16)

    lane_q = lax.broadcasted_iota(jnp.int32, (blk, w), 1)
    lane_m = lax.broadcasted_iota(jnp.int32, (nblk, w), 1)
    ridx = lax.broadcasted_iota(jnp.int32, (nblk, blk), 0)
    own_causal = (lax.broadcasted_iota(jnp.int32, (blk, blk), 1)
                  <= lax.broadcasted_iota(jnp.int32, (blk, blk), 0))

    for jq in range(nblk):
        qs = slice(jq * blk, (jq + 1) * blk)
        q2 = q_ref[0, qs, :]
        for e in range(2):
            lo = (1 - e) * hd
            if jq > topk:
                kme = jnp.where((lane_m >= e * hd) & (lane_m < (e + 1) * hd), kmean, 0.0)
                st = lax.dot_general(kme, q2, _NT, precision=lax.Precision.HIGHEST,
                                     preferred_element_type=F32)
                valid = ridx < jq
                rows = []
                for n in range(nblk):
                    if n < jq:
                        sn = st[n:n + 1, :]
                        beats = valid & ((st > sn) | ((st == sn) & (ridx < n)))
                        rank = jnp.sum(beats.astype(F32), axis=0, keepdims=True)
                        rows.append(jnp.where(rank < topk, 0.0, NEG))
                    else:
                        rows.append(jnp.zeros((1, blk), F32))
                pieces = [jnp.concatenate(rows, axis=0)]
                if lo:
                    pieces.insert(0, jnp.zeros((lo, blk), F32))
                if w - lo - nblk:
                    pieces.append(jnp.zeros((w - lo - nblk, blk), F32))
                bias_q = jnp.concatenate(pieces, axis=0).T
            else:
                bias_q = jnp.zeros((blk, w), F32)
            in_head = (lane_q >= e * hd) & (lane_q < (e + 1) * hd)
            qaug_ref[e, qs, :] = jnp.where(in_head, q2 * (scale * LOG2E), bias_q).astype(BF16)

    base = [sum(seq - i * blk for i in range(n)) for n in range(nblk)]

    def fold_max(e, rows, s, first):
        t = jnp.maximum(s[:, :half], s[:, half:])
        m_ref[e, rows, :] = t if first else jnp.maximum(m_ref[e, rows, :], t)

    def score(e, n):
        s = lax.dot_general(qaug_ref[e, n * blk:, :], kaug_ref[e, n * blk:(n + 1) * blk, :], _NT,
                            preferred_element_type=F32)
        own = jnp.where(own_causal, s[:blk], NEG)
        s_ref[e, base[n]:base[n] + blk, :] = own
        fold_max(e, slice(n * blk, (n + 1) * blk), own, n == 0)
        if n + 1 < nblk:
            s_ref[e, base[n] + blk:base[n] + seq - n * blk, :] = s[blk:]
            fold_max(e, slice((n + 1) * blk, seq), s[blk:], n == 0)

    def row_max(e):
        m_ref[e] = jnp.broadcast_to(jnp.max(m_ref[e], axis=-1, keepdims=True), (seq, half))

    def weights(e, n):
        rows = slice(base[n], base[n] + seq - n * blk)
        s = s_ref[e, rows, :]
        mb = m_ref[e, n * blk:, :]
        p_ref[e, rows, :half] = jnp.exp2(s[:, :half] - mb).astype(BF16)
        p_ref[e, rows, half:] = jnp.exp2(s[:, half:] - mb).astype(BF16)

    def values(e, n):
        rows = slice(base[n], base[n] + seq - n * blk)
        o = _dot(p_ref[e, rows, :], vaug_ref[e, n * blk:(n + 1) * blk, :])
        if n == 0:
            acc_ref[e] = o
        else:
            acc_ref[e, n * blk:, :] += o

    for n in range(nblk):
        score(0, n)
    row_max(0)
    for n in range(nblk):
        score(1, n)
        weights(0, n)
    row_max(1)
    for n in range(nblk):
        values(0, n)
        weights(1, n)
    for n in range(nblk):
        values(1, n)
    outs = []
    for e in range(2):
        lo = (1 - e) * hd
        o = acc_ref[e]
        outs.append(o * (1.0 / o[:, lo:lo + 1]))
    lane_o = lax.broadcasted_iota(jnp.int32, (seq, w), 1)
    o_ref[0] = jnp.where(lane_o < hd, outs[0], outs[1]).astype(o_ref.dtype)


def _moba(proj, width):
    b, s, _ = proj.shape
    w = 2 * MOBA_HEAD_DIM
    pairs = width // w
    nblk = s // MOBA_BLOCK
    tiles = nblk * (nblk + 1) // 2
    sec = lambda k: pl.BlockSpec((1, s, w), lambda i, h: (i, 0, k * pairs + h))
    return pl.pallas_call(
        _moba_kernel,
        grid=(b, pairs),
        in_specs=[sec(0), sec(1), sec(2)],
        out_specs=pl.BlockSpec((1, s, w), lambda i, h: (i, 0, h)),
        out_shape=jax.ShapeDtypeStruct((b, s, width), BF16),
        scratch_shapes=[pltpu.VMEM((2, s, w), BF16), pltpu.VMEM((2, s, w), BF16), pltpu.VMEM((2, s, w), BF16),
                        pltpu.VMEM((2, tiles * MOBA_BLOCK, MOBA_BLOCK), F32),
                        pltpu.VMEM((2, tiles * MOBA_BLOCK, MOBA_BLOCK), BF16),
                        pltpu.VMEM((2, s, MOBA_BLOCK // 2), F32),
                        pltpu.VMEM((2, s, w), F32)],
        compiler_params=_params("parallel", "parallel"),
        name="moba",
    )(proj, proj, proj)


def _sgu_kernel(u_ref, z_ref, lg_ref, lb_ref, w_ref, bias_ref, o_ref):
    ts, width = u_ref.shape[1], u_ref.shape[2]
    c = SGU_CHUNK
    gd = width // SGU_GROUPS
    row = lax.broadcasted_iota(jnp.int32, (c, c), 0)
    col = lax.broadcasted_iota(jnp.int32, (c, c), 1)
    for g in range(SGU_GROUPS):
        gs = slice(g * gd, (g + 1) * gd)
        z = _gelu(z_ref[0, :, gs])
        mu = jnp.mean(z, axis=-1, keepdims=True)
        d = z - mu
        zn = (d * lax.rsqrt(jnp.mean(d * d, axis=-1, keepdims=True) + EPS) * lg_ref[:, gs]
              + lb_ref[:, gs]).astype(BF16)
        wg = jnp.where(col <= row, w_ref[g], 0.0).astype(BF16)
        for n in range(ts // c):
            ts_ = slice(n * c, (n + 1) * c)
            mixed = _dot(wg, zn[ts_, :]) + bias_ref[:, gs]
            o_ref[0, ts_, gs] = (_gelu(u_ref[0, ts_, gs]) * mixed).astype(o_ref.dtype)


def _sgu(proj, col_u, col_z, ln_g, ln_b, w, bias):
    b, s, _ = proj.shape
    width = ln_g.shape[0]
    groups, c, _ = w.shape
    ts = SGU_ROWS
    bias_full = jnp.repeat(bias.T, width // groups, axis=1)
    return pl.pallas_call(
        _sgu_kernel,
        grid=(b, s // ts),
        in_specs=[pl.BlockSpec((1, ts, width), lambda i, j: (i, j, col_u)),
                  pl.BlockSpec((1, ts, width), lambda i, j: (i, j, col_z)),
                  pl.BlockSpec((1, width), lambda i, j: (0, 0)),
                  pl.BlockSpec((1, width), lambda i, j: (0, 0)),
                  pl.BlockSpec((groups, c, c), lambda i, j: (0, 0, 0)),
                  pl.BlockSpec((c, width), lambda i, j: (0, 0))],
        out_specs=pl.BlockSpec((1, ts, width), lambda i, j: (i, j, 0)),
        out_shape=jax.ShapeDtypeStruct((b, s, width), BF16),
        compiler_params=_params("parallel", "parallel"),
        name="sgu",
    )(proj, proj, ln_g.reshape(1, width), ln_b.reshape(1, width), w, bias_full)


def _cross_attention(x, g, wq_ref, kv_ref, wo_ref):
    d = x.shape[1]
    hd = d // XA_HEADS
    q = (_dot(_rms(x, g).astype(BF16), wq_ref[...]) * (hd ** -0.5 * LOG2E)).astype(BF16)

    def logits(h):
        hs = slice(h * hd, (h + 1) * hd)
        s = lax.dot_general(q[:, hs], kv_ref[0, :, hs], _NT, preferred_element_type=F32)
        return s, jnp.max(s, axis=-1, keepdims=True)

    pending = logits(0)
    outs = []
    for h in range(XA_HEADS):
        ahead = logits(h + 1) if h + 1 < XA_HEADS else None
        s, m = pending
        p = jnp.exp2(s - m)
        o = _dot(p.astype(BF16), kv_ref[0, :, d + h * hd:d + (h + 1) * hd])
        outs.append((o * (1.0 / jnp.sum(p, axis=-1, keepdims=True))).astype(BF16))
        pending = ahead
    return x + _dot(jnp.concatenate(outs, axis=-1), wo_ref[...])


def _swiglu(x, g, w1_ref, w2_ref):
    hidden = w2_ref.shape[0]
    h = _rms(x, g).astype(BF16)
    acc = x
    for c in range(hidden // FFN_CHUNK):
        cs = slice(c * FFN_CHUNK, (c + 1) * FFN_CHUNK)
        gs = slice(hidden + c * FFN_CHUNK, hidden + (c + 1) * FFN_CHUNK)
        u = (_silu(_dot(h, w1_ref[:, cs])) * _dot(h, w1_ref[:, gs])).astype(BF16)
        acc = acc + _dot(u, w2_ref[cs, :])
    return acc


def _post_mixer_kernel(x_ref, a_ref, b_ref, gx_ref, kv_ref, gf_ref, fg_ref,
                       wmix_c, wq_c, wo_c, w1_c, w2_c, o_ref,
                       wmix_b, wq_b, wo_b, w1_b, w2_b, *, final):
    i = pl.program_id(0)

    @pl.when(i < POST_CAST_STEPS)
    def _():
        for src, dst in ((wmix_c, wmix_b), (wq_c, wq_b), (wo_c, wo_b), (w1_c, w1_b), (w2_c, w2_b)):
            rows = src.shape[0]
            dst[pl.ds(pl.multiple_of(i * rows, rows), rows), :] = src[...].astype(BF16)

    @pl.when(i >= POST_CAST_STEPS)
    def _():
        ka = a_ref.shape[1]
        x = x_ref[...] + _dot(a_ref[...], wmix_b[0:ka, :]) + _dot(b_ref[...], wmix_b[ka:, :])
        x = _cross_attention(x, gx_ref[...], wq_b, kv_ref, wo_b)
        x = _swiglu(x, gf_ref[...], w1_b, w2_b)
        o_ref[...] = _rms(x, fg_ref[...]) if final else x


def _post_mixer(x2d, a2d, b2d, w_mix, mix_layer, gx, wq, kv, wo, gf, w1, w2, layer, final_gain, final, seq):
    t, d = x2d.shape
    ka, kb = a2d.shape[1], b2d.shape[1]
    m = kv.shape[1]
    nc = POST_CAST_STEPS
    per_batch = seq // ROW_TILE
    tile = lambda i: jnp.maximum(i - nc, 0)
    row = lambda n: pl.BlockSpec((ROW_TILE, n), lambda i: (tile(i), 0))
    vec = lambda: pl.BlockSpec((1, d), lambda i: (0, 0))

    def chunk(w, idx):
        return pl.BlockSpec((None, w.shape[1] // nc, w.shape[2]), lambda i: (idx, jnp.minimum(i, nc - 1), 0))

    weights = ((w_mix, mix_layer), (wq, layer), (wo, layer), (w1, layer), (w2, layer))
    return pl.pallas_call(
        functools.partial(_post_mixer_kernel, final=final),
        grid=(nc + t // ROW_TILE,),
        in_specs=[row(d), row(ka), row(kb), vec(),
                  pl.BlockSpec((1, m, 2 * d), lambda i: (tile(i) // per_batch, 0, 0)), vec(), vec()]
                 + [chunk(w, idx) for w, idx in weights],
        out_specs=row(d),
        out_shape=jax.ShapeDtypeStruct((t, d), F32),
        scratch_shapes=[pltpu.VMEM(w.shape[1:], BF16) for w, _ in weights],
        compiler_params=_params("arbitrary"),
        name="post_mixer",
    )(x2d, a2d, b2d, gx.reshape(1, d), kv, gf.reshape(1, d), final_gain.reshape(1, d),
      *[w for w, _ in weights])


def kernel(x, mem, norm_mix, norm_xattn, norm_ffn, mem_norm, final_norm, w_in_ab, w_out_ab, hgrn_lower_bounds, hgrn_out_norm, conv_dw_w, conv_dw_b, conv_ln_g, conv_ln_b, w_in_cd, w_out_cd, sgu_ln_g, sgu_ln_b, sgu_w, sgu_b, xa_wq, xa_wkv, xa_wo, ffn_w_in, ffn_w_out):
    b, s, d = x.shape
    m = mem.shape[1]
    depth = norm_mix.shape[0]
    x2d = x.reshape(b * s, d)
    mem2d = mem.reshape(b * m, d)
    for l in range(depth):
        if l % 2 == 0:
            e = l // 2
            a_width = hgrn_out_norm.shape[1]
            b_width = conv_dw_w.shape[2]
            proj = _norm_matmul(x2d, norm_mix[l], w_in_ab, e, F32).reshape(b, s, -1)
            o_a = _hgrn(proj, hgrn_lower_bounds, hgrn_out_norm[e], l, a_width)
            col = 4 * a_width // b_width
            o_b = _conv_module(proj, col, col + 1, conv_dw_w[e], conv_dw_b[e], conv_ln_g[e], conv_ln_b[e])
            w_out, mix_layer = w_out_ab, e
        else:
            o = l // 2
            d_width = sgu_ln_g.shape[1]
            c_width = w_out_cd.shape[1] - d_width
            proj = _norm_matmul(x2d, norm_mix[l], w_in_cd, o, F32).reshape(b, s, -1)
            o_a = _moba(proj, c_width)
            col = 3 * c_width // d_width
            o_b = _sgu(proj, col, col + 1, sgu_ln_g[o], sgu_ln_b[o], sgu_w[o], sgu_b[o])
            w_out, mix_layer = w_out_cd, o
        kv = _norm_matmul(mem2d, mem_norm, xa_wkv, l, BF16).reshape(b, m, 2 * d)
        x2d = _post_mixer(x2d, o_a.reshape(b * s, -1), o_b.reshape(b * s, -1), w_out, mix_layer,
                          norm_xattn[l], xa_wq, kv, xa_wo, norm_ffn[l], ffn_w_in, ffn_w_out, l,
                          final_norm, l == depth - 1, s)
    return x2d.reshape(b, s, d)
```

```python
import functools

import jax
import jax.numpy as jnp
from jax import lax
from jax.experimental import pallas as pl
from jax.experimental.pallas import tpu as pltpu

F32 = jnp.float32
BF16 = jnp.bfloat16
EPS = 1e-6
NEG = -1e30
LOG2E = 1.4426950408889634

V7X_VMEM_BYTES = 64 * 1024 * 1024
VMEM_LIMIT = V7X_VMEM_BYTES - 8 * 1024 * 1024
SUBLANES = 8
LANES = 128

HGRN_HEAD_DIM = 128
HGRN_CHUNK = 64
HGRN_GROUP = 4
CONV_TILE = 512
CONV_NORM_ROWS = 512
MOBA_HEAD_DIM = 64
MOBA_BLOCK = 256
MOBA_TOPK = 3
SGU_CHUNK = 128
SGU_GROUPS = 4
SGU_ROWS = 2048
XA_HEADS = 4
ROW_TILE = 512
FFN_CHUNK = 256
POST_CAST_STEPS = 16

_NT = (((1,), (1,)), ((), ()))
_TN = (((0,), (0,)), ((), ()))


def _params(*sem):
    return pltpu.CompilerParams(dimension_semantics=sem, vmem_limit_bytes=VMEM_LIMIT)


def _row_scale(x):
    return lax.rsqrt(jnp.mean(x * x, axis=-1, keepdims=True) + EPS)


def _rms(x, g):
    return x * _row_scale(x) * g


def _sigmoid(x):
    return 1.0 / (1.0 + jnp.exp2(x * -LOG2E))


def _silu(x):
    return x * _sigmoid(x)


def _gelu(x):
    return 0.5 * x * (1.0 + lax.erf(x * (2.0 ** -0.5)))


def _dot(a, b):
    return jnp.dot(a, b, preferred_element_type=F32)


def _dot01_f32(m01, x):
    hi = x.astype(BF16)
    lo = (x - hi.astype(F32)).astype(BF16)
    return _dot(m01, hi) + _dot(m01, lo)


def _resident(stacked, layer):
    return pl.BlockSpec((None,) + stacked.shape[1:], lambda i: (layer, 0, 0), pipeline_mode=pl.Buffered(1))


def _norm_matmul_kernel(x_ref, g_ref, w_ref, o_ref, wb_ref):
    @pl.when(pl.program_id(0) == 0)
    def _():
        wb_ref[...] = (w_ref[...] * g_ref[...]).astype(BF16)

    x = x_ref[...]
    o_ref[...] = (_dot(x.astype(BF16), wb_ref[...]) * _row_scale(x)).astype(o_ref.dtype)


def _norm_matmul(x2d, g, w_stacked, layer, out_dtype):
    t, d = x2d.shape
    n = w_stacked.shape[2]
    return pl.pallas_call(
        _norm_matmul_kernel,
        grid=(t // ROW_TILE,),
        in_specs=[pl.BlockSpec((ROW_TILE, d), lambda i: (i, 0)),
                  pl.BlockSpec((d, 1), lambda i: (0, 0)),
                  _resident(w_stacked, layer)],
        out_specs=pl.BlockSpec((ROW_TILE, n), lambda i: (i, 0)),
        out_shape=jax.ShapeDtypeStruct((t, n), out_dtype),
        scratch_shapes=[pltpu.VMEM((d, n), BF16)],
        compiler_params=_params("arbitrary"),
        name="norm_matmul",
    )(x2d, g.reshape(d, 1), w_stacked)


def _hgrn_kernel(q_ref, f_ref, i_ref, g_ref, lbz_ref, on_ref, o_ref, *, layer):
    seq, dk = q_ref.shape[1], q_ref.shape[2]
    c, grp = HGRN_CHUNK, HGRN_GROUP
    rows = c * grp
    lbz = lbz_ref[...]
    e = jnp.exp(lbz - jnp.max(lbz, axis=0, keepdims=True))
    lb = jnp.sum(e[:layer + 1], axis=0, keepdims=True) / jnp.sum(e, axis=0, keepdims=True)
    on = on_ref[...]
    row = lax.broadcasted_iota(jnp.int32, (rows, rows), 0)
    col = lax.broadcasted_iota(jnp.int32, (rows, rows), 1)
    causal = (col <= row) & (col >= (row // c) * c)
    tril = causal.astype(BF16)

    def group(n):
        sl = slice(n * rows, (n + 1) * rows)
        f = lb + (1.0 - lb) * _sigmoid(f_ref[0, sl, :])
        cum = _dot01_f32(tril, jnp.log(f))
        yield
        cum = cum * LOG2E
        k = 1.0 - f
        q_in = (_silu(q_ref[0, sl, :]) * jnp.exp2(cum)).astype(BF16)
        k_in = (k * jnp.exp2(-cum)).astype(BF16)
        att = lax.dot_general(q_in, k_in, _NT, preferred_element_type=F32)
        vb = i_ref[0, sl, :].astype(BF16)
        cum3 = cum.reshape(grp, c, dk)
        cl = cum3[:, c - 1:c, :]
        kdec = (k.reshape(grp, c, dk) * jnp.exp2(cl - cum3)).astype(BF16)
        decay = jnp.exp2(cl)
        kv = [lax.dot_general(vb[j * c:(j + 1) * c], kdec[j], _TN, preferred_element_type=F32)
              for j in range(grp)]
        yield
        att = jnp.where(causal, att, 0.0).astype(BF16)
        yield q_in, decay, kv, _dot(att, vb)

    gens = [group(n) for n in range(seq // rows)]
    for _ in range(2):
        for gen in gens:
            next(gen)
    st = jnp.zeros((dk, dk), F32)
    for n, (q_in, decay, kv, o_intra) in enumerate([next(gen) for gen in gens]):
        sl = slice(n * rows, (n + 1) * rows)
        o_inter = []
        for j in range(grp):
            o_inter.append(lax.dot_general(q_in[j * c:(j + 1) * c], st.astype(BF16), _NT,
                                           preferred_element_type=F32))
            st = decay[j] * st + kv[j]
        o = o_intra + jnp.concatenate(o_inter, axis=0)
        o = o * lax.rsqrt(jnp.mean(o * o, axis=-1, keepdims=True) + EPS)
        o_ref[0, sl, :] = (o * on * _silu(g_ref[0, sl, :])).astype(o_ref.dtype)


def _hgrn(proj, lbz, out_norm, layer, width):
    b, s, _ = proj.shape
    hd = HGRN_HEAD_DIM
    heads = width // hd
    sec = lambda k: pl.BlockSpec((1, s, hd), lambda i, h: (i, 0, k * heads + h))
    return pl.pallas_call(
        functools.partial(_hgrn_kernel, layer=layer),
        grid=(b, heads),
        in_specs=[sec(0), sec(1), sec(2), sec(3),
                  pl.BlockSpec((lbz.shape[0], hd), lambda i, h: (0, h)),
                  pl.BlockSpec((1, hd), lambda i, h: (0, h))],
        out_specs=pl.BlockSpec((1, s, hd), lambda i, h: (i, 0, h)),
        out_shape=jax.ShapeDtypeStruct((b, s, width), BF16),
        compiler_params=_params("parallel", "parallel"),
        name="hgrn2",
    )(proj, proj, proj, proj, lbz, out_norm.reshape(1, width))


def _conv_kernel(a_ref, b_ref, w_ref, db_ref, lg_ref, lb_ref, o_ref, cpad_ref, acc_ref):
    seq, ch = a_ref.shape[1], a_ref.shape[2]
    kw = w_ref.shape[0]
    pad = cpad_ref.shape[0] - seq
    tt = CONV_TILE
    cpad_ref[0:pad, :] = jnp.zeros((pad, ch), F32)

    def fill(t, carry):
        sl = pl.ds(pl.multiple_of(t * tt, tt), tt)
        cpad_ref[pl.ds(pl.multiple_of(pad + t * tt, SUBLANES), tt), :] = (
            a_ref[0, sl, :] * _sigmoid(b_ref[0, sl, :]))
        return carry

    lax.fori_loop(0, seq // tt, fill, 0)
    db, lg, lb = db_ref[...], lg_ref[...], lb_ref[...]

    offs = [pad - kw + 1 + k for k in range(kw)]
    sup = acc_ref.shape[0]
    lane_blocks = ch // LANES

    def conv(s, carry):
        sbase = pl.multiple_of(s * sup, sup)

        def block(i, c2):
            t, j = i // lane_blocks, i % lane_blocks
            base = pl.multiple_of(sbase + t * tt, tt)
            ls = pl.ds(pl.multiple_of(j * LANES, LANES), LANES)
            win = cpad_ref[pl.ds(base, tt + pad), ls]
            acc = jnp.zeros((tt, LANES), F32)
            for r in range(SUBLANES):
                taps = [k for k in range(kw) if offs[k] % SUBLANES == r]
                if not taps:
                    continue
                shifted = pltpu.roll(win, tt + pad - r, axis=0) if r else win
                for k in taps:
                    acc = acc + w_ref[k:k + 1, ls] * shifted[offs[k] - r:offs[k] - r + tt, :]
            acc_ref[pl.ds(pl.multiple_of(t * tt, tt), tt), ls] = acc
            return c2

        lax.fori_loop(0, (sup // tt) * lane_blocks, block, 0)
        acc = acc_ref[...] + db
        mu = jnp.mean(acc, axis=-1, keepdims=True)
        d = acc - mu
        y = d * lax.rsqrt(jnp.mean(d * d, axis=-1, keepdims=True) + EPS) * lg + lb
        o_ref[0, pl.ds(sbase, sup), :] = _silu(y).astype(o_ref.dtype)
        return carry

    lax.fori_loop(0, seq // sup, conv, 0)


def _conv_module(proj, col_a, col_b, dw_w, dw_b, ln_g, ln_b):
    b, s, _ = proj.shape
    kw, ch = dw_w.shape
    pad = -(-(kw - 1) // SUBLANES) * SUBLANES
    vec = lambda: pl.BlockSpec((1, ch), lambda i: (0, 0))
    return pl.pallas_call(
        _conv_kernel,
        grid=(b,),
        in_specs=[pl.BlockSpec((1, s, ch), lambda i: (i, 0, col_a)),
                  pl.BlockSpec((1, s, ch), lambda i: (i, 0, col_b)),
                  pl.BlockSpec((kw, ch), lambda i: (0, 0)), vec(), vec(), vec()],
        out_specs=pl.BlockSpec((1, s, ch), lambda i: (i, 0, 0)),
        out_shape=jax.ShapeDtypeStruct((b, s, ch), BF16),
        scratch_shapes=[pltpu.VMEM((pad + s, ch), F32), pltpu.VMEM((CONV_NORM_ROWS, ch), F32)],
        compiler_params=_params("parallel"),
        name="conv_module",
    )(proj, proj, dw_w, dw_b.reshape(1, ch), ln_g.reshape(1, ch), ln_b.reshape(1, ch))


def _moba_kernel(q_ref, k_ref, v_ref, o_ref, qaug_ref, kaug_ref, vaug_ref, s_ref, p_ref, m_ref, acc_ref):
    seq, w = q_ref.shape[1], q_ref.shape[2]
    hd, blk, topk = MOBA_HEAD_DIM, MOBA_BLOCK, MOBA_TOPK
    nblk = seq // blk
    half = blk // 2
    scale = hd ** -0.5
    k = k_ref[0]
    v = v_ref[0]
    kmean = jnp.mean(k.reshape(nblk, blk, w), axis=1)
    lane_k = lax.broadcasted_iota(jnp.int32, (seq, w), 1)
    blk_k = lax.broadcasted_iota(jnp.int32, (seq, w), 0) // blk
    for e in range(2):
        in_head = (lane_k >= e * hd) & (lane_k < (e + 1) * hd)
        onehot = (lane_k - (1 - e) * hd == blk_k).astype(F32)
        kaug_ref[e] = jnp.where(in_head, k, onehot).astype(BF16)
        vaug_ref[e] = jnp.where(in_head, v, 1.0).astype(BF16)

    lane_q = lax.broadcasted_iota(jnp.int32, (blk, w), 1)
    lane_m = lax.broadcasted_iota(jnp.int32, (nblk, w), 1)
    ridx = lax.broadcasted_iota(jnp.int32, (nblk, blk), 0)
    own_causal = (lax.broadcasted_iota(jnp.int32, (blk, blk), 1)
                  <= lax.broadcasted_iota(jnp.int32, (blk, blk), 0))

    for jq in range(nblk):
        qs = slice(jq * blk, (jq + 1) * blk)
        q2 = q_ref[0, qs, :]
        for e in range(2):
            lo = (1 - e) * hd
            if jq > topk:
                kme = jnp.where((lane_m >= e * hd) & (lane_m < (e + 1) * hd), kmean, 0.0)
                st = lax.dot_general(kme, q2, _NT, precision=lax.Precision.HIGHEST,
                                     preferred_element_type=F32)
                valid = ridx < jq
                rows = []
                for n in range(nblk):
                    if n < jq:
                        sn = st[n:n + 1, :]
                        beats = valid & ((st > sn) | ((st == sn) & (ridx < n)))
                        rank = jnp.sum(beats.astype(F32), axis=0, keepdims=True)
                        rows.append(jnp.where(rank < topk, 0.0, NEG))
                    else:
                        rows.append(jnp.zeros((1, blk), F32))
                pieces = [jnp.concatenate(rows, axis=0)]
                if lo:
                    pieces.insert(0, jnp.zeros((lo, blk), F32))
                if w - lo - nblk:
                    pieces.append(jnp.zeros((w - lo - nblk, blk), F32))
                bias_q = jnp.concatenate(pieces, axis=0).T
            else:
                bias_q = jnp.zeros((blk, w), F32)
            in_head = (lane_q >= e * hd) & (lane_q < (e + 1) * hd)
            qaug_ref[e, qs, :] = jnp.where(in_head, q2 * (scale * LOG2E), bias_q).astype(BF16)

    base = [sum(seq - i * blk for i in range(n)) for n in range(nblk)]

    def fold_max(e, rows, s, first):
        t = jnp.maximum(s[:, :half], s[:, half:])
        m_ref[e, rows, :] = t if first else jnp.maximum(m_ref[e, rows, :], t)

    def score(e, n):
        s = lax.dot_general(qaug_ref[e, n * blk:, :], kaug_ref[e, n * blk:(n + 1) * blk, :], _NT,
                            preferred_element_type=F32)
        own = jnp.where(own_causal, s[:blk], NEG)
        s_ref[e, base[n]:base[n] + blk, :] = own
        fold_max(e, slice(n * blk, (n + 1) * blk), own, n == 0)
        if n + 1 < nblk:
            s_ref[e, base[n] + blk:base[n] + seq - n * blk, :] = s[blk:]
            fold_max(e, slice((n + 1) * blk, seq), s[blk:], n == 0)

    def row_max(e):
        m_ref[e] = jnp.broadcast_to(jnp.max(m_ref[e], axis=-1, keepdims=True), (seq, half))

    def weights(e, n):
        rows = slice(base[n], base[n] + seq - n * blk)
        s = s_ref[e, rows, :]
        mb = m_ref[e, n * blk:, :]
        p_ref[e, rows, :half] = jnp.exp2(s[:, :half] - mb).astype(BF16)
        p_ref[e, rows, half:] = jnp.exp2(s[:, half:] - mb).astype(BF16)

    def values(e, n):
        rows = slice(base[n], base[n] + seq - n * blk)
        o = _dot(p_ref[e, rows, :], vaug_ref[e, n * blk:(n + 1) * blk, :])
        if n == 0:
            acc_ref[e] = o
        else:
            acc_ref[e, n * blk:, :] += o

    for n in range(nblk):
        score(0, n)
    row_max(0)
    for n in range(nblk):
        score(1, n)
        weights(0, n)
    row_max(1)
    for n in range(nblk):
        values(0, n)
        weights(1, n)
    for n in range(nblk):
        values(1, n)
    outs = []
    for e in range(2):
        lo = (1 - e) * hd
        o = acc_ref[e]
        outs.append(o * (1.0 / o[:, lo:lo + 1]))
    lane_o = lax.broadcasted_iota(jnp.int32, (seq, w), 1)
    o_ref[0] = jnp.where(lane_o < hd, outs[0], outs[1]).astype(o_ref.dtype)


def _moba(proj, width):
    b, s, _ = proj.shape
    w = 2 * MOBA_HEAD_DIM
    pairs = width // w
    nblk = s // MOBA_BLOCK
    tiles = nblk * (nblk + 1) // 2
    sec = lambda k: pl.BlockSpec((1, s, w), lambda i, h: (i, 0, k * pairs + h))
    return pl.pallas_call(
        _moba_kernel,
        grid=(b, pairs),
        in_specs=[sec(0), sec(1), sec(2)],
        out_specs=pl.BlockSpec((1, s, w), lambda i, h: (i, 0, h)),
        out_shape=jax.ShapeDtypeStruct((b, s, width), BF16),
        scratch_shapes=[pltpu.VMEM((2, s, w), BF16), pltpu.VMEM((2, s, w), BF16), pltpu.VMEM((2, s, w), BF16),
                        pltpu.VMEM((2, tiles * MOBA_BLOCK, MOBA_BLOCK), F32),
                        pltpu.VMEM((2, tiles * MOBA_BLOCK, MOBA_BLOCK), BF16),
                        pltpu.VMEM((2, s, MOBA_BLOCK // 2), F32),
                        pltpu.VMEM((2, s, w), F32)],
        compiler_params=_params("parallel", "parallel"),
        name="moba",
    )(proj, proj, proj)


def _sgu_kernel(u_ref, z_ref, lg_ref, lb_ref, w_ref, bias_ref, o_ref):
    ts, width = u_ref.shape[1], u_ref.shape[2]
    c = SGU_CHUNK
    gd = width // SGU_GROUPS
    row = lax.broadcasted_iota(jnp.int32, (c, c), 0)
    col = lax.broadcasted_iota(jnp.int32, (c, c), 1)
    for g in range(SGU_GROUPS):
        gs = slice(g * gd, (g + 1) * gd)
        z = _gelu(z_ref[0, :, gs])
        mu = jnp.mean(z, axis=-1, keepdims=True)
        d = z - mu
        zn = (d * lax.rsqrt(jnp.mean(d * d, axis=-1, keepdims=True) + EPS) * lg_ref[:, gs]
              + lb_ref[:, gs]).astype(BF16)
        wg = jnp.where(col <= row, w_ref[g], 0.0).astype(BF16)
        for n in range(ts // c):
            ts_ = slice(n * c, (n + 1) * c)
            mixed = _dot(wg, zn[ts_, :]) + bias_ref[:, gs]
            o_ref[0, ts_, gs] = (_gelu(u_ref[0, ts_, gs]) * mixed).astype(o_ref.dtype)


def _sgu(proj, col_u, col_z, ln_g, ln_b, w, bias):
    b, s, _ = proj.shape
    width = ln_g.shape[0]
    groups, c, _ = w.shape
    ts = SGU_ROWS
    bias_full = jnp.repeat(bias.T, width // groups, axis=1)
    return pl.pallas_call(
        _sgu_kernel,
        grid=(b, s // ts),
        in_specs=[pl.BlockSpec((1, ts, width), lambda i, j: (i, j, col_u)),
                  pl.BlockSpec((1, ts, width), lambda i, j: (i, j, col_z)),
                  pl.BlockSpec((1, width), lambda i, j: (0, 0)),
                  pl.BlockSpec((1, width), lambda i, j: (0, 0)),
                  pl.BlockSpec((groups, c, c), lambda i, j: (0, 0, 0)),
                  pl.BlockSpec((c, width), lambda i, j: (0, 0))],
        out_specs=pl.BlockSpec((1, ts, width), lambda i, j: (i, j, 0)),
        out_shape=jax.ShapeDtypeStruct((b, s, width), BF16),
        compiler_params=_params("parallel", "parallel"),
        name="sgu",
    )(proj, proj, ln_g.reshape(1, width), ln_b.reshape(1, width), w, bias_full)


def _cross_attention(x, wq_ref, kv_ref, wo_ref):
    d = x.shape[1]
    hd = d // XA_HEADS
    q = (_dot(x.astype(BF16), wq_ref[...]) * (_row_scale(x) * (hd ** -0.5 * LOG2E))).astype(BF16)

    def logits(h):
        hs = slice(h * hd, (h + 1) * hd)
        s = lax.dot_general(q[:, hs], kv_ref[0, :, hs], _NT, preferred_element_type=F32)
        return s, jnp.max(s, axis=-1, keepdims=True)

    pending = logits(0)
    outs = []
    for h in range(XA_HEADS):
        ahead = logits(h + 1) if h + 1 < XA_HEADS else None
        s, m = pending
        p = jnp.exp2(s - m)
        o = _dot(p.astype(BF16), kv_ref[0, :, d + h * hd:d + (h + 1) * hd])
        outs.append((o * (1.0 / jnp.sum(p, axis=-1, keepdims=True))).astype(BF16))
        pending = ahead
    return x + _dot(jnp.concatenate(outs, axis=-1), wo_ref[...])


def _swiglu(x, w1_ref, w2_ref):
    hidden = w2_ref.shape[0]
    r = _row_scale(x)
    h = x.astype(BF16)
    acc = x
    for c in range(hidden // FFN_CHUNK):
        cs = slice(c * FFN_CHUNK, (c + 1) * FFN_CHUNK)
        gs = slice(hidden + c * FFN_CHUNK, hidden + (c + 1) * FFN_CHUNK)
        u = (_silu(_dot(h, w1_ref[:, cs]) * r) * (_dot(h, w1_ref[:, gs]) * r)).astype(BF16)
        acc = acc + _dot(u, w2_ref[cs, :])
    return acc


def _post_mixer_kernel(x_ref, a_ref, b_ref, kv_ref, fg_ref, gx_c, gf_c,
                       wmix_c, wq_c, wo_c, w1_c, w2_c, o_ref,
                       wmix_b, wq_b, wo_b, w1_b, w2_b, *, final):
    i = pl.program_id(0)

    @pl.when(i < POST_CAST_STEPS)
    def _():
        for src, dst, gain in ((wmix_c, wmix_b, None), (wq_c, wq_b, gx_c), (wo_c, wo_b, None),
                               (w1_c, w1_b, gf_c), (w2_c, w2_b, None)):
            rows = src.shape[0]
            chunk = src[...] if gain is None else src[...] * gain[...]
            dst[pl.ds(pl.multiple_of(i * rows, rows), rows), :] = chunk.astype(BF16)

    @pl.when(i >= POST_CAST_STEPS)
    def _():
        ka = a_ref.shape[1]
        x = x_ref[...] + _dot(a_ref[...], wmix_b[0:ka, :]) + _dot(b_ref[...], wmix_b[ka:, :])
        x = _cross_attention(x, wq_b, kv_ref, wo_b)
        x = _swiglu(x, w1_b, w2_b)
        o_ref[...] = _rms(x, fg_ref[...]) if final else x


def _post_mixer(x2d, a2d, b2d, w_mix, mix_layer, gx, wq, kv, wo, gf, w1, w2, layer, final_gain, final, seq):
    t, d = x2d.shape
    ka, kb = a2d.shape[1], b2d.shape[1]
    m = kv.shape[1]
    nc = POST_CAST_STEPS
    per_batch = seq // ROW_TILE
    tile = lambda i: jnp.maximum(i - nc, 0)
    row = lambda n: pl.BlockSpec((ROW_TILE, n), lambda i: (tile(i), 0))
    vec = lambda: pl.BlockSpec((1, d), lambda i: (0, 0))

    def chunk(w, idx):
        return pl.BlockSpec((None, w.shape[1] // nc, w.shape[2]), lambda i: (idx, jnp.minimum(i, nc - 1), 0))

    def gain():
        return pl.BlockSpec((d // nc, 1), lambda i: (jnp.minimum(i, nc - 1), 0))

    weights = ((w_mix, mix_layer), (wq, layer), (wo, layer), (w1, layer), (w2, layer))
    return pl.pallas_call(
        functools.partial(_post_mixer_kernel, final=final),
        grid=(nc + t // ROW_TILE,),
        in_specs=[row(d), row(ka), row(kb),
                  pl.BlockSpec((1, m, 2 * d), lambda i: (tile(i) // per_batch, 0, 0)), vec(), gain(), gain()]
                 + [chunk(w, idx) for w, idx in weights],
        out_specs=row(d),
        out_shape=jax.ShapeDtypeStruct((t, d), F32),
        scratch_shapes=[pltpu.VMEM(w.shape[1:], BF16) for w, _ in weights],
        compiler_params=_params("arbitrary"),
        name="post_mixer",
    )(x2d, a2d, b2d, kv, final_gain.reshape(1, d), gx.reshape(d, 1), gf.reshape(d, 1),
      *[w for w, _ in weights])


def kernel(x, mem, norm_mix, norm_xattn, norm_ffn, mem_norm, final_norm, w_in_ab, w_out_ab, hgrn_lower_bounds, hgrn_out_norm, conv_dw_w, conv_dw_b, conv_ln_g, conv_ln_b, w_in_cd, w_out_cd, sgu_ln_g, sgu_ln_b, sgu_w, sgu_b, xa_wq, xa_wkv, xa_wo, ffn_w_in, ffn_w_out):
    b, s, d = x.shape
    m = mem.shape[1]
    depth = norm_mix.shape[0]
    x2d = x.reshape(b * s, d)
    mem2d = mem.reshape(b * m, d)
    for l in range(depth):
        if l % 2 == 0:
            e = l // 2
            a_width = hgrn_out_norm.shape[1]
            b_width = conv_dw_w.shape[2]
            proj = _norm_matmul(x2d, norm_mix[l], w_in_ab, e, F32).reshape(b, s, -1)
            o_a = _hgrn(proj, hgrn_lower_bounds, hgrn_out_norm[e], l, a_width)
            col = 4 * a_width // b_width
            o_b = _conv_module(proj, col, col + 1, conv_dw_w[e], conv_dw_b[e], conv_ln_g[e], conv_ln_b[e])
            w_out, mix_layer = w_out_ab, e
        else:
            o = l // 2
            d_width = sgu_ln_g.shape[1]
            c_width = w_out_cd.shape[1] - d_width
            proj = _norm_matmul(x2d, norm_mix[l], w_in_cd, o, F32).reshape(b, s, -1)
            o_a = _moba(proj, c_width)
            col = 3 * c_width // d_width
            o_b = _sgu(proj, col, col + 1, sgu_ln_g[o], sgu_ln_b[o], sgu_w[o], sgu_b[o])
            w_out, mix_layer = w_out_cd, o
        kv = _norm_matmul(mem2d, mem_norm, xa_wkv, l, BF16).reshape(b, m, 2 * d)
        x2d = _post_mixer(x2d, o_a.reshape(b * s, -1), o_b.reshape(b * s, -1), w_out, mix_layer,
                          norm_xattn[l], xa_wq, kv, xa_wo, norm_ffn[l], ffn_w_in, ffn_w_out, l,
                          final_norm, l == depth - 1, s)
    return x2d.reshape(b, s, d)
```

```python
import functools

import jax
import jax.numpy as jnp
from jax import lax
from jax.experimental import pallas as pl
from jax.experimental.pallas import tpu as pltpu

F32 = jnp.float32
BF16 = jnp.bfloat16
EPS = 1e-6
NEG = -1e30
LOG2E = 1.4426950408889634

V7X_VMEM_BYTES = 64 * 1024 * 1024
VMEM_LIMIT = V7X_VMEM_BYTES - 8 * 1024 * 1024
SUBLANES = 8
LANES = 128

HGRN_HEAD_DIM = 128
HGRN_CHUNK = 64
HGRN_HEADS_PER_STEP = 4
HGRN_GROUP = 4
CONV_TILE = 2048
CONV_NORM_ROWS = 2048
MOBA_HEAD_DIM = 64
MOBA_BLOCK = 256
MOBA_TOPK = 3
SGU_CHUNK = 128
SGU_GROUPS = 4
SGU_ROWS = 2048
XA_HEADS = 4
ROW_TILE = 512
FFN_CHUNK = 256
POST_CAST_STEPS = 16

_NT = (((1,), (1,)), ((), ()))
_TN = (((0,), (0,)), ((), ()))


def _params(*sem):
    return pltpu.CompilerParams(dimension_semantics=sem, vmem_limit_bytes=VMEM_LIMIT)


def _row_scale(x):
    return lax.rsqrt(jnp.mean(x * x, axis=-1, keepdims=True) + EPS)


def _rms(x, g):
    return x * _row_scale(x) * g


def _sigmoid(x):
    return 1.0 / (1.0 + jnp.exp2(x * -LOG2E))


def _silu(x):
    return x * _sigmoid(x)


def _gelu(x):
    return 0.5 * x * (1.0 + lax.erf(x * (2.0 ** -0.5)))


def _dot(a, b):
    return jnp.dot(a, b, preferred_element_type=F32)


def _dot01_f32(m01, x):
    hi = x.astype(BF16)
    lo = (x - hi.astype(F32)).astype(BF16)
    return _dot(m01, hi) + _dot(m01, lo)


def _resident(stacked, layer):
    return pl.BlockSpec((None,) + stacked.shape[1:], lambda i: (layer, 0, 0), pipeline_mode=pl.Buffered(1))


def _norm_matmul_kernel(x_ref, g_ref, w_ref, o_ref, wb_ref):
    @pl.when(pl.program_id(0) == 0)
    def _():
        wb_ref[...] = (w_ref[...] * g_ref[...]).astype(BF16)

    x = x_ref[...]
    o_ref[...] = (_dot(x.astype(BF16), wb_ref[...]) * _row_scale(x)).astype(o_ref.dtype)


def _norm_matmul(x2d, g, w_stacked, layer, out_dtype):
    t, d = x2d.shape
    n = w_stacked.shape[2]
    return pl.pallas_call(
        _norm_matmul_kernel,
        grid=(t // ROW_TILE,),
        in_specs=[pl.BlockSpec((ROW_TILE, d), lambda i: (i, 0)),
                  pl.BlockSpec((d, 1), lambda i: (0, 0)),
                  _resident(w_stacked, layer)],
        out_specs=pl.BlockSpec((ROW_TILE, n), lambda i: (i, 0)),
        out_shape=jax.ShapeDtypeStruct((t, n), out_dtype),
        scratch_shapes=[pltpu.VMEM((d, n), BF16)],
        compiler_params=_params("arbitrary"),
        name="norm_matmul",
    )(x2d, g.reshape(d, 1), w_stacked)


def _hgrn_kernel(q_ref, f_ref, i_ref, g_ref, lbz_ref, on_ref, o_ref, *, layer):
    seq, dk = q_ref.shape[1], HGRN_HEAD_DIM
    heads_here = q_ref.shape[2] // dk
    c, grp = HGRN_CHUNK, HGRN_GROUP
    rows = c * grp
    lbz = lbz_ref[...]
    e = jnp.exp(lbz - jnp.max(lbz, axis=0, keepdims=True))
    lb_all = jnp.sum(e[:layer + 1], axis=0, keepdims=True) / jnp.sum(e, axis=0, keepdims=True)
    on_all = on_ref[...]
    row = lax.broadcasted_iota(jnp.int32, (rows, rows), 0)
    col = lax.broadcasted_iota(jnp.int32, (rows, rows), 1)
    causal = (col <= row) & (col >= (row // c) * c)
    tril = causal.astype(BF16)

    def group(hh, n):
        sl = slice(n * rows, (n + 1) * rows)
        ls = slice(hh * dk, (hh + 1) * dk)
        lb = lb_all[:, ls]
        f = lb + (1.0 - lb) * _sigmoid(f_ref[0, sl, ls])
        cum = _dot01_f32(tril, jnp.log(f))
        yield
        cum = cum * LOG2E
        k = 1.0 - f
        q_in = (_silu(q_ref[0, sl, ls]) * jnp.exp2(cum)).astype(BF16)
        k_in = (k * jnp.exp2(-cum)).astype(BF16)
        att = lax.dot_general(q_in, k_in, _NT, preferred_element_type=F32)
        vb = i_ref[0, sl, ls].astype(BF16)
        cum3 = cum.reshape(grp, c, dk)
        cl = cum3[:, c - 1:c, :]
        kdec = (k.reshape(grp, c, dk) * jnp.exp2(cl - cum3)).astype(BF16)
        decay = jnp.exp2(cl)
        kv = [lax.dot_general(vb[j * c:(j + 1) * c], kdec[j], _TN, preferred_element_type=F32)
              for j in range(grp)]
        yield
        att = jnp.where(causal, att, 0.0).astype(BF16)
        yield q_in, decay, kv, _dot(att, vb)

    order = [(hh, n) for n in range(seq // rows) for hh in range(heads_here)]
    gens = [group(hh, n) for hh, n in order]
    for _ in range(2):
        for gen in gens:
            next(gen)
    states = [jnp.zeros((dk, dk), F32) for _ in range(heads_here)]
    for (hh, n), (q_in, decay, kv, o_intra) in zip(order, [next(gen) for gen in gens]):
        sl = slice(n * rows, (n + 1) * rows)
        ls = slice(hh * dk, (hh + 1) * dk)
        st = states[hh]
        o_inter = []
        for j in range(grp):
            o_inter.append(lax.dot_general(q_in[j * c:(j + 1) * c], st.astype(BF16), _NT,
                                           preferred_element_type=F32))
            st = decay[j] * st + kv[j]
        states[hh] = st
        o = o_intra + jnp.concatenate(o_inter, axis=0)
        o = o * lax.rsqrt(jnp.mean(o * o, axis=-1, keepdims=True) + EPS)
        o_ref[0, sl, ls] = (o * on_all[:, ls] * _silu(g_ref[0, sl, ls])).astype(o_ref.dtype)


def _hgrn(proj, lbz, out_norm, layer, width):
    b, s, _ = proj.shape
    hd = HGRN_HEAD_DIM * HGRN_HEADS_PER_STEP
    heads = width // hd
    sec = lambda k: pl.BlockSpec((1, s, hd), lambda i, h: (i, 0, k * heads + h))
    return pl.pallas_call(
        functools.partial(_hgrn_kernel, layer=layer),
        grid=(b, heads),
        in_specs=[sec(0), sec(1), sec(2), sec(3),
                  pl.BlockSpec((lbz.shape[0], hd), lambda i, h: (0, h)),
                  pl.BlockSpec((1, hd), lambda i, h: (0, h))],
        out_specs=pl.BlockSpec((1, s, hd), lambda i, h: (i, 0, h)),
        out_shape=jax.ShapeDtypeStruct((b, s, width), BF16),
        compiler_params=_params("parallel", "parallel"),
        name="hgrn2",
    )(proj, proj, proj, proj, lbz, out_norm.reshape(1, width))


def _conv_kernel(a_ref, b_ref, w_ref, db_ref, lg_ref, lb_ref, o_ref, cpad_ref, acc_ref):
    seq, ch = a_ref.shape[1], a_ref.shape[2]
    kw = w_ref.shape[0]
    pad = cpad_ref.shape[0] - seq
    tt = CONV_TILE
    cpad_ref[0:pad, :] = jnp.zeros((pad, ch), F32)

    def fill(t, carry):
        sl = pl.ds(pl.multiple_of(t * tt, tt), tt)
        cpad_ref[pl.ds(pl.multiple_of(pad + t * tt, SUBLANES), tt), :] = (
            a_ref[0, sl, :] * _sigmoid(b_ref[0, sl, :]))
        return carry

    lax.fori_loop(0, seq // tt, fill, 0)
    db, lg, lb = db_ref[...], lg_ref[...], lb_ref[...]

    offs = [pad - kw + 1 + k for k in range(kw)]
    sup = acc_ref.shape[0]
    lane_blocks = ch // LANES

    def conv(s, carry):
        sbase = pl.multiple_of(s * sup, sup)

        def block(i, c2):
            t, j = i // lane_blocks, i % lane_blocks
            base = pl.multiple_of(sbase + t * tt, tt)
            ls = pl.ds(pl.multiple_of(j * LANES, LANES), LANES)
            win = cpad_ref[pl.ds(base, tt + pad), ls]
            acc = jnp.zeros((tt, LANES), F32)
            for r in range(SUBLANES):
                taps = [k for k in range(kw) if offs[k] % SUBLANES == r]
                if not taps:
                    continue
                shifted = pltpu.roll(win, tt + pad - r, axis=0) if r else win
                for k in taps:
                    acc = acc + w_ref[k:k + 1, ls] * shifted[offs[k] - r:offs[k] - r + tt, :]
            acc_ref[pl.ds(pl.multiple_of(t * tt, tt), tt), ls] = acc
            return c2

        lax.fori_loop(0, (sup // tt) * lane_blocks, block, 0)
        acc = acc_ref[...] + db
        mu = jnp.mean(acc, axis=-1, keepdims=True)
        d = acc - mu
        y = d * lax.rsqrt(jnp.mean(d * d, axis=-1, keepdims=True) + EPS) * lg + lb
        o_ref[0, pl.ds(sbase, sup), :] = _silu(y).astype(o_ref.dtype)
        return carry

    lax.fori_loop(0, seq // sup, conv, 0)


def _conv_module(proj, col_a, col_b, dw_w, dw_b, ln_g, ln_b):
    b, s, _ = proj.shape
    kw, ch = dw_w.shape
    pad = -(-(kw - 1) // SUBLANES) * SUBLANES
    vec = lambda: pl.BlockSpec((1, ch), lambda i: (0, 0))
    return pl.pallas_call(
        _conv_kernel,
        grid=(b,),
        in_specs=[pl.BlockSpec((1, s, ch), lambda i: (i, 0, col_a)),
                  pl.BlockSpec((1, s, ch), lambda i: (i, 0, col_b)),
                  pl.BlockSpec((kw, ch), lambda i: (0, 0)), vec(), vec(), vec()],
        out_specs=pl.BlockSpec((1, s, ch), lambda i: (i, 0, 0)),
        out_shape=jax.ShapeDtypeStruct((b, s, ch), BF16),
        scratch_shapes=[pltpu.VMEM((pad + s, ch), F32), pltpu.VMEM((CONV_NORM_ROWS, ch), F32)],
        compiler_params=_params("parallel"),
        name="conv_module",
    )(proj, proj, dw_w, dw_b.reshape(1, ch), ln_g.reshape(1, ch), ln_b.reshape(1, ch))


def _moba_kernel(q_ref, k_ref, v_ref, o_ref, qaug_ref, kaug_ref, vaug_ref, s_ref, p_ref, m_ref, acc_ref):
    seq, w = q_ref.shape[1], q_ref.shape[2]
    hd, blk, topk = MOBA_HEAD_DIM, MOBA_BLOCK, MOBA_TOPK
    nblk = seq // blk
    half = blk // 2
    scale = hd ** -0.5
    k = k_ref[0]
    v = v_ref[0]
    kmean = jnp.mean(k.reshape(nblk, blk, w), axis=1)
    lane_k = lax.broadcasted_iota(jnp.int32, (seq, w), 1)
    blk_k = lax.broadcasted_iota(jnp.int32, (seq, w), 0) // blk
    for e in range(2):
        in_head = (lane_k >= e * hd) & (lane_k < (e + 1) * hd)
        onehot = (lane_k - (1 - e) * hd == blk_k).astype(F32)
        kaug_ref[e] = jnp.where(in_head, k, onehot).astype(BF16)
        vaug_ref[e] = jnp.where(in_head, v, 1.0).astype(BF16)

    lane_q = lax.broadcasted_iota(jnp.int32, (blk, w), 1)
    lane_m = lax.broadcasted_iota(jnp.int32, (nblk, w), 1)
    ridx = lax.broadcasted_iota(jnp.int32, (nblk, blk), 0)
    own_causal = (lax.broadcasted_iota(jnp.int32, (blk, blk), 1)
                  <= lax.broadcasted_iota(jnp.int32, (blk, blk), 0))

    for jq in range(nblk):
        qs = slice(jq * blk, (jq + 1) * blk)
        q2 = q_ref[0, qs, :]
        for e in range(2):
            lo = (1 - e) * hd
            if jq > topk:
                kme = jnp.where((lane_m >= e * hd) & (lane_m < (e + 1) * hd), kmean, 0.0)
                st = lax.dot_general(kme, q2, _NT, precision=lax.Precision.HIGHEST,
                                     preferred_element_type=F32)
                valid = ridx < jq
                rows = []
                for n in range(nblk):
                    if n < jq:
                        sn = st[n:n + 1, :]
                        beats = valid & ((st > sn) | ((st == sn) & (ridx < n)))
                        rank = jnp.sum(beats.astype(F32), axis=0, keepdims=True)
                        rows.append(jnp.where(rank < topk, 0.0, NEG))
                    else:
                        rows.append(jnp.zeros((1, blk), F32))
                pieces = [jnp.concatenate(rows, axis=0)]
                if lo:
                    pieces.insert(0, jnp.zeros((lo, blk), F32))
                if w - lo - nblk:
                    pieces.append(jnp.zeros((w - lo - nblk, blk), F32))
                bias_q = jnp.concatenate(pieces, axis=0).T
            else:
                bias_q = jnp.zeros((blk, w), F32)
            in_head = (lane_q >= e * hd) & (lane_q < (e + 1) * hd)
            qaug_ref[e, qs, :] = jnp.where(in_head, q2 * (scale * LOG2E), bias_q).astype(BF16)

    base = [sum(seq - i * blk for i in range(n)) for n in range(nblk)]

    def fold_max(e, rows, s, first):
        t = jnp.maximum(s[:, :half], s[:, half:])
        m_ref[e, rows, :] = t if first else jnp.maximum(m_ref[e, rows, :], t)

    def score(e, n):
        s = lax.dot_general(qaug_ref[e, n * blk:, :], kaug_ref[e, n * blk:(n + 1) * blk, :], _NT,
                            preferred_element_type=F32)
        own = jnp.where(own_causal, s[:blk], NEG)
        s_ref[e, base[n]:base[n] + blk, :] = own
        fold_max(e, slice(n * blk, (n + 1) * blk), own, n == 0)
        if n + 1 < nblk:
            s_ref[e, base[n] + blk:base[n] + seq - n * blk, :] = s[blk:]
            fold_max(e, slice((n + 1) * blk, seq), s[blk:], n == 0)

    def row_max(e):
        m_ref[e] = jnp.broadcast_to(jnp.max(m_ref[e], axis=-1, keepdims=True), (seq, half))

    def weights(e, n):
        rows = slice(base[n], base[n] + seq - n * blk)
        s = s_ref[e, rows, :]
        mb = m_ref[e, n * blk:, :]
        p_ref[e, rows, :half] = jnp.exp2(s[:, :half] - mb).astype(BF16)
        p_ref[e, rows, half:] = jnp.exp2(s[:, half:] - mb).astype(BF16)

    def values(e, n):
        rows = slice(base[n], base[n] + seq - n * blk)
        o = _dot(p_ref[e, rows, :], vaug_ref[e, n * blk:(n + 1) * blk, :])
        if n == 0:
            acc_ref[e] = o
        else:
            acc_ref[e, n * blk:, :] += o

    for n in range(nblk):
        score(0, n)
    row_max(0)
    for n in range(nblk):
        score(1, n)
        weights(0, n)
    row_max(1)
    for n in range(nblk):
        values(0, n)
        weights(1, n)
    for n in range(nblk):
        values(1, n)
    outs = []
    for e in range(2):
        lo = (1 - e) * hd
        o = acc_ref[e]
        outs.append(o * (1.0 / o[:, lo:lo + 1]))
    lane_o = lax.broadcasted_iota(jnp.int32, (seq, w), 1)
    o_ref[0] = jnp.where(lane_o < hd, outs[0], outs[1]).astype(o_ref.dtype)


def _moba(proj, width):
    b, s, _ = proj.shape
    w = 2 * MOBA_HEAD_DIM
    pairs = width // w
    nblk = s // MOBA_BLOCK
    tiles = nblk * (nblk + 1) // 2
    sec = lambda k: pl.BlockSpec((1, s, w), lambda i, h: (i, 0, k * pairs + h))
    return pl.pallas_call(
        _moba_kernel,
        grid=(b, pairs),
        in_specs=[sec(0), sec(1), sec(2)],
        out_specs=pl.BlockSpec((1, s, w), lambda i, h: (i, 0, h)),
        out_shape=jax.ShapeDtypeStruct((b, s, width), BF16),
        scratch_shapes=[pltpu.VMEM((2, s, w), BF16), pltpu.VMEM((2, s, w), BF16), pltpu.VMEM((2, s, w), BF16),
                        pltpu.VMEM((2, tiles * MOBA_BLOCK, MOBA_BLOCK), F32),
                        pltpu.VMEM((2, tiles * MOBA_BLOCK, MOBA_BLOCK), BF16),
                        pltpu.VMEM((2, s, MOBA_BLOCK // 2), F32),
                        pltpu.VMEM((2, s, w), F32)],
        compiler_params=_params("parallel", "parallel"),
        name="moba",
    )(proj, proj, proj)


def _sgu_kernel(u_ref, z_ref, lg_ref, lb_ref, w_ref, bias_ref, o_ref):
    ts, width = u_ref.shape[1], u_ref.shape[2]
    c = SGU_CHUNK
    gd = width // SGU_GROUPS
    row = lax.broadcasted_iota(jnp.int32, (c, c), 0)
    col = lax.broadcasted_iota(jnp.int32, (c, c), 1)
    for g in range(SGU_GROUPS):
        gs = slice(g * gd, (g + 1) * gd)
        z = _gelu(z_ref[0, :, gs])
        mu = jnp.mean(z, axis=-1, keepdims=True)
        d = z - mu
        zn = (d * lax.rsqrt(jnp.mean(d * d, axis=-1, keepdims=True) + EPS) * lg_ref[:, gs]
              + lb_ref[:, gs]).astype(BF16)
        wg = jnp.where(col <= row, w_ref[g], 0.0).astype(BF16)
        for n in range(ts // c):
            ts_ = slice(n * c, (n + 1) * c)
            mixed = _dot(wg, zn[ts_, :]) + bias_ref[:, gs]
            o_ref[0, ts_, gs] = (_gelu(u_ref[0, ts_, gs]) * mixed).astype(o_ref.dtype)


def _sgu(proj, col_u, col_z, ln_g, ln_b, w, bias):
    b, s, _ = proj.shape
    width = ln_g.shape[0]
    groups, c, _ = w.shape
    ts = SGU_ROWS
    bias_full = jnp.repeat(bias.T, width // groups, axis=1)
    return pl.pallas_call(
        _sgu_kernel,
        grid=(b, s // ts),
        in_specs=[pl.BlockSpec((1, ts, width), lambda i, j: (i, j, col_u)),
                  pl.BlockSpec((1, ts, width), lambda i, j: (i, j, col_z)),
                  pl.BlockSpec((1, width), lambda i, j: (0, 0)),
                  pl.BlockSpec((1, width), lambda i, j: (0, 0)),
                  pl.BlockSpec((groups, c, c), lambda i, j: (0, 0, 0)),
                  pl.BlockSpec((c, width), lambda i, j: (0, 0))],
        out_specs=pl.BlockSpec((1, ts, width), lambda i, j: (i, j, 0)),
        out_shape=jax.ShapeDtypeStruct((b, s, width), BF16),
        compiler_params=_params("parallel", "parallel"),
        name="sgu",
    )(proj, proj, ln_g.reshape(1, width), ln_b.reshape(1, width), w, bias_full)


def _cross_attention(x, wq_ref, kv_ref, wo_ref):
    d = x.shape[1]
    hd = d // XA_HEADS
    q = (_dot(x.astype(BF16), wq_ref[...]) * (_row_scale(x) * (hd ** -0.5 * LOG2E))).astype(BF16)

    def logits(h):
        hs = slice(h * hd, (h + 1) * hd)
        s = lax.dot_general(q[:, hs], kv_ref[0, :, hs], _NT, preferred_element_type=F32)
        return s, jnp.max(s, axis=-1, keepdims=True)

    pending = logits(0)
    outs = []
    for h in range(XA_HEADS):
        ahead = logits(h + 1) if h + 1 < XA_HEADS else None
        s, m = pending
        p = jnp.exp2(s - m)
        o = _dot(p.astype(BF16), kv_ref[0, :, d + h * hd:d + (h + 1) * hd])
        outs.append((o * (1.0 / jnp.sum(p, axis=-1, keepdims=True))).astype(BF16))
        pending = ahead
    return x + _dot(jnp.concatenate(outs, axis=-1), wo_ref[...])


def _swiglu(x, w1_ref, w2_ref):
    hidden = w2_ref.shape[0]
    r = _row_scale(x)
    h = x.astype(BF16)
    acc = x
    for c in range(hidden // FFN_CHUNK):
        cs = slice(c * FFN_CHUNK, (c + 1) * FFN_CHUNK)
        gs = slice(hidden + c * FFN_CHUNK, hidden + (c + 1) * FFN_CHUNK)
        u = (_silu(_dot(h, w1_ref[:, cs]) * r) * (_dot(h, w1_ref[:, gs]) * r)).astype(BF16)
        acc = acc + _dot(u, w2_ref[cs, :])
    return acc


def _post_mixer_kernel(x_ref, a_ref, b_ref, kv_ref, fg_ref, gx_c, gf_c,
                       wmix_c, wq_c, wo_c, w1_c, w2_c, o_ref,
                       wmix_b, wq_b, wo_b, w1_b, w2_b, *, final):
    i = pl.program_id(0)

    @pl.when(i < POST_CAST_STEPS)
    def _():
        for src, dst, gain in ((wmix_c, wmix_b, None), (wq_c, wq_b, gx_c), (wo_c, wo_b, None),
                               (w1_c, w1_b, gf_c), (w2_c, w2_b, None)):
            rows = src.shape[0]
            chunk = src[...] if gain is None else src[...] * gain[...]
            dst[pl.ds(pl.multiple_of(i * rows, rows), rows), :] = chunk.astype(BF16)

    @pl.when(i >= POST_CAST_STEPS)
    def _():
        ka = a_ref.shape[1]
        x = x_ref[...] + _dot(a_ref[...], wmix_b[0:ka, :]) + _dot(b_ref[...], wmix_b[ka:, :])
        x = _cross_attention(x, wq_b, kv_ref, wo_b)
        x = _swiglu(x, w1_b, w2_b)
        o_ref[...] = _rms(x, fg_ref[...]) if final else x


def _post_mixer(x2d, a2d, b2d, w_mix, mix_layer, gx, wq, kv, wo, gf, w1, w2, layer, final_gain, final, seq):
    t, d = x2d.shape
    ka, kb = a2d.shape[1], b2d.shape[1]
    m = kv.shape[1]
    nc = POST_CAST_STEPS
    per_batch = seq // ROW_TILE
    tile = lambda i: jnp.maximum(i - nc, 0)
    row = lambda n: pl.BlockSpec((ROW_TILE, n), lambda i: (tile(i), 0))
    vec = lambda: pl.BlockSpec((1, d), lambda i: (0, 0))

    def chunk(w, idx):
        return pl.BlockSpec((None, w.shape[1] // nc, w.shape[2]), lambda i: (idx, jnp.minimum(i, nc - 1), 0))

    def gain():
        return pl.BlockSpec((d // nc, 1), lambda i: (jnp.minimum(i, nc - 1), 0))

    weights = ((w_mix, mix_layer), (wq, layer), (wo, layer), (w1, layer), (w2, layer))
    return pl.pallas_call(
        functools.partial(_post_mixer_kernel, final=final),
        grid=(nc + t // ROW_TILE,),
        in_specs=[row(d), row(ka), row(kb),
                  pl.BlockSpec((1, m, 2 * d), lambda i: (tile(i) // per_batch, 0, 0)), vec(), gain(), gain()]
                 + [chunk(w, idx) for w, idx in weights],
        out_specs=row(d),
        out_shape=jax.ShapeDtypeStruct((t, d), F32),
        scratch_shapes=[pltpu.VMEM(w.shape[1:], BF16) for w, _ in weights],
        compiler_params=_params("arbitrary"),
        name="post_mixer",
    )(x2d, a2d, b2d, kv, final_gain.reshape(1, d), gx.reshape(d, 1), gf.reshape(d, 1),
      *[w for w, _ in weights])


def kernel(x, mem, norm_mix, norm_xattn, norm_ffn, mem_norm, final_norm, w_in_ab, w_out_ab, hgrn_lower_bounds, hgrn_out_norm, conv_dw_w, conv_dw_b, conv_ln_g, conv_ln_b, w_in_cd, w_out_cd, sgu_ln_g, sgu_ln_b, sgu_w, sgu_b, xa_wq, xa_wkv, xa_wo, ffn_w_in, ffn_w_out):
    b, s, d = x.shape
    m = mem.shape[1]
    depth = norm_mix.shape[0]
    x2d = x.reshape(b * s, d)
    mem2d = mem.reshape(b * m, d)
    for l in range(depth):
        if l % 2 == 0:
            e = l // 2
            a_width = hgrn_out_norm.shape[1]
            b_width = conv_dw_w.shape[2]
            proj = _norm_matmul(x2d, norm_mix[l], w_in_ab, e, F32).reshape(b, s, -1)
            o_a = _hgrn(proj, hgrn_lower_bounds, hgrn_out_norm[e], l, a_width)
            col = 4 * a_width // b_width
            o_b = _conv_module(proj, col, col + 1, conv_dw_w[e], conv_dw_b[e], conv_ln_g[e], conv_ln_b[e])
            w_out, mix_layer = w_out_ab, e
        else:
            o = l // 2
            d_width = sgu_ln_g.shape[1]
            c_width = w_out_cd.shape[1] - d_width
            proj = _norm_matmul(x2d, norm_mix[l], w_in_cd, o, F32).reshape(b, s, -1)
            o_a = _moba(proj, c_width)
            col = 3 * c_width // d_width
            o_b = _sgu(proj, col, col + 1, sgu_ln_g[o], sgu_ln_b[o], sgu_w[o], sgu_b[o])
            w_out, mix_layer = w_out_cd, o
        kv = _norm_matmul(mem2d, mem_norm, xa_wkv, l, BF16).reshape(b, m, 2 * d)
        x2d = _post_mixer(x2d, o_a.reshape(b * s, -1), o_b.reshape(b * s, -1), w_out, mix_layer,
                          norm_xattn[l], xa_wq, kv, xa_wo, norm_ffn[l], ffn_w_in, ffn_w_out, l,
                          final_norm, l == depth - 1, s)
    return x2d.reshape(b, s, d)
```

```python
import functools

import jax
import jax.numpy as jnp
from jax import lax
from jax.experimental import pallas as pl
from jax.experimental.pallas import tpu as pltpu

F32 = jnp.float32
BF16 = jnp.bfloat16
EPS = 1e-6
NEG = -1e30
LOG2E = 1.4426950408889634

V7X_VMEM_BYTES = 64 * 1024 * 1024
VMEM_LIMIT = V7X_VMEM_BYTES - 8 * 1024 * 1024
SUBLANES = 8
LANES = 128

HGRN_HEAD_DIM = 128
HGRN_CHUNK = 64
HGRN_HEADS_PER_STEP = 4
HGRN_GROUP = 4
CONV_TILE = 2048
CONV_NORM_ROWS = 2048
MOBA_HEAD_DIM = 64
MOBA_BLOCK = 256
MOBA_TOPK = 3
SGU_CHUNK = 128
SGU_GROUPS = 4
SGU_ROWS = 2048
XA_HEADS = 4
ROW_TILE = 512
FFN_CHUNK = 256
POST_CAST_STEPS = 16

_NT = (((1,), (1,)), ((), ()))
_TN = (((0,), (0,)), ((), ()))


def _params(*sem):
    return pltpu.CompilerParams(dimension_semantics=sem, vmem_limit_bytes=VMEM_LIMIT)


def _row_scale(x):
    return lax.rsqrt(jnp.mean(x * x, axis=-1, keepdims=True) + EPS)


def _rms(x, g):
    return x * _row_scale(x) * g


def _sigmoid(x):
    return 1.0 / (1.0 + jnp.exp2(x * -LOG2E))


def _silu(x):
    return x * _sigmoid(x)


def _gelu(x):
    return 0.5 * x * (1.0 + lax.erf(x * (2.0 ** -0.5)))


def _dot(a, b):
    return jnp.dot(a, b, preferred_element_type=F32)


def _dot_nt3(a, b):
    a_hi, b_hi = a.astype(BF16), b.astype(BF16)
    a_lo = (a - a_hi.astype(F32)).astype(BF16)
    b_lo = (b - b_hi.astype(F32)).astype(BF16)
    nt = lambda u, v: lax.dot_general(u, v, _NT, preferred_element_type=F32)
    return nt(a_hi, b_hi) + nt(a_hi, b_lo) + nt(a_lo, b_hi)


def _dot01_f32(m01, x):
    hi = x.astype(BF16)
    lo = (x - hi.astype(F32)).astype(BF16)
    return _dot(m01, hi) + _dot(m01, lo)


def _resident(stacked, layer):
    return pl.BlockSpec((None,) + stacked.shape[1:], lambda i: (layer, 0, 0), pipeline_mode=pl.Buffered(1))


def _norm_matmul_kernel(x_ref, g_ref, w_ref, o_ref, wb_ref):
    @pl.when(pl.program_id(0) == 0)
    def _():
        wb_ref[...] = (w_ref[...] * g_ref[...]).astype(BF16)

    x = x_ref[...]
    o_ref[...] = (_dot(x.astype(BF16), wb_ref[...]) * _row_scale(x)).astype(o_ref.dtype)


def _norm_matmul(x2d, g, w_stacked, layer, out_dtype):
    t, d = x2d.shape
    n = w_stacked.shape[2]
    return pl.pallas_call(
        _norm_matmul_kernel,
        grid=(t // ROW_TILE,),
        in_specs=[pl.BlockSpec((ROW_TILE, d), lambda i: (i, 0)),
                  pl.BlockSpec((d, 1), lambda i: (0, 0)),
                  _resident(w_stacked, layer)],
        out_specs=pl.BlockSpec((ROW_TILE, n), lambda i: (i, 0)),
        out_shape=jax.ShapeDtypeStruct((t, n), out_dtype),
        scratch_shapes=[pltpu.VMEM((d, n), BF16)],
        compiler_params=_params("arbitrary"),
        name="norm_matmul",
    )(x2d, g.reshape(d, 1), w_stacked)


def _hgrn_kernel(q_ref, f_ref, i_ref, g_ref, lbz_ref, on_ref, o_ref, *, layer):
    seq, dk = q_ref.shape[1], HGRN_HEAD_DIM
    heads_here = q_ref.shape[2] // dk
    c, grp = HGRN_CHUNK, HGRN_GROUP
    rows = c * grp
    lbz = lbz_ref[...]
    e = jnp.exp(lbz - jnp.max(lbz, axis=0, keepdims=True))
    lb_all = jnp.sum(e[:layer + 1], axis=0, keepdims=True) / jnp.sum(e, axis=0, keepdims=True)
    on_all = on_ref[...]
    row = lax.broadcasted_iota(jnp.int32, (rows, rows), 0)
    col = lax.broadcasted_iota(jnp.int32, (rows, rows), 1)
    causal = (col <= row) & (col >= (row // c) * c)
    tril = causal.astype(BF16)

    def group(hh, n):
        sl = slice(n * rows, (n + 1) * rows)
        ls = slice(hh * dk, (hh + 1) * dk)
        lb = lb_all[:, ls]
        f = lb + (1.0 - lb) * _sigmoid(f_ref[0, sl, ls])
        cum = _dot01_f32(tril, jnp.log(f))
        yield
        cum = cum * LOG2E
        k = 1.0 - f
        q_in = (_silu(q_ref[0, sl, ls]) * jnp.exp2(cum)).astype(BF16)
        k_in = (k * jnp.exp2(-cum)).astype(BF16)
        att = lax.dot_general(q_in, k_in, _NT, preferred_element_type=F32)
        vb = i_ref[0, sl, ls].astype(BF16)
        cum3 = cum.reshape(grp, c, dk)
        cl = cum3[:, c - 1:c, :]
        kdec = (k.reshape(grp, c, dk) * jnp.exp2(cl - cum3)).astype(BF16)
        decay = jnp.exp2(cl)
        kv = [lax.dot_general(vb[j * c:(j + 1) * c], kdec[j], _TN, preferred_element_type=F32)
              for j in range(grp)]
        yield
        att = jnp.where(causal, att, 0.0).astype(BF16)
        yield q_in, decay, kv, _dot(att, vb)

    order = [(hh, n) for n in range(seq // rows) for hh in range(heads_here)]
    gens = [group(hh, n) for hh, n in order]
    for _ in range(2):
        for gen in gens:
            next(gen)
    states = [jnp.zeros((dk, dk), F32) for _ in range(heads_here)]
    for (hh, n), (q_in, decay, kv, o_intra) in zip(order, [next(gen) for gen in gens]):
        sl = slice(n * rows, (n + 1) * rows)
        ls = slice(hh * dk, (hh + 1) * dk)
        st = states[hh]
        o_inter = []
        for j in range(grp):
            o_inter.append(lax.dot_general(q_in[j * c:(j + 1) * c], st.astype(BF16), _NT,
                                           preferred_element_type=F32))
            st = decay[j] * st + kv[j]
        states[hh] = st
        o = o_intra + jnp.concatenate(o_inter, axis=0)
        o = o * lax.rsqrt(jnp.mean(o * o, axis=-1, keepdims=True) + EPS)
        o_ref[0, sl, ls] = (o * on_all[:, ls] * _silu(g_ref[0, sl, ls])).astype(o_ref.dtype)


def _hgrn(proj, lbz, out_norm, layer, width):
    b, s, _ = proj.shape
    hd = HGRN_HEAD_DIM * HGRN_HEADS_PER_STEP
    heads = width // hd
    sec = lambda k: pl.BlockSpec((1, s, hd), lambda i, h: (i, 0, k * heads + h))
    return pl.pallas_call(
        functools.partial(_hgrn_kernel, layer=layer),
        grid=(b, heads),
        in_specs=[sec(0), sec(1), sec(2), sec(3),
                  pl.BlockSpec((lbz.shape[0], hd), lambda i, h: (0, h)),
                  pl.BlockSpec((1, hd), lambda i, h: (0, h))],
        out_specs=pl.BlockSpec((1, s, hd), lambda i, h: (i, 0, h)),
        out_shape=jax.ShapeDtypeStruct((b, s, width), BF16),
        compiler_params=_params("parallel", "parallel"),
        name="hgrn2",
    )(proj, proj, proj, proj, lbz, out_norm.reshape(1, width))


def _conv_kernel(a_ref, b_ref, w_ref, db_ref, lg_ref, lb_ref, o_ref, cpad_ref, acc_ref):
    seq, ch = a_ref.shape[1], a_ref.shape[2]
    kw = w_ref.shape[0]
    pad = cpad_ref.shape[0] - seq
    tt = CONV_TILE
    cpad_ref[0:pad, :] = jnp.zeros((pad, ch), F32)

    def fill(t, carry):
        sl = pl.ds(pl.multiple_of(t * tt, tt), tt)
        cpad_ref[pl.ds(pl.multiple_of(pad + t * tt, SUBLANES), tt), :] = (
            a_ref[0, sl, :] * _sigmoid(b_ref[0, sl, :]))
        return carry

    lax.fori_loop(0, seq // tt, fill, 0)
    db, lg, lb = db_ref[...], lg_ref[...], lb_ref[...]

    offs = [pad - kw + 1 + k for k in range(kw)]
    sup = acc_ref.shape[0]
    lane_blocks = ch // LANES

    def conv(s, carry):
        sbase = pl.multiple_of(s * sup, sup)

        def block(i, c2):
            t, j = i // lane_blocks, i % lane_blocks
            base = pl.multiple_of(sbase + t * tt, tt)
            ls = pl.ds(pl.multiple_of(j * LANES, LANES), LANES)
            win = cpad_ref[pl.ds(base, tt + pad), ls]
            acc = jnp.zeros((tt, LANES), F32)
            for r in range(SUBLANES):
                taps = [k for k in range(kw) if offs[k] % SUBLANES == r]
                if not taps:
                    continue
                shifted = pltpu.roll(win, tt + pad - r, axis=0) if r else win
                for k in taps:
                    acc = acc + w_ref[k:k + 1, ls] * shifted[offs[k] - r:offs[k] - r + tt, :]
            acc_ref[pl.ds(pl.multiple_of(t * tt, tt), tt), ls] = acc
            return c2

        lax.fori_loop(0, (sup // tt) * lane_blocks, block, 0)
        acc = acc_ref[...] + db
        mu = jnp.mean(acc, axis=-1, keepdims=True)
        d = acc - mu
        y = d * lax.rsqrt(jnp.mean(d * d, axis=-1, keepdims=True) + EPS) * lg + lb
        o_ref[0, pl.ds(sbase, sup), :] = _silu(y).astype(o_ref.dtype)
        return carry

    lax.fori_loop(0, seq // sup, conv, 0)


def _conv_module(proj, col_a, col_b, dw_w, dw_b, ln_g, ln_b):
    b, s, _ = proj.shape
    kw, ch = dw_w.shape
    pad = -(-(kw - 1) // SUBLANES) * SUBLANES
    vec = lambda: pl.BlockSpec((1, ch), lambda i: (0, 0))
    return pl.pallas_call(
        _conv_kernel,
        grid=(b,),
        in_specs=[pl.BlockSpec((1, s, ch), lambda i: (i, 0, col_a)),
                  pl.BlockSpec((1, s, ch), lambda i: (i, 0, col_b)),
                  pl.BlockSpec((kw, ch), lambda i: (0, 0)), vec(), vec(), vec()],
        out_specs=pl.BlockSpec((1, s, ch), lambda i: (i, 0, 0)),
        out_shape=jax.ShapeDtypeStruct((b, s, ch), BF16),
        scratch_shapes=[pltpu.VMEM((pad + s, ch), F32), pltpu.VMEM((CONV_NORM_ROWS, ch), F32)],
        compiler_params=_params("parallel"),
        name="conv_module",
    )(proj, proj, dw_w, dw_b.reshape(1, ch), ln_g.reshape(1, ch), ln_b.reshape(1, ch))


def _moba_kernel(q_ref, k_ref, v_ref, o_ref, qaug_ref, kaug_ref, vaug_ref, s_ref, p_ref, m_ref, acc_ref):
    seq, w = q_ref.shape[1], q_ref.shape[2]
    hd, blk, topk = MOBA_HEAD_DIM, MOBA_BLOCK, MOBA_TOPK
    nblk = seq // blk
    half = blk // 2
    scale = hd ** -0.5
    k = k_ref[0]
    v = v_ref[0]
    kmean = jnp.mean(k.reshape(nblk, blk, w), axis=1)
    lane_k = lax.broadcasted_iota(jnp.int32, (seq, w), 1)
    blk_k = lax.broadcasted_iota(jnp.int32, (seq, w), 0) // blk
    for e in range(2):
        in_head = (lane_k >= e * hd) & (lane_k < (e + 1) * hd)
        onehot = (lane_k - (1 - e) * hd == blk_k).astype(F32)
        kaug_ref[e] = jnp.where(in_head, k, onehot).astype(BF16)
        vaug_ref[e] = jnp.where(in_head, v, 1.0).astype(BF16)

    lane_q = lax.broadcasted_iota(jnp.int32, (blk, w), 1)
    lane_m = lax.broadcasted_iota(jnp.int32, (nblk, w), 1)
    ridx = lax.broadcasted_iota(jnp.int32, (nblk, blk), 0)
    own_causal = (lax.broadcasted_iota(jnp.int32, (blk, blk), 1)
                  <= lax.broadcasted_iota(jnp.int32, (blk, blk), 0))

    for jq in range(nblk):
        qs = slice(jq * blk, (jq + 1) * blk)
        q2 = q_ref[0, qs, :]
        for e in range(2):
            lo = (1 - e) * hd
            if jq > topk:
                kme = jnp.where((lane_m >= e * hd) & (lane_m < (e + 1) * hd), kmean, 0.0)
                st = _dot_nt3(kme, q2)
                valid = ridx < jq
                rows = []
                for n in range(nblk):
                    if n < jq:
                        sn = st[n:n + 1, :]
                        beats = valid & ((st > sn) | ((st == sn) & (ridx < n)))
                        rank = jnp.sum(beats.astype(F32), axis=0, keepdims=True)
                        rows.append(jnp.where(rank < topk, 0.0, NEG))
                    else:
                        rows.append(jnp.zeros((1, blk), F32))
                pieces = [jnp.concatenate(rows, axis=0)]
                if lo:
                    pieces.insert(0, jnp.zeros((lo, blk), F32))
                if w - lo - nblk:
                    pieces.append(jnp.zeros((w - lo - nblk, blk), F32))
                bias_q = jnp.concatenate(pieces, axis=0).T
            else:
                bias_q = jnp.zeros((blk, w), F32)
            in_head = (lane_q >= e * hd) & (lane_q < (e + 1) * hd)
            qaug_ref[e, qs, :] = jnp.where(in_head, q2 * (scale * LOG2E), bias_q).astype(BF16)

    base = [sum(seq - i * blk for i in range(n)) for n in range(nblk)]

    def fold_max(e, rows, s, first):
        t = jnp.maximum(s[:, :half], s[:, half:])
        m_ref[e, rows, :] = t if first else jnp.maximum(m_ref[e, rows, :], t)

    def score(e, n):
        s = lax.dot_general(qaug_ref[e, n * blk:, :], kaug_ref[e, n * blk:(n + 1) * blk, :], _NT,
                            preferred_element_type=F32)
        own = jnp.where(own_causal, s[:blk], NEG)
        s_ref[e, base[n]:base[n] + blk, :] = own
        fold_max(e, slice(n * blk, (n + 1) * blk), own, n == 0)
        if n + 1 < nblk:
            s_ref[e, base[n] + blk:base[n] + seq - n * blk, :] = s[blk:]
            fold_max(e, slice((n + 1) * blk, seq), s[blk:], n == 0)

    def row_max(e):
        m_ref[e] = jnp.broadcast_to(jnp.max(m_ref[e], axis=-1, keepdims=True), (seq, half))

    def weights(e, n):
        rows = slice(base[n], base[n] + seq - n * blk)
        s = s_ref[e, rows, :]
        mb = m_ref[e, n * blk:, :]
        p_ref[e, rows, :half] = jnp.exp2(s[:, :half] - mb).astype(BF16)
        p_ref[e, rows, half:] = jnp.exp2(s[:, half:] - mb).astype(BF16)

    def values(e, n):
        rows = slice(base[n], base[n] + seq - n * blk)
        o = _dot(p_ref[e, rows, :], vaug_ref[e, n * blk:(n + 1) * blk, :])
        if n == 0:
            acc_ref[e] = o
        else:
            acc_ref[e, n * blk:, :] += o

    for n in range(nblk):
        score(0, n)
    row_max(0)
    for n in range(nblk):
        score(1, n)
        weights(0, n)
    row_max(1)
    for n in range(nblk):
        values(0, n)
        weights(1, n)
    for n in range(nblk):
        values(1, n)
    outs = []
    for e in range(2):
        lo = (1 - e) * hd
        o = acc_ref[e]
        outs.append(o * (1.0 / o[:, lo:lo + 1]))
    lane_o = lax.broadcasted_iota(jnp.int32, (seq, w), 1)
    o_ref[0] = jnp.where(lane_o < hd, outs[0], outs[1]).astype(o_ref.dtype)


def _moba(proj, width):
    b, s, _ = proj.shape
    w = 2 * MOBA_HEAD_DIM
    pairs = width // w
    nblk = s // MOBA_BLOCK
    tiles = nblk * (nblk + 1) // 2
    sec = lambda k: pl.BlockSpec((1, s, w), lambda i, h: (i, 0, k * pairs + h))
    return pl.pallas_call(
        _moba_kernel,
        grid=(b, pairs),
        in_specs=[sec(0), sec(1), sec(2)],
        out_specs=pl.BlockSpec((1, s, w), lambda i, h: (i, 0, h)),
        out_shape=jax.ShapeDtypeStruct((b, s, width), BF16),
        scratch_shapes=[pltpu.VMEM((2, s, w), BF16), pltpu.VMEM((2, s, w), BF16), pltpu.VMEM((2, s, w), BF16),
                        pltpu.VMEM((2, tiles * MOBA_BLOCK, MOBA_BLOCK), F32),
                        pltpu.VMEM((2, tiles * MOBA_BLOCK, MOBA_BLOCK), BF16),
                        pltpu.VMEM((2, s, MOBA_BLOCK // 2), F32),
                        pltpu.VMEM((2, s, w), F32)],
        compiler_params=_params("parallel", "parallel"),
        name="moba",
    )(proj, proj, proj)


def _sgu_kernel(u_ref, z_ref, lg_ref, lb_ref, w_ref, bias_ref, o_ref):
    ts, width = u_ref.shape[1], u_ref.shape[2]
    c = SGU_CHUNK
    gd = width // SGU_GROUPS
    row = lax.broadcasted_iota(jnp.int32, (c, c), 0)
    col = lax.broadcasted_iota(jnp.int32, (c, c), 1)
    for g in range(SGU_GROUPS):
        gs = slice(g * gd, (g + 1) * gd)
        z = _gelu(z_ref[0, :, gs])
        mu = jnp.mean(z, axis=-1, keepdims=True)
        d = z - mu
        zn = (d * lax.rsqrt(jnp.mean(d * d, axis=-1, keepdims=True) + EPS) * lg_ref[:, gs]
              + lb_ref[:, gs]).astype(BF16)
        wg = jnp.where(col <= row, w_ref[g], 0.0).astype(BF16)
        for n in range(ts // c):
            ts_ = slice(n * c, (n + 1) * c)
            mixed = _dot(wg, zn[ts_, :]) + bias_ref[:, gs]
            o_ref[0, ts_, gs] = (_gelu(u_ref[0, ts_, gs]) * mixed).astype(o_ref.dtype)


def _sgu(proj, col_u, col_z, ln_g, ln_b, w, bias):
    b, s, _ = proj.shape
    width = ln_g.shape[0]
    groups, c, _ = w.shape
    ts = SGU_ROWS
    bias_full = jnp.repeat(bias.T, width // groups, axis=1)
    return pl.pallas_call(
        _sgu_kernel,
        grid=(b, s // ts),
        in_specs=[pl.BlockSpec((1, ts, width), lambda i, j: (i, j, col_u)),
                  pl.BlockSpec((1, ts, width), lambda i, j: (i, j, col_z)),
                  pl.BlockSpec((1, width), lambda i, j: (0, 0)),
                  pl.BlockSpec((1, width), lambda i, j: (0, 0)),
                  pl.BlockSpec((groups, c, c), lambda i, j: (0, 0, 0)),
                  pl.BlockSpec((c, width), lambda i, j: (0, 0))],
        out_specs=pl.BlockSpec((1, ts, width), lambda i, j: (i, j, 0)),
        out_shape=jax.ShapeDtypeStruct((b, s, width), BF16),
        compiler_params=_params("parallel", "parallel"),
        name="sgu",
    )(proj, proj, ln_g.reshape(1, width), ln_b.reshape(1, width), w, bias_full)


def _cross_attention(x, wq_ref, kv_ref, wo_ref):
    d = x.shape[1]
    hd = d // XA_HEADS
    q = (_dot(x.astype(BF16), wq_ref[...]) * (_row_scale(x) * (hd ** -0.5 * LOG2E))).astype(BF16)

    def logits(h):
        hs = slice(h * hd, (h + 1) * hd)
        s = lax.dot_general(q[:, hs], kv_ref[0, :, hs], _NT, preferred_element_type=F32)
        return s, jnp.max(s, axis=-1, keepdims=True)

    pending = logits(0)
    outs = []
    for h in range(XA_HEADS):
        ahead = logits(h + 1) if h + 1 < XA_HEADS else None
        s, m = pending
        p = jnp.exp2(s - m)
        o = _dot(p.astype(BF16), kv_ref[0, :, d + h * hd:d + (h + 1) * hd])
        outs.append((o * (1.0 / jnp.sum(p, axis=-1, keepdims=True))).astype(BF16))
        pending = ahead
    return x + _dot(jnp.concatenate(outs, axis=-1), wo_ref[...])


def _swiglu(x, w1_ref, w2_ref):
    hidden = w2_ref.shape[0]
    r = _row_scale(x)
    h = x.astype(BF16)
    acc = x
    for c in range(hidden // FFN_CHUNK):
        cs = slice(c * FFN_CHUNK, (c + 1) * FFN_CHUNK)
        gs = slice(hidden + c * FFN_CHUNK, hidden + (c + 1) * FFN_CHUNK)
        u = (_silu(_dot(h, w1_ref[:, cs]) * r) * (_dot(h, w1_ref[:, gs]) * r)).astype(BF16)
        acc = acc + _dot(u, w2_ref[cs, :])
    return acc


def _post_mixer_kernel(x_ref, a_ref, b_ref, kv_ref, fg_ref, gx_c, gf_c,
                       wmix_c, wq_c, wo_c, w1_c, w2_c, o_ref,
                       wmix_b, wq_b, wo_b, w1_b, w2_b, *, final):
    i = pl.program_id(0)

    @pl.when(i < POST_CAST_STEPS)
    def _():
        for src, dst, gain in ((wmix_c, wmix_b, None), (wq_c, wq_b, gx_c), (wo_c, wo_b, None),
                               (w1_c, w1_b, gf_c), (w2_c, w2_b, None)):
            rows = src.shape[0]
            chunk = src[...] if gain is None else src[...] * gain[...]
            dst[pl.ds(pl.multiple_of(i * rows, rows), rows), :] = chunk.astype(BF16)

    @pl.when(i >= POST_CAST_STEPS)
    def _():
        ka = a_ref.shape[1]
        x = x_ref[...] + _dot(a_ref[...], wmix_b[0:ka, :]) + _dot(b_ref[...], wmix_b[ka:, :])
        x = _cross_attention(x, wq_b, kv_ref, wo_b)
        x = _swiglu(x, w1_b, w2_b)
        o_ref[...] = _rms(x, fg_ref[...]) if final else x


def _post_mixer(x2d, a2d, b2d, w_mix, mix_layer, gx, wq, kv, wo, gf, w1, w2, layer, final_gain, final, seq):
    t, d = x2d.shape
    ka, kb = a2d.shape[1], b2d.shape[1]
    m = kv.shape[1]
    nc = POST_CAST_STEPS
    per_batch = seq // ROW_TILE
    tile = lambda i: jnp.maximum(i - nc, 0)
    row = lambda n: pl.BlockSpec((ROW_TILE, n), lambda i: (tile(i), 0))
    vec = lambda: pl.BlockSpec((1, d), lambda i: (0, 0))

    def chunk(w, idx):
        return pl.BlockSpec((None, w.shape[1] // nc, w.shape[2]), lambda i: (idx, jnp.minimum(i, nc - 1), 0))

    def gain():
        return pl.BlockSpec((d // nc, 1), lambda i: (jnp.minimum(i, nc - 1), 0))

    weights = ((w_mix, mix_layer), (wq, layer), (wo, layer), (w1, layer), (w2, layer))
    return pl.pallas_call(
        functools.partial(_post_mixer_kernel, final=final),
        grid=(nc + t // ROW_TILE,),
        in_specs=[row(d), row(ka), row(kb),
                  pl.BlockSpec((1, m, 2 * d), lambda i: (tile(i) // per_batch, 0, 0)), vec(), gain(), gain()]
                 + [chunk(w, idx) for w, idx in weights],
        out_specs=row(d),
        out_shape=jax.ShapeDtypeStruct((t, d), F32),
        scratch_shapes=[pltpu.VMEM(w.shape[1:], BF16) for w, _ in weights],
        compiler_params=_params("arbitrary"),
        name="post_mixer",
    )(x2d, a2d, b2d, kv, final_gain.reshape(1, d), gx.reshape(d, 1), gf.reshape(d, 1),
      *[w for w, _ in weights])


def kernel(x, mem, norm_mix, norm_xattn, norm_ffn, mem_norm, final_norm, w_in_ab, w_out_ab, hgrn_lower_bounds, hgrn_out_norm, conv_dw_w, conv_dw_b, conv_ln_g, conv_ln_b, w_in_cd, w_out_cd, sgu_ln_g, sgu_ln_b, sgu_w, sgu_b, xa_wq, xa_wkv, xa_wo, ffn_w_in, ffn_w_out):
    b, s, d = x.shape
    m = mem.shape[1]
    depth = norm_mix.shape[0]
    x2d = x.reshape(b * s, d)
    mem2d = mem.reshape(b * m, d)
    for l in range(depth):
        if l % 2 == 0:
            e = l // 2
            a_width = hgrn_out_norm.shape[1]
            b_width = conv_dw_w.shape[2]
            proj = _norm_matmul(x2d, norm_mix[l], w_in_ab, e, F32).reshape(b, s, -1)
            o_a = _hgrn(proj, hgrn_lower_bounds, hgrn_out_norm[e], l, a_width)
            col = 4 * a_width // b_width
            o_b = _conv_module(proj, col, col + 1, conv_dw_w[e], conv_dw_b[e], conv_ln_g[e], conv_ln_b[e])
            w_out, mix_layer = w_out_ab, e
        else:
            o = l // 2
            d_width = sgu_ln_g.shape[1]
            c_width = w_out_cd.shape[1] - d_width
            proj = _norm_matmul(x2d, norm_mix[l], w_in_cd, o, F32).reshape(b, s, -1)
            o_a = _moba(proj, c_width)
            col = 3 * c_width // d_width
            o_b = _sgu(proj, col, col + 1, sgu_ln_g[o], sgu_ln_b[o], sgu_w[o], sgu_b[o])
            w_out, mix_layer = w_out_cd, o
        kv = _norm_matmul(mem2d, mem_norm, xa_wkv, l, BF16).reshape(b, m, 2 * d)
        x2d = _post_mixer(x2d, o_a.reshape(b * s, -1), o_b.reshape(b * s, -1), w_out, mix_layer,
                          norm_xattn[l], xa_wq, kv, xa_wo, norm_ffn[l], ffn_w_in, ffn_w_out, l,
                          final_norm, l == depth - 1, s)
    return x2d.reshape(b, s, d)
```

```python
import functools

import jax
import jax.numpy as jnp
from jax import lax
from jax.experimental import pallas as pl
from jax.experimental.pallas import tpu as pltpu

F32 = jnp.float32
BF16 = jnp.bfloat16
EPS = 1e-6
NEG = -1e30
LOG2E = 1.4426950408889634

V7X_VMEM_BYTES = 64 * 1024 * 1024
VMEM_LIMIT = V7X_VMEM_BYTES - 8 * 1024 * 1024
SUBLANES = 8
LANES = 128

HGRN_HEAD_DIM = 128
HGRN_CHUNK = 64
HGRN_HEADS_PER_STEP = 4
HGRN_GROUP = 4
CONV_TILE = 2048
CONV_NORM_ROWS = 2048
MOBA_HEAD_DIM = 64
MOBA_BLOCK = 256
MOBA_TOPK = 3
SGU_CHUNK = 128
SGU_GROUPS = 4
SGU_ROWS = 2048
XA_HEADS = 4
ROW_TILE = 512
FFN_CHUNK = 256
POST_CAST_STEPS = 16

_NT = (((1,), (1,)), ((), ()))
_TN = (((0,), (0,)), ((), ()))


def _params(*sem):
    return pltpu.CompilerParams(dimension_semantics=sem, vmem_limit_bytes=VMEM_LIMIT)


def _row_scale(x):
    return lax.rsqrt(jnp.mean(x * x, axis=-1, keepdims=True) + EPS)


def _rms(x, g):
    return x * _row_scale(x) * g


def _sigmoid(x):
    return 1.0 / (1.0 + jnp.exp2(x * -LOG2E))


def _silu(x):
    return x * _sigmoid(x)


def _gelu(x):
    return 0.5 * x * (1.0 + lax.erf(x * (2.0 ** -0.5)))


def _dot(a, b):
    return jnp.dot(a, b, preferred_element_type=F32)


def _dot_nt3(a, b):
    a_hi, b_hi = a.astype(BF16), b.astype(BF16)
    a_lo = (a - a_hi.astype(F32)).astype(BF16)
    b_lo = (b - b_hi.astype(F32)).astype(BF16)
    nt = lambda u, v: lax.dot_general(u, v, _NT, preferred_element_type=F32)
    return nt(a_hi, b_hi) + nt(a_hi, b_lo) + nt(a_lo, b_hi)


def _dot01_f32(m01, x):
    hi = x.astype(BF16)
    lo = (x - hi.astype(F32)).astype(BF16)
    return _dot(m01, hi) + _dot(m01, lo)


def _resident(stacked, layer):
    return pl.BlockSpec((None,) + stacked.shape[1:], lambda i: (layer, 0, 0), pipeline_mode=pl.Buffered(1))


def _norm_matmul_kernel(x_ref, g_ref, *refs):
    n = len(refs) // 3
    w_refs, o_refs, wb_refs = refs[:n], refs[n:2 * n], refs[2 * n:]

    @pl.when(pl.program_id(0) == 0)
    def _():
        for w_ref, wb_ref in zip(w_refs, wb_refs):
            wb_ref[...] = (w_ref[...] * g_ref[...]).astype(BF16)

    x = x_ref[...]
    xb, r = x.astype(BF16), _row_scale(x)
    for o_ref, wb_ref in zip(o_refs, wb_refs):
        o_ref[...] = (_dot(xb, wb_ref[...]) * r).astype(o_ref.dtype)


def _norm_matmul(x2d, g, w_stacked, layers, out_dtype):
    t, d = x2d.shape
    n = w_stacked.shape[2]
    tile = lambda width: pl.BlockSpec((ROW_TILE, width), lambda i: (i, 0))
    return pl.pallas_call(
        _norm_matmul_kernel,
        grid=(t // ROW_TILE,),
        in_specs=[tile(d), pl.BlockSpec((d, 1), lambda i: (0, 0))] + [_resident(w_stacked, l) for l in layers],
        out_specs=[tile(n) for _ in layers],
        out_shape=[jax.ShapeDtypeStruct((t, n), out_dtype) for _ in layers],
        scratch_shapes=[pltpu.VMEM((d, n), BF16) for _ in layers],
        compiler_params=_params("arbitrary"),
        name="norm_matmul",
    )(x2d, g.reshape(d, 1), *[w_stacked for _ in layers])


def _hgrn_kernel(q_ref, f_ref, i_ref, g_ref, lbz_ref, on_ref, o_ref, *, layer):
    seq, dk = q_ref.shape[1], HGRN_HEAD_DIM
    heads_here = q_ref.shape[2] // dk
    c, grp = HGRN_CHUNK, HGRN_GROUP
    rows = c * grp
    lbz = lbz_ref[...]
    e = jnp.exp(lbz - jnp.max(lbz, axis=0, keepdims=True))
    lb_all = jnp.sum(e[:layer + 1], axis=0, keepdims=True) / jnp.sum(e, axis=0, keepdims=True)
    on_all = on_ref[...]
    row = lax.broadcasted_iota(jnp.int32, (rows, rows), 0)
    col = lax.broadcasted_iota(jnp.int32, (rows, rows), 1)
    causal = (col <= row) & (col >= (row // c) * c)
    tril = causal.astype(BF16)

    def group(hh, n):
        sl = slice(n * rows, (n + 1) * rows)
        ls = slice(hh * dk, (hh + 1) * dk)
        lb = lb_all[:, ls]
        f = lb + (1.0 - lb) * _sigmoid(f_ref[0, sl, ls])
        cum = _dot01_f32(tril, jnp.log(f))
        yield
        cum = cum * LOG2E
        k = 1.0 - f
        q_in = (_silu(q_ref[0, sl, ls]) * jnp.exp2(cum)).astype(BF16)
        k_in = (k * jnp.exp2(-cum)).astype(BF16)
        att = lax.dot_general(q_in, k_in, _NT, preferred_element_type=F32)
        vb = i_ref[0, sl, ls].astype(BF16)
        cum3 = cum.reshape(grp, c, dk)
        cl = cum3[:, c - 1:c, :]
        kdec = (k.reshape(grp, c, dk) * jnp.exp2(cl - cum3)).astype(BF16)
        decay = jnp.exp2(cl)
        kv = [lax.dot_general(vb[j * c:(j + 1) * c], kdec[j], _TN, preferred_element_type=F32)
              for j in range(grp)]
        yield
        att = jnp.where(causal, att, 0.0).astype(BF16)
        yield q_in, decay, kv, _dot(att, vb)

    order = [(hh, n) for n in range(seq // rows) for hh in range(heads_here)]
    gens = [group(hh, n) for hh, n in order]
    for _ in range(2):
        for gen in gens:
            next(gen)
    states = [jnp.zeros((dk, dk), F32) for _ in range(heads_here)]
    for (hh, n), (q_in, decay, kv, o_intra) in zip(order, [next(gen) for gen in gens]):
        sl = slice(n * rows, (n + 1) * rows)
        ls = slice(hh * dk, (hh + 1) * dk)
        st = states[hh]
        o_inter = []
        for j in range(grp):
            o_inter.append(lax.dot_general(q_in[j * c:(j + 1) * c], st.astype(BF16), _NT,
                                           preferred_element_type=F32))
            st = decay[j] * st + kv[j]
        states[hh] = st
        o = o_intra + jnp.concatenate(o_inter, axis=0)
        o = o * lax.rsqrt(jnp.mean(o * o, axis=-1, keepdims=True) + EPS)
        o_ref[0, sl, ls] = (o * on_all[:, ls] * _silu(g_ref[0, sl, ls])).astype(o_ref.dtype)


def _hgrn(proj, lbz, out_norm, layer, width):
    b, s, _ = proj.shape
    hd = HGRN_HEAD_DIM * HGRN_HEADS_PER_STEP
    heads = width // hd
    sec = lambda k: pl.BlockSpec((1, s, hd), lambda i, h: (i, 0, k * heads + h))
    return pl.pallas_call(
        functools.partial(_hgrn_kernel, layer=layer),
        grid=(b, heads),
        in_specs=[sec(0), sec(1), sec(2), sec(3),
                  pl.BlockSpec((lbz.shape[0], hd), lambda i, h: (0, h)),
                  pl.BlockSpec((1, hd), lambda i, h: (0, h))],
        out_specs=pl.BlockSpec((1, s, hd), lambda i, h: (i, 0, h)),
        out_shape=jax.ShapeDtypeStruct((b, s, width), BF16),
        compiler_params=_params("parallel", "parallel"),
        name="hgrn2",
    )(proj, proj, proj, proj, lbz, out_norm.reshape(1, width))


def _conv_kernel(a_ref, b_ref, w_ref, db_ref, lg_ref, lb_ref, o_ref, cpad_ref, acc_ref):
    seq, ch = a_ref.shape[1], a_ref.shape[2]
    kw = w_ref.shape[0]
    pad = cpad_ref.shape[0] - seq
    tt = CONV_TILE
    cpad_ref[0:pad, :] = jnp.zeros((pad, ch), F32)

    def fill(t, carry):
        sl = pl.ds(pl.multiple_of(t * tt, tt), tt)
        cpad_ref[pl.ds(pl.multiple_of(pad + t * tt, SUBLANES), tt), :] = (
            a_ref[0, sl, :] * _sigmoid(b_ref[0, sl, :]))
        return carry

    lax.fori_loop(0, seq // tt, fill, 0)
    db, lg, lb = db_ref[...], lg_ref[...], lb_ref[...]

    offs = [pad - kw + 1 + k for k in range(kw)]
    sup = acc_ref.shape[0]
    lane_blocks = ch // LANES

    def conv(s, carry):
        sbase = pl.multiple_of(s * sup, sup)

        def block(i, c2):
            t, j = i // lane_blocks, i % lane_blocks
            base = pl.multiple_of(sbase + t * tt, tt)
            ls = pl.ds(pl.multiple_of(j * LANES, LANES), LANES)
            win = cpad_ref[pl.ds(base, tt + pad), ls]
            acc = jnp.zeros((tt, LANES), F32)
            for r in range(SUBLANES):
                taps = [k for k in range(kw) if offs[k] % SUBLANES == r]
                if not taps:
                    continue
                shifted = pltpu.roll(win, tt + pad - r, axis=0) if r else win
                for k in taps:
                    acc = acc + w_ref[k:k + 1, ls] * shifted[offs[k] - r:offs[k] - r + tt, :]
            acc_ref[pl.ds(pl.multiple_of(t * tt, tt), tt), ls] = acc
            return c2

        lax.fori_loop(0, (sup // tt) * lane_blocks, block, 0)
        acc = acc_ref[...] + db
        mu = jnp.mean(acc, axis=-1, keepdims=True)
        d = acc - mu
        y = d * lax.rsqrt(jnp.mean(d * d, axis=-1, keepdims=True) + EPS) * lg + lb
        o_ref[0, pl.ds(sbase, sup), :] = _silu(y).astype(o_ref.dtype)
        return carry

    lax.fori_loop(0, seq // sup, conv, 0)


def _conv_module(proj, col_a, col_b, dw_w, dw_b, ln_g, ln_b):
    b, s, _ = proj.shape
    kw, ch = dw_w.shape
    pad = -(-(kw - 1) // SUBLANES) * SUBLANES
    vec = lambda: pl.BlockSpec((1, ch), lambda i: (0, 0))
    return pl.pallas_call(
        _conv_kernel,
        grid=(b,),
        in_specs=[pl.BlockSpec((1, s, ch), lambda i: (i, 0, col_a)),
                  pl.BlockSpec((1, s, ch), lambda i: (i, 0, col_b)),
                  pl.BlockSpec((kw, ch), lambda i: (0, 0)), vec(), vec(), vec()],
        out_specs=pl.BlockSpec((1, s, ch), lambda i: (i, 0, 0)),
        out_shape=jax.ShapeDtypeStruct((b, s, ch), BF16),
        scratch_shapes=[pltpu.VMEM((pad + s, ch), F32), pltpu.VMEM((CONV_NORM_ROWS, ch), F32)],
        compiler_params=_params("parallel"),
        name="conv_module",
    )(proj, proj, dw_w, dw_b.reshape(1, ch), ln_g.reshape(1, ch), ln_b.reshape(1, ch))


def _moba_kernel(q_ref, k_ref, v_ref, o_ref, qaug_ref, kaug_ref, vaug_ref, s_ref, p_ref, m_ref, acc_ref):
    seq, w = q_ref.shape[1], q_ref.shape[2]
    hd, blk, topk = MOBA_HEAD_DIM, MOBA_BLOCK, MOBA_TOPK
    nblk = seq // blk
    half = blk // 2
    scale = hd ** -0.5
    k = k_ref[0]
    v = v_ref[0]
    kmean = jnp.mean(k.reshape(nblk, blk, w), axis=1)
    lane_k = lax.broadcasted_iota(jnp.int32, (seq, w), 1)
    blk_k = lax.broadcasted_iota(jnp.int32, (seq, w), 0) // blk
    for e in range(2):
        in_head = (lane_k >= e * hd) & (lane_k < (e + 1) * hd)
        onehot = (lane_k - (1 - e) * hd == blk_k).astype(F32)
        kaug_ref[e] = jnp.where(in_head, k, onehot).astype(BF16)
        vaug_ref[e] = jnp.where(in_head, v, 1.0).astype(BF16)

    lane_q = lax.broadcasted_iota(jnp.int32, (blk, w), 1)
    lane_m = lax.broadcasted_iota(jnp.int32, (nblk, w), 1)
    ridx = lax.broadcasted_iota(jnp.int32, (nblk, blk), 0)
    own_causal = (lax.broadcasted_iota(jnp.int32, (blk, blk), 1)
                  <= lax.broadcasted_iota(jnp.int32, (blk, blk), 0))

    for jq in range(nblk):
        qs = slice(jq * blk, (jq + 1) * blk)
        q2 = q_ref[0, qs, :]
        for e in range(2):
            lo = (1 - e) * hd
            if jq > topk:
                kme = jnp.where((lane_m >= e * hd) & (lane_m < (e + 1) * hd), kmean, 0.0)
                st = _dot_nt3(kme, q2)
                valid = ridx < jq
                rows = []
                for n in range(nblk):
                    if n < jq:
                        sn = st[n:n + 1, :]
                        beats = valid & ((st > sn) | ((st == sn) & (ridx < n)))
                        rank = jnp.sum(beats.astype(F32), axis=0, keepdims=True)
                        rows.append(jnp.where(rank < topk, 0.0, NEG))
                    else:
                        rows.append(jnp.zeros((1, blk), F32))
                pieces = [jnp.concatenate(rows, axis=0)]
                if lo:
                    pieces.insert(0, jnp.zeros((lo, blk), F32))
                if w - lo - nblk:
                    pieces.append(jnp.zeros((w - lo - nblk, blk), F32))
                bias_q = jnp.concatenate(pieces, axis=0).T
            else:
                bias_q = jnp.zeros((blk, w), F32)
            in_head = (lane_q >= e * hd) & (lane_q < (e + 1) * hd)
            qaug_ref[e, qs, :] = jnp.where(in_head, q2 * (scale * LOG2E), bias_q).astype(BF16)

    base = [sum(seq - i * blk for i in range(n)) for n in range(nblk)]

    def fold_max(e, rows, s, first):
        t = jnp.maximum(s[:, :half], s[:, half:])
        m_ref[e, rows, :] = t if first else jnp.maximum(m_ref[e, rows, :], t)

    def score(e, n):
        s = lax.dot_general(qaug_ref[e, n * blk:, :], kaug_ref[e, n * blk:(n + 1) * blk, :], _NT,
                            preferred_element_type=F32)
        own = jnp.where(own_causal, s[:blk], NEG)
        s_ref[e, base[n]:base[n] + blk, :] = own
        fold_max(e, slice(n * blk, (n + 1) * blk), own, n == 0)
        if n + 1 < nblk:
            s_ref[e, base[n] + blk:base[n] + seq - n * blk, :] = s[blk:]
            fold_max(e, slice((n + 1) * blk, seq), s[blk:], n == 0)

    def row_max(e):
        m_ref[e] = jnp.broadcast_to(jnp.max(m_ref[e], axis=-1, keepdims=True), (seq, half))

    def weights(e, n):
        rows = slice(base[n], base[n] + seq - n * blk)
        s = s_ref[e, rows, :]
        mb = m_ref[e, n * blk:, :]
        p_ref[e, rows, :half] = jnp.exp2(s[:, :half] - mb).astype(BF16)
        p_ref[e, rows, half:] = jnp.exp2(s[:, half:] - mb).astype(BF16)

    def values(e, n):
        rows = slice(base[n], base[n] + seq - n * blk)
        o = _dot(p_ref[e, rows, :], vaug_ref[e, n * blk:(n + 1) * blk, :])
        if n == 0:
            acc_ref[e] = o
        else:
            acc_ref[e, n * blk:, :] += o

    for n in range(nblk):
        score(0, n)
    row_max(0)
    for n in range(nblk):
        score(1, n)
        weights(0, n)
    row_max(1)
    for n in range(nblk):
        values(0, n)
        weights(1, n)
    for n in range(nblk):
        values(1, n)
    outs = []
    for e in range(2):
        lo = (1 - e) * hd
        o = acc_ref[e]
        outs.append(o * (1.0 / o[:, lo:lo + 1]))
    lane_o = lax.broadcasted_iota(jnp.int32, (seq, w), 1)
    o_ref[0] = jnp.where(lane_o < hd, outs[0], outs[1]).astype(o_ref.dtype)


def _moba(proj, width):
    b, s, _ = proj.shape
    w = 2 * MOBA_HEAD_DIM
    pairs = width // w
    nblk = s // MOBA_BLOCK
    tiles = nblk * (nblk + 1) // 2
    sec = lambda k: pl.BlockSpec((1, s, w), lambda i, h: (i, 0, k * pairs + h))
    return pl.pallas_call(
        _moba_kernel,
        grid=(b, pairs),
        in_specs=[sec(0), sec(1), sec(2)],
        out_specs=pl.BlockSpec((1, s, w), lambda i, h: (i, 0, h)),
        out_shape=jax.ShapeDtypeStruct((b, s, width), BF16),
        scratch_shapes=[pltpu.VMEM((2, s, w), BF16), pltpu.VMEM((2, s, w), BF16), pltpu.VMEM((2, s, w), BF16),
                        pltpu.VMEM((2, tiles * MOBA_BLOCK, MOBA_BLOCK), F32),
                        pltpu.VMEM((2, tiles * MOBA_BLOCK, MOBA_BLOCK), BF16),
                        pltpu.VMEM((2, s, MOBA_BLOCK // 2), F32),
                        pltpu.VMEM((2, s, w), F32)],
        compiler_params=_params("parallel", "parallel"),
        name="moba",
    )(proj, proj, proj)


def _sgu_kernel(u_ref, z_ref, lg_ref, lb_ref, w_ref, bias_ref, o_ref):
    ts, width = u_ref.shape[1], u_ref.shape[2]
    c = SGU_CHUNK
    gd = width // SGU_GROUPS
    row = lax.broadcasted_iota(jnp.int32, (c, c), 0)
    col = lax.broadcasted_iota(jnp.int32, (c, c), 1)
    for g in range(SGU_GROUPS):
        gs = slice(g * gd, (g + 1) * gd)
        z = _gelu(z_ref[0, :, gs])
        mu = jnp.mean(z, axis=-1, keepdims=True)
        d = z - mu
        zn = (d * lax.rsqrt(jnp.mean(d * d, axis=-1, keepdims=True) + EPS) * lg_ref[:, gs]
              + lb_ref[:, gs]).astype(BF16)
        wg = jnp.where(col <= row, w_ref[g], 0.0).astype(BF16)
        for n in range(ts // c):
            ts_ = slice(n * c, (n + 1) * c)
            mixed = _dot(wg, zn[ts_, :]) + bias_ref[:, gs]
            o_ref[0, ts_, gs] = (_gelu(u_ref[0, ts_, gs]) * mixed).astype(o_ref.dtype)


def _sgu(proj, col_u, col_z, ln_g, ln_b, w, bias):
    b, s, _ = proj.shape
    width = ln_g.shape[0]
    groups, c, _ = w.shape
    ts = SGU_ROWS
    bias_full = jnp.repeat(bias.T, width // groups, axis=1)
    return pl.pallas_call(
        _sgu_kernel,
        grid=(b, s // ts),
        in_specs=[pl.BlockSpec((1, ts, width), lambda i, j: (i, j, col_u)),
                  pl.BlockSpec((1, ts, width), lambda i, j: (i, j, col_z)),
                  pl.BlockSpec((1, width), lambda i, j: (0, 0)),
                  pl.BlockSpec((1, width), lambda i, j: (0, 0)),
                  pl.BlockSpec((groups, c, c), lambda i, j: (0, 0, 0)),
                  pl.BlockSpec((c, width), lambda i, j: (0, 0))],
        out_specs=pl.BlockSpec((1, ts, width), lambda i, j: (i, j, 0)),
        out_shape=jax.ShapeDtypeStruct((b, s, width), BF16),
        compiler_params=_params("parallel", "parallel"),
        name="sgu",
    )(proj, proj, ln_g.reshape(1, width), ln_b.reshape(1, width), w, bias_full)


def _cross_attention(x, wq_ref, kv_ref, wo_ref):
    d = x.shape[1]
    hd = d // XA_HEADS
    q = (_dot(x.astype(BF16), wq_ref[...]) * (_row_scale(x) * (hd ** -0.5 * LOG2E))).astype(BF16)

    def logits(h):
        hs = slice(h * hd, (h + 1) * hd)
        s = lax.dot_general(q[:, hs], kv_ref[0, :, hs], _NT, preferred_element_type=F32)
        return s, jnp.max(s, axis=-1, keepdims=True)

    pending = logits(0)
    outs = []
    for h in range(XA_HEADS):
        ahead = logits(h + 1) if h + 1 < XA_HEADS else None
        s, m = pending
        p = jnp.exp2(s - m)
        o = _dot(p.astype(BF16), kv_ref[0, :, d + h * hd:d + (h + 1) * hd])
        outs.append((o * (1.0 / jnp.sum(p, axis=-1, keepdims=True))).astype(BF16))
        pending = ahead
    return x + _dot(jnp.concatenate(outs, axis=-1), wo_ref[...])


def _swiglu(x, w1_ref, w2_ref):
    hidden = w2_ref.shape[0]
    r = _row_scale(x)
    h = x.astype(BF16)
    acc = x
    for c in range(hidden // FFN_CHUNK):
        cs = slice(c * FFN_CHUNK, (c + 1) * FFN_CHUNK)
        gs = slice(hidden + c * FFN_CHUNK, hidden + (c + 1) * FFN_CHUNK)
        u = (_silu(_dot(h, w1_ref[:, cs]) * r) * (_dot(h, w1_ref[:, gs]) * r)).astype(BF16)
        acc = acc + _dot(u, w2_ref[cs, :])
    return acc


def _post_mixer_kernel(x_ref, a_ref, b_ref, kv_ref, fg_ref, gx_c, gf_c,
                       wmix_c, wq_c, wo_c, w1_c, w2_c, o_ref,
                       wmix_b, wq_b, wo_b, w1_b, w2_b, *, final):
    i = pl.program_id(0)

    @pl.when(i < POST_CAST_STEPS)
    def _():
        for src, dst, gain in ((wmix_c, wmix_b, None), (wq_c, wq_b, gx_c), (wo_c, wo_b, None),
                               (w1_c, w1_b, gf_c), (w2_c, w2_b, None)):
            rows = src.shape[0]
            chunk = src[...] if gain is None else src[...] * gain[...]
            dst[pl.ds(pl.multiple_of(i * rows, rows), rows), :] = chunk.astype(BF16)

    @pl.when(i >= POST_CAST_STEPS)
    def _():
        ka = a_ref.shape[1]
        x = x_ref[...] + _dot(a_ref[...], wmix_b[0:ka, :]) + _dot(b_ref[...], wmix_b[ka:, :])
        x = _cross_attention(x, wq_b, kv_ref, wo_b)
        x = _swiglu(x, w1_b, w2_b)
        o_ref[...] = _rms(x, fg_ref[...]) if final else x


def _post_mixer(x2d, a2d, b2d, w_mix, mix_layer, gx, wq, kv, wo, gf, w1, w2, layer, final_gain, final, seq):
    t, d = x2d.shape
    ka, kb = a2d.shape[1], b2d.shape[1]
    m = kv.shape[1]
    nc = POST_CAST_STEPS
    per_batch = seq // ROW_TILE
    tile = lambda i: jnp.maximum(i - nc, 0)
    row = lambda n: pl.BlockSpec((ROW_TILE, n), lambda i: (tile(i), 0))
    vec = lambda: pl.BlockSpec((1, d), lambda i: (0, 0))

    def chunk(w, idx):
        return pl.BlockSpec((None, w.shape[1] // nc, w.shape[2]), lambda i: (idx, jnp.minimum(i, nc - 1), 0))

    def gain():
        return pl.BlockSpec((d // nc, 1), lambda i: (jnp.minimum(i, nc - 1), 0))

    weights = ((w_mix, mix_layer), (wq, layer), (wo, layer), (w1, layer), (w2, layer))
    return pl.pallas_call(
        functools.partial(_post_mixer_kernel, final=final),
        grid=(nc + t // ROW_TILE,),
        in_specs=[row(d), row(ka), row(kb),
                  pl.BlockSpec((1, m, 2 * d), lambda i: (tile(i) // per_batch, 0, 0)), vec(), gain(), gain()]
                 + [chunk(w, idx) for w, idx in weights],
        out_specs=row(d),
        out_shape=jax.ShapeDtypeStruct((t, d), F32),
        scratch_shapes=[pltpu.VMEM(w.shape[1:], BF16) for w, _ in weights],
        compiler_params=_params("arbitrary"),
        name="post_mixer",
    )(x2d, a2d, b2d, kv, final_gain.reshape(1, d), gx.reshape(d, 1), gf.reshape(d, 1),
      *[w for w, _ in weights])


def kernel(x, mem, norm_mix, norm_xattn, norm_ffn, mem_norm, final_norm, w_in_ab, w_out_ab, hgrn_lower_bounds, hgrn_out_norm, conv_dw_w, conv_dw_b, conv_ln_g, conv_ln_b, w_in_cd, w_out_cd, sgu_ln_g, sgu_ln_b, sgu_w, sgu_b, xa_wq, xa_wkv, xa_wo, ffn_w_in, ffn_w_out):
    b, s, d = x.shape
    m = mem.shape[1]
    depth = norm_mix.shape[0]
    x2d = x.reshape(b * s, d)
    kvs = _norm_matmul(mem.reshape(b * m, d), mem_norm, xa_wkv, range(depth), BF16)
    for l in range(depth):
        if l % 2 == 0:
            e = l // 2
            a_width = hgrn_out_norm.shape[1]
            b_width = conv_dw_w.shape[2]
            proj = _norm_matmul(x2d, norm_mix[l], w_in_ab, [e], F32)[0].reshape(b, s, -1)
            o_a = _hgrn(proj, hgrn_lower_bounds, hgrn_out_norm[e], l, a_width)
            col = 4 * a_width // b_width
            o_b = _conv_module(proj, col, col + 1, conv_dw_w[e], conv_dw_b[e], conv_ln_g[e], conv_ln_b[e])
            w_out, mix_layer = w_out_ab, e
        else:
            o = l // 2
            d_width = sgu_ln_g.shape[1]
            c_width = w_out_cd.shape[1] - d_width
            proj = _norm_matmul(x2d, norm_mix[l], w_in_cd, [o], F32)[0].reshape(b, s, -1)
            o_a = _moba(proj, c_width)
            col = 3 * c_width // d_width
            o_b = _sgu(proj, col, col + 1, sgu_ln_g[o], sgu_ln_b[o], sgu_w[o], sgu_b[o])
            w_out, mix_layer = w_out_cd, o
        kv = kvs[l].reshape(b, m, 2 * d)
        x2d = _post_mixer(x2d, o_a.reshape(b * s, -1), o_b.reshape(b * s, -1), w_out, mix_layer,
                          norm_xattn[l], xa_wq, kv, xa_wo, norm_ffn[l], ffn_w_in, ffn_w_out, l,
                          final_norm, l == depth - 1, s)
    return x2d.reshape(b, s, d)
```

```python
import functools

import jax
import jax.numpy as jnp
from jax import lax
from jax.experimental import pallas as pl
from jax.experimental.pallas import tpu as pltpu

F32 = jnp.float32
BF16 = jnp.bfloat16
EPS = 1e-6
NEG = -1e30
LOG2E = 1.4426950408889634

V7X_VMEM_BYTES = 64 * 1024 * 1024
VMEM_LIMIT = V7X_VMEM_BYTES - 8 * 1024 * 1024
SUBLANES = 8
LANES = 128

HGRN_HEAD_DIM = 128
HGRN_CHUNK = 64
HGRN_HEADS_PER_STEP = 4
HGRN_GROUP = 4
CONV_TILE = 2048
CONV_NORM_ROWS = 2048
MOBA_HEAD_DIM = 64
MOBA_BLOCK = 256
MOBA_TOPK = 3
SGU_CHUNK = 128
SGU_GROUPS = 4
SGU_ROWS = 2048
XA_HEADS = 4
ROW_TILE = 512
FFN_CHUNK = 256
POST_CAST_STEPS = 8

_NT = (((1,), (1,)), ((), ()))
_TN = (((0,), (0,)), ((), ()))


def _params(*sem):
    return pltpu.CompilerParams(dimension_semantics=sem, vmem_limit_bytes=VMEM_LIMIT)


def _row_scale(x):
    return lax.rsqrt(jnp.mean(x * x, axis=-1, keepdims=True) + EPS)


def _rms(x, g):
    return x * _row_scale(x) * g


def _sigmoid(x):
    return 1.0 / (1.0 + jnp.exp2(x * -LOG2E))


def _silu(x):
    return x * _sigmoid(x)


def _gelu(x):
    return 0.5 * x * (1.0 + lax.erf(x * (2.0 ** -0.5)))


def _dot(a, b):
    return jnp.dot(a, b, preferred_element_type=F32)


def _dot_nt3(a, b):
    a_hi, b_hi = a.astype(BF16), b.astype(BF16)
    a_lo = (a - a_hi.astype(F32)).astype(BF16)
    b_lo = (b - b_hi.astype(F32)).astype(BF16)
    nt = lambda u, v: lax.dot_general(u, v, _NT, preferred_element_type=F32)
    return nt(a_hi, b_hi) + nt(a_hi, b_lo) + nt(a_lo, b_hi)


def _dot01_f32(m01, x):
    hi = x.astype(BF16)
    lo = (x - hi.astype(F32)).astype(BF16)
    return _dot(m01, hi) + _dot(m01, lo)


def _resident(stacked, layer):
    return pl.BlockSpec((None,) + stacked.shape[1:], lambda i: (layer, 0, 0), pipeline_mode=pl.Buffered(1))


def _norm_matmul_kernel(x_ref, g_ref, *refs):
    n = len(refs) // 3
    w_refs, o_refs, wb_refs = refs[:n], refs[n:2 * n], refs[2 * n:]

    @pl.when(pl.program_id(0) == 0)
    def _():
        for w_ref, wb_ref in zip(w_refs, wb_refs):
            wb_ref[...] = (w_ref[...] * g_ref[...]).astype(BF16)

    x = x_ref[...]
    xb, r = x.astype(BF16), _row_scale(x)
    for o_ref, wb_ref in zip(o_refs, wb_refs):
        o_ref[...] = (_dot(xb, wb_ref[...]) * r).astype(o_ref.dtype)


def _norm_matmul(x2d, g, w_stacked, layers, out_dtype):
    t, d = x2d.shape
    n = w_stacked.shape[2]
    tile = lambda width: pl.BlockSpec((ROW_TILE, width), lambda i: (i, 0))
    return pl.pallas_call(
        _norm_matmul_kernel,
        grid=(t // ROW_TILE,),
        in_specs=[tile(d), pl.BlockSpec((d, 1), lambda i: (0, 0))] + [_resident(w_stacked, l) for l in layers],
        out_specs=[tile(n) for _ in layers],
        out_shape=[jax.ShapeDtypeStruct((t, n), out_dtype) for _ in layers],
        scratch_shapes=[pltpu.VMEM((d, n), BF16) for _ in layers],
        compiler_params=_params("arbitrary"),
        name="norm_matmul",
    )(x2d, g.reshape(d, 1), *[w_stacked for _ in layers])


def _hgrn_kernel(q_ref, f_ref, i_ref, g_ref, lbz_ref, o_ref, *, layer):
    seq, dk = q_ref.shape[1], HGRN_HEAD_DIM
    heads_here = q_ref.shape[2] // dk
    c, grp = HGRN_CHUNK, HGRN_GROUP
    rows = c * grp
    lbz = lbz_ref[...]
    e = jnp.exp(lbz - jnp.max(lbz, axis=0, keepdims=True))
    lb_all = jnp.sum(e[:layer + 1], axis=0, keepdims=True) / jnp.sum(e, axis=0, keepdims=True)
    row = lax.broadcasted_iota(jnp.int32, (rows, rows), 0)
    col = lax.broadcasted_iota(jnp.int32, (rows, rows), 1)
    causal = (col <= row) & (col >= (row // c) * c)
    tril = causal.astype(BF16)

    def group(hh, n):
        sl = slice(n * rows, (n + 1) * rows)
        ls = slice(hh * dk, (hh + 1) * dk)
        lb = lb_all[:, ls]
        f = lb + (1.0 - lb) * _sigmoid(f_ref[0, sl, ls])
        cum = _dot01_f32(tril, jnp.log(f))
        yield
        cum = cum * LOG2E
        k = 1.0 - f
        q_in = (_silu(q_ref[0, sl, ls]) * jnp.exp2(cum)).astype(BF16)
        k_in = (k * jnp.exp2(-cum)).astype(BF16)
        att = lax.dot_general(q_in, k_in, _NT, preferred_element_type=F32)
        vb = i_ref[0, sl, ls].astype(BF16)
        cum3 = cum.reshape(grp, c, dk)
        cl = cum3[:, c - 1:c, :]
        kdec = (k.reshape(grp, c, dk) * jnp.exp2(cl - cum3)).astype(BF16)
        decay = jnp.exp2(cl)
        kv = [lax.dot_general(vb[j * c:(j + 1) * c], kdec[j], _TN, preferred_element_type=F32)
              for j in range(grp)]
        yield
        att = jnp.where(causal, att, 0.0).astype(BF16)
        yield q_in, decay, kv, _dot(att, vb)

    order = [(hh, n) for n in range(seq // rows) for hh in range(heads_here)]
    gens = [group(hh, n) for hh, n in order]
    for _ in range(2):
        for gen in gens:
            next(gen)
    states = [jnp.zeros((dk, dk), F32) for _ in range(heads_here)]
    for (hh, n), (q_in, decay, kv, o_intra) in zip(order, [next(gen) for gen in gens]):
        sl = slice(n * rows, (n + 1) * rows)
        ls = slice(hh * dk, (hh + 1) * dk)
        st = states[hh]
        o_inter = []
        for j in range(grp):
            o_inter.append(lax.dot_general(q_in[j * c:(j + 1) * c], st.astype(BF16), _NT,
                                           preferred_element_type=F32))
            st = decay[j] * st + kv[j]
        states[hh] = st
        o = o_intra + jnp.concatenate(o_inter, axis=0)
        o = o * lax.rsqrt(jnp.mean(o * o, axis=-1, keepdims=True) + EPS)
        o_ref[0, sl, ls] = (o * _silu(g_ref[0, sl, ls])).astype(o_ref.dtype)


def _hgrn(proj, lbz, layer, width):
    b, s, _ = proj.shape
    hd = HGRN_HEAD_DIM * HGRN_HEADS_PER_STEP
    heads = width // hd
    sec = lambda k: pl.BlockSpec((1, s, hd), lambda i, h: (i, 0, k * heads + h))
    return pl.pallas_call(
        functools.partial(_hgrn_kernel, layer=layer),
        grid=(b, heads),
        in_specs=[sec(0), sec(1), sec(2), sec(3),
                  pl.BlockSpec((lbz.shape[0], hd), lambda i, h: (0, h))],
        out_specs=pl.BlockSpec((1, s, hd), lambda i, h: (i, 0, h)),
        out_shape=jax.ShapeDtypeStruct((b, s, width), BF16),
        compiler_params=_params("parallel", "parallel"),
        name="hgrn2",
    )(proj, proj, proj, proj, lbz)


def _conv_kernel(a_ref, b_ref, w_ref, db_ref, lg_ref, lb_ref, o_ref, cpad_ref, acc_ref):
    seq, ch = a_ref.shape[1], a_ref.shape[2]
    kw = w_ref.shape[0]
    pad = cpad_ref.shape[0] - seq
    tt = CONV_TILE
    cpad_ref[0:pad, :] = jnp.zeros((pad, ch), F32)

    def fill(t, carry):
        sl = pl.ds(pl.multiple_of(t * tt, tt), tt)
        cpad_ref[pl.ds(pl.multiple_of(pad + t * tt, SUBLANES), tt), :] = (
            a_ref[0, sl, :] * _sigmoid(b_ref[0, sl, :]))
        return carry

    lax.fori_loop(0, seq // tt, fill, 0)
    db, lg, lb = db_ref[...], lg_ref[...], lb_ref[...]

    offs = [pad - kw + 1 + k for k in range(kw)]
    sup = acc_ref.shape[0]
    lane_blocks = ch // LANES

    def conv(s, carry):
        sbase = pl.multiple_of(s * sup, sup)

        def block(i, c2):
            t, j = i // lane_blocks, i % lane_blocks
            base = pl.multiple_of(sbase + t * tt, tt)
            ls = pl.ds(pl.multiple_of(j * LANES, LANES), LANES)
            win = cpad_ref[pl.ds(base, tt + pad), ls]
            acc = jnp.zeros((tt, LANES), F32)
            for r in range(SUBLANES):
                taps = [k for k in range(kw) if offs[k] % SUBLANES == r]
                if not taps:
                    continue
                shifted = pltpu.roll(win, tt + pad - r, axis=0) if r else win
                for k in taps:
                    acc = acc + w_ref[k:k + 1, ls] * shifted[offs[k] - r:offs[k] - r + tt, :]
            acc_ref[pl.ds(pl.multiple_of(t * tt, tt), tt), ls] = acc
            return c2

        lax.fori_loop(0, (sup // tt) * lane_blocks, block, 0)
        acc = acc_ref[...] + db
        mu = jnp.mean(acc, axis=-1, keepdims=True)
        d = acc - mu
        y = d * lax.rsqrt(jnp.mean(d * d, axis=-1, keepdims=True) + EPS) * lg + lb
        o_ref[0, pl.ds(sbase, sup), :] = _silu(y).astype(o_ref.dtype)
        return carry

    lax.fori_loop(0, seq // sup, conv, 0)


def _conv_module(proj, col_a, col_b, dw_w, dw_b, ln_g, ln_b):
    b, s, _ = proj.shape
    kw, ch = dw_w.shape
    pad = -(-(kw - 1) // SUBLANES) * SUBLANES
    vec = lambda: pl.BlockSpec((1, ch), lambda i: (0, 0))
    return pl.pallas_call(
        _conv_kernel,
        grid=(b,),
        in_specs=[pl.BlockSpec((1, s, ch), lambda i: (i, 0, col_a)),
                  pl.BlockSpec((1, s, ch), lambda i: (i, 0, col_b)),
                  pl.BlockSpec((kw, ch), lambda i: (0, 0)), vec(), vec(), vec()],
        out_specs=pl.BlockSpec((1, s, ch), lambda i: (i, 0, 0)),
        out_shape=jax.ShapeDtypeStruct((b, s, ch), BF16),
        scratch_shapes=[pltpu.VMEM((pad + s, ch), F32), pltpu.VMEM((CONV_NORM_ROWS, ch), F32)],
        compiler_params=_params("parallel"),
        name="conv_module",
    )(proj, proj, dw_w, dw_b.reshape(1, ch), ln_g.reshape(1, ch), ln_b.reshape(1, ch))


def _moba_kernel(q_ref, k_ref, v_ref, o_ref, qaug_ref, kaug_ref, vaug_ref, s_ref, p_ref, m_ref, acc_ref):
    seq, w = q_ref.shape[1], q_ref.shape[2]
    hd, blk, topk = MOBA_HEAD_DIM, MOBA_BLOCK, MOBA_TOPK
    nblk = seq // blk
    half = blk // 2
    scale = hd ** -0.5
    k = k_ref[0]
    v = v_ref[0]
    kmean = jnp.mean(k.reshape(nblk, blk, w), axis=1)
    lane_k = lax.broadcasted_iota(jnp.int32, (seq, w), 1)
    blk_k = lax.broadcasted_iota(jnp.int32, (seq, w), 0) // blk
    for e in range(2):
        in_head = (lane_k >= e * hd) & (lane_k < (e + 1) * hd)
        onehot = (lane_k - (1 - e) * hd == blk_k).astype(F32)
        kaug_ref[e] = jnp.where(in_head, k, onehot).astype(BF16)
        vaug_ref[e] = jnp.where(in_head, v, 1.0).astype(BF16)

    lane_q = lax.broadcasted_iota(jnp.int32, (blk, w), 1)
    lane_m = lax.broadcasted_iota(jnp.int32, (nblk, w), 1)
    ridx = lax.broadcasted_iota(jnp.int32, (nblk, blk), 0)
    own_causal = (lax.broadcasted_iota(jnp.int32, (blk, blk), 1)
                  <= lax.broadcasted_iota(jnp.int32, (blk, blk), 0))

    for jq in range(nblk):
        qs = slice(jq * blk, (jq + 1) * blk)
        q2 = q_ref[0, qs, :]
        for e in range(2):
            lo = (1 - e) * hd
            if jq > topk:
                kme = jnp.where((lane_m >= e * hd) & (lane_m < (e + 1) * hd), kmean, 0.0)
                st = _dot_nt3(kme, q2)
                valid = ridx < jq
                rows = []
                for n in range(nblk):
                    if n < jq:
                        sn = st[n:n + 1, :]
                        beats = valid & ((st > sn) | ((st == sn) & (ridx < n)))
                        rank = jnp.sum(beats.astype(F32), axis=0, keepdims=True)
                        rows.append(jnp.where(rank < topk, 0.0, NEG))
                    else:
                        rows.append(jnp.zeros((1, blk), F32))
                pieces = [jnp.concatenate(rows, axis=0)]
                if lo:
                    pieces.insert(0, jnp.zeros((lo, blk), F32))
                if w - lo - nblk:
                    pieces.append(jnp.zeros((w - lo - nblk, blk), F32))
                bias_q = jnp.concatenate(pieces, axis=0).T
            else:
                bias_q = jnp.zeros((blk, w), F32)
            in_head = (lane_q >= e * hd) & (lane_q < (e + 1) * hd)
            qaug_ref[e, qs, :] = jnp.where(in_head, q2 * (scale * LOG2E), bias_q).astype(BF16)

    base = [sum(seq - i * blk for i in range(n)) for n in range(nblk)]

    def fold_max(e, rows, s, first):
        t = jnp.maximum(s[:, :half], s[:, half:])
        m_ref[e, rows, :] = t if first else jnp.maximum(m_ref[e, rows, :], t)

    def score(e, n):
        s = lax.dot_general(qaug_ref[e, n * blk:, :], kaug_ref[e, n * blk:(n + 1) * blk, :], _NT,
                            preferred_element_type=F32)
        own = jnp.where(own_causal, s[:blk], NEG)
        s_ref[e, base[n]:base[n] + blk, :] = own
        fold_max(e, slice(n * blk, (n + 1) * blk), own, n == 0)
        if n + 1 < nblk:
            s_ref[e, base[n] + blk:base[n] + seq - n * blk, :] = s[blk:]
            fold_max(e, slice((n + 1) * blk, seq), s[blk:], n == 0)

    def row_max(e):
        m_ref[e] = jnp.broadcast_to(jnp.max(m_ref[e], axis=-1, keepdims=True), (seq, half))

    def weights(e, n):
        rows = slice(base[n], base[n] + seq - n * blk)
        s = s_ref[e, rows, :]
        mb = m_ref[e, n * blk:, :]
        p_ref[e, rows, :half] = jnp.exp2(s[:, :half] - mb).astype(BF16)
        p_ref[e, rows, half:] = jnp.exp2(s[:, half:] - mb).astype(BF16)

    def values(e, n):
        rows = slice(base[n], base[n] + seq - n * blk)
        o = _dot(p_ref[e, rows, :], vaug_ref[e, n * blk:(n + 1) * blk, :])
        if n == 0:
            acc_ref[e] = o
        else:
            acc_ref[e, n * blk:, :] += o

    for n in range(nblk):
        score(0, n)
    row_max(0)
    for n in range(nblk):
        score(1, n)
        weights(0, n)
    row_max(1)
    for n in range(nblk):
        values(0, n)
        weights(1, n)
    for n in range(nblk):
        values(1, n)
    outs = []
    for e in range(2):
        lo = (1 - e) * hd
        o = acc_ref[e]
        outs.append(o * (1.0 / o[:, lo:lo + 1]))
    lane_o = lax.broadcasted_iota(jnp.int32, (seq, w), 1)
    o_ref[0] = jnp.where(lane_o < hd, outs[0], outs[1]).astype(o_ref.dtype)


def _moba(proj, width):
    b, s, _ = proj.shape
    w = 2 * MOBA_HEAD_DIM
    pairs = width // w
    nblk = s // MOBA_BLOCK
    tiles = nblk * (nblk + 1) // 2
    sec = lambda k: pl.BlockSpec((1, s, w), lambda i, h: (i, 0, k * pairs + h))
    return pl.pallas_call(
        _moba_kernel,
        grid=(b, pairs),
        in_specs=[sec(0), sec(1), sec(2)],
        out_specs=pl.BlockSpec((1, s, w), lambda i, h: (i, 0, h)),
        out_shape=jax.ShapeDtypeStruct((b, s, width), BF16),
        scratch_shapes=[pltpu.VMEM((2, s, w), BF16), pltpu.VMEM((2, s, w), BF16), pltpu.VMEM((2, s, w), BF16),
                        pltpu.VMEM((2, tiles * MOBA_BLOCK, MOBA_BLOCK), F32),
                        pltpu.VMEM((2, tiles * MOBA_BLOCK, MOBA_BLOCK), BF16),
                        pltpu.VMEM((2, s, MOBA_BLOCK // 2), F32),
                        pltpu.VMEM((2, s, w), F32)],
        compiler_params=_params("parallel", "parallel"),
        name="moba",
    )(proj, proj, proj)


def _sgu_kernel(u_ref, z_ref, lg_ref, lb_ref, w_ref, bias_ref, o_ref):
    ts, width = u_ref.shape[1], u_ref.shape[2]
    c = SGU_CHUNK
    gd = width // SGU_GROUPS
    row = lax.broadcasted_iota(jnp.int32, (c, c), 0)
    col = lax.broadcasted_iota(jnp.int32, (c, c), 1)
    for g in range(SGU_GROUPS):
        gs = slice(g * gd, (g + 1) * gd)
        z = _gelu(z_ref[0, :, gs])
        mu = jnp.mean(z, axis=-1, keepdims=True)
        d = z - mu
        zn = (d * lax.rsqrt(jnp.mean(d * d, axis=-1, keepdims=True) + EPS) * lg_ref[:, gs]
              + lb_ref[:, gs]).astype(BF16)
        wg = jnp.where(col <= row, w_ref[g], 0.0).astype(BF16)
        for n in range(ts // c):
            ts_ = slice(n * c, (n + 1) * c)
            mixed = _dot(wg, zn[ts_, :]) + bias_ref[:, gs]
            o_ref[0, ts_, gs] = (_gelu(u_ref[0, ts_, gs]) * mixed).astype(o_ref.dtype)


def _sgu(proj, col_u, col_z, ln_g, ln_b, w, bias):
    b, s, _ = proj.shape
    width = ln_g.shape[0]
    groups, c, _ = w.shape
    ts = SGU_ROWS
    bias_full = jnp.repeat(bias.T, width // groups, axis=1)
    return pl.pallas_call(
        _sgu_kernel,
        grid=(b, s // ts),
        in_specs=[pl.BlockSpec((1, ts, width), lambda i, j: (i, j, col_u)),
                  pl.BlockSpec((1, ts, width), lambda i, j: (i, j, col_z)),
                  pl.BlockSpec((1, width), lambda i, j: (0, 0)),
                  pl.BlockSpec((1, width), lambda i, j: (0, 0)),
                  pl.BlockSpec((groups, c, c), lambda i, j: (0, 0, 0)),
                  pl.BlockSpec((c, width), lambda i, j: (0, 0))],
        out_specs=pl.BlockSpec((1, ts, width), lambda i, j: (i, j, 0)),
        out_shape=jax.ShapeDtypeStruct((b, s, width), BF16),
        compiler_params=_params("parallel", "parallel"),
        name="sgu",
    )(proj, proj, ln_g.reshape(1, width), ln_b.reshape(1, width), w, bias_full)


def _cross_attention(x, wq_ref, kv_ref, wo_ref):
    d = x.shape[1]
    hd = d // XA_HEADS
    q = (_dot(x.astype(BF16), wq_ref[...]) * (_row_scale(x) * (hd ** -0.5 * LOG2E))).astype(BF16)

    def logits(h):
        hs = slice(h * hd, (h + 1) * hd)
        s = lax.dot_general(q[:, hs], kv_ref[0, :, hs], _NT, preferred_element_type=F32)
        return s, jnp.max(s, axis=-1, keepdims=True)

    pending = logits(0)
    outs = []
    for h in range(XA_HEADS):
        ahead = logits(h + 1) if h + 1 < XA_HEADS else None
        s, m = pending
        p = jnp.exp2(s - m)
        o = _dot(p.astype(BF16), kv_ref[0, :, d + h * hd:d + (h + 1) * hd])
        outs.append((o * (1.0 / jnp.sum(p, axis=-1, keepdims=True))).astype(BF16))
        pending = ahead
    return x + _dot(jnp.concatenate(outs, axis=-1), wo_ref[...])


def _swiglu(x, w1_ref, w2_ref):
    hidden = w2_ref.shape[0]
    r = _row_scale(x)
    h = x.astype(BF16)
    acc = x
    for c in range(hidden // FFN_CHUNK):
        cs = slice(c * FFN_CHUNK, (c + 1) * FFN_CHUNK)
        gs = slice(hidden + c * FFN_CHUNK, hidden + (c + 1) * FFN_CHUNK)
        u = (_silu(_dot(h, w1_ref[:, cs]) * r) * (_dot(h, w1_ref[:, gs]) * r)).astype(BF16)
        acc = acc + _dot(u, w2_ref[cs, :])
    return acc


def _post_mixer_kernel(x_ref, a_ref, b_ref, kv_ref, fg_ref, gm_c, gx_c, gf_c,
                       wmix_c, wq_c, wo_c, w1_c, w2_c, o_ref,
                       wmix_b, wq_b, wo_b, w1_b, w2_b, *, final):
    i = pl.program_id(0)

    @pl.when(i < POST_CAST_STEPS)
    def _():
        for src, dst, gain in ((wmix_c, wmix_b, gm_c), (wq_c, wq_b, gx_c), (wo_c, wo_b, None),
                               (w1_c, w1_b, gf_c), (w2_c, w2_b, None)):
            rows = src.shape[0]
            chunk = src[...] if gain is None else src[...] * gain[...]
            dst[pl.ds(pl.multiple_of(i * rows, rows), rows), :] = chunk.astype(BF16)

    @pl.when(i >= POST_CAST_STEPS)
    def _():
        ka = a_ref.shape[1]
        x = x_ref[...] + _dot(a_ref[...], wmix_b[0:ka, :]) + _dot(b_ref[...], wmix_b[ka:, :])
        x = _cross_attention(x, wq_b, kv_ref, wo_b)
        x = _swiglu(x, w1_b, w2_b)
        o_ref[...] = _rms(x, fg_ref[...]) if final else x


def _post_mixer(x2d, a2d, b2d, w_mix, mix_layer, gm, gx, wq, kv, wo, gf, w1, w2, layer, final_gain, final, seq):
    t, d = x2d.shape
    ka, kb = a2d.shape[1], b2d.shape[1]
    m = kv.shape[1]
    nc = POST_CAST_STEPS
    per_batch = seq // ROW_TILE
    tile = lambda i: jnp.maximum(i - nc, 0)
    row = lambda n: pl.BlockSpec((ROW_TILE, n), lambda i: (tile(i), 0))
    vec = lambda: pl.BlockSpec((1, d), lambda i: (0, 0))

    def chunk(w, idx):
        return pl.BlockSpec((None, w.shape[1] // nc, w.shape[2]), lambda i: (idx, jnp.minimum(i, nc - 1), 0))

    def gain():
        return pl.BlockSpec((d // nc, 1), lambda i: (jnp.minimum(i, nc - 1), 0))

    weights = ((w_mix, mix_layer), (wq, layer), (wo, layer), (w1, layer), (w2, layer))
    return pl.pallas_call(
        functools.partial(_post_mixer_kernel, final=final),
        grid=(nc + t // ROW_TILE,),
        in_specs=[row(d), row(ka), row(kb),
                  pl.BlockSpec((1, m, 2 * d), lambda i: (tile(i) // per_batch, 0, 0)), vec(), gain(), gain(), gain()]
                 + [chunk(w, idx) for w, idx in weights],
        out_specs=row(d),
        out_shape=jax.ShapeDtypeStruct((t, d), F32),
        scratch_shapes=[pltpu.VMEM(w.shape[1:], BF16) for w, _ in weights],
        compiler_params=_params("arbitrary"),
        name="post_mixer",
    )(x2d, a2d, b2d, kv, final_gain.reshape(1, d), gm.reshape(d, 1), gx.reshape(d, 1), gf.reshape(d, 1),
      *[w for w, _ in weights])


def kernel(x, mem, norm_mix, norm_xattn, norm_ffn, mem_norm, final_norm, w_in_ab, w_out_ab, hgrn_lower_bounds, hgrn_out_norm, conv_dw_w, conv_dw_b, conv_ln_g, conv_ln_b, w_in_cd, w_out_cd, sgu_ln_g, sgu_ln_b, sgu_w, sgu_b, xa_wq, xa_wkv, xa_wo, ffn_w_in, ffn_w_out):
    b, s, d = x.shape
    m = mem.shape[1]
    depth = norm_mix.shape[0]
    x2d = x.reshape(b * s, d)
    kvs = _norm_matmul(mem.reshape(b * m, d), mem_norm, xa_wkv, range(depth), BF16)
    for l in range(depth):
        if l % 2 == 0:
            e = l // 2
            a_width = hgrn_out_norm.shape[1]
            b_width = conv_dw_w.shape[2]
            proj = _norm_matmul(x2d, norm_mix[l], w_in_ab, [e], F32)[0].reshape(b, s, -1)
            o_a = _hgrn(proj, hgrn_lower_bounds, l, a_width)
            col = 4 * a_width // b_width
            o_b = _conv_module(proj, col, col + 1, conv_dw_w[e], conv_dw_b[e], conv_ln_g[e], conv_ln_b[e])
            w_out, mix_layer = w_out_ab, e
            gm = jnp.concatenate([hgrn_out_norm[e], jnp.ones((b_width,), F32)])
        else:
            o = l // 2
            d_width = sgu_ln_g.shape[1]
            c_width = w_out_cd.shape[1] - d_width
            proj = _norm_matmul(x2d, norm_mix[l], w_in_cd, [o], F32)[0].reshape(b, s, -1)
            o_a = _moba(proj, c_width)
            col = 3 * c_width // d_width
            o_b = _sgu(proj, col, col + 1, sgu_ln_g[o], sgu_ln_b[o], sgu_w[o], sgu_b[o])
            w_out, mix_layer = w_out_cd, o
            gm = jnp.ones((d,), F32)
        kv = kvs[l].reshape(b, m, 2 * d)
        x2d = _post_mixer(x2d, o_a.reshape(b * s, -1), o_b.reshape(b * s, -1), w_out, mix_layer, gm,
                          norm_xattn[l], xa_wq, kv, xa_wo, norm_ffn[l], ffn_w_in, ffn_w_out, l,
                          final_norm, l == depth - 1, s)
    return x2d.reshape(b, s, d)
```
